```python
import math
import jax
import jax.numpy as jnp
from jax import lax
import numpy as np


D_MODEL = 1024
BATCH = 4
SEQ = 8192
DEPTH = 2

N_META = 16
BLOCK = 128
N_FRONT = BLOCK - N_META
ROPE_THETA = 500000.0
ROT_FRAC_DIV = 4
EPS = 1e-6
NEG_INF = -1e30
D_FF = 2816

DIFF_HEADS = 4
DIFF_HEAD_DIM = 64
DIFF_V_DIM = 2 * DIFF_HEAD_DIM
DIFF_EPS = 1e-5
MLA_HEADS = 4
MLA_NOPE = 128
MLA_ROPE = 64
MLA_V = 128
MLA_Q_RANK = 256
MLA_KV_RANK = 256
SWA_HEADS = 16
SWA_KV_HEADS = 2
SWA_GROUP = SWA_HEADS // SWA_KV_HEADS
SWA_HEAD_DIM = 64
WINDOW = 128

N_EVEN = (DEPTH + 1) // 2
N_ODD = DEPTH // 2

DIFF_QK_COLS = DIFF_HEADS * 2 * DIFF_HEAD_DIM
DIFF_V_COLS = DIFF_HEADS * DIFF_V_DIM
AB_SPLITS = (DIFF_QK_COLS,
             2 * DIFF_QK_COLS,
             2 * DIFF_QK_COLS + DIFF_V_COLS,
             2 * DIFF_QK_COLS + DIFF_V_COLS + MLA_Q_RANK,
             2 * DIFF_QK_COLS + DIFF_V_COLS + MLA_Q_RANK + MLA_KV_RANK)
AB_IN_COLS = AB_SPLITS[-1] + MLA_ROPE
AB_OUT_COLS = DIFF_V_COLS + MLA_HEADS * MLA_V
SWA_Q_COLS = SWA_HEADS * SWA_HEAD_DIM
SWA_KV_COLS = SWA_KV_HEADS * SWA_HEAD_DIM
SWA_IN_COLS = SWA_Q_COLS + 2 * SWA_KV_COLS

kernel_name = "hybrid_diffattn_mla_swa_macaron"


def rms_norm(x, g, eps=EPS):
    xf = x.astype(jnp.float32)
    y = xf * lax.rsqrt(jnp.mean(xf * xf, axis=-1, keepdims=True) + eps)
    return (y * g.astype(jnp.float32)).astype(x.dtype)


def rope_tables(pos, rot_dim):
    inv = ROPE_THETA ** (-jnp.arange(0, rot_dim, 2, dtype=jnp.float32) / rot_dim)
    ang = pos.astype(jnp.float32)[:, None] * inv[None, :]
    return jnp.cos(ang), jnp.sin(ang)


def apply_rope(x, cos, sin):
    half = x.shape[-1] // 2
    x1 = x[..., :half].astype(jnp.float32)
    x2 = x[..., half:].astype(jnp.float32)
    return jnp.concatenate([x1 * cos - x2 * sin, x2 * cos + x1 * sin], axis=-1).astype(x.dtype)


def partial_rope(x, cos, sin):
    r = x.shape[-1] // ROT_FRAC_DIV
    return jnp.concatenate([apply_rope(x[..., :r], cos, sin), x[..., r:]], axis=-1)


def swiglu(x, w_gate, w_up, w_down):
    return (jax.nn.silu(x @ w_gate) * (x @ w_up)) @ w_down


def to_blocks(a):
    b, l = a.shape[:2]
    return jnp.moveaxis(a.reshape((b, l // BLOCK, BLOCK) + a.shape[2:]), 1, 0)


def from_blocks(a):
    a = jnp.moveaxis(a, 0, 1)
    return a.reshape((a.shape[0], a.shape[1] * a.shape[2]) + a.shape[3:])


def causal_mask(q_idx, k_idx, k_valid):
    return (k_idx[None, :] <= q_idx[:, None]) & k_valid[None, :]


def diff_attention(q, k, v, lam, k_valid):
    L = q.shape[1]
    idx = jnp.arange(L)
    scale = DIFF_HEAD_DIM ** -0.5

    def block(args):
        qb, qi = args
        s = jnp.einsum('bqhcd,bkhcd->bhcqk', qb, k).astype(jnp.float32) * scale
        m = causal_mask(qi, idx, k_valid)
        p = jax.nn.softmax(jnp.where(m, s, NEG_INF), axis=-1)
        w = p[:, :, 0] - lam * p[:, :, 1]
        return jnp.einsum('bhqk,bkhe->bqhe', w.astype(v.dtype), v)

    return from_blocks(lax.map(block, (to_blocks(q), idx.reshape(-1, BLOCK))))


def mla_attention(qn, qr, kn, kr, v, k_valid):
    L = qn.shape[1]
    idx = jnp.arange(L)
    scale = (MLA_NOPE + MLA_ROPE) ** -0.5

    def block(args):
        qnb, qrb, qi = args
        s = (jnp.einsum('bqhd,bkhd->bhqk', qnb, kn)
             + jnp.einsum('bqhr,bkr->bhqk', qrb, kr)).astype(jnp.float32) * scale
        m = causal_mask(qi, idx, k_valid)
        p = jax.nn.softmax(jnp.where(m, s, NEG_INF), axis=-1)
        return jnp.einsum('bhqk,bkhe->bqhe', p.astype(v.dtype), v)

    return from_blocks(lax.map(block, (to_blocks(qn), to_blocks(qr), idx.reshape(-1, BLOCK))))


def swa_sink_attention(q, k, v, sinks, k_valid):
    L = q.shape[1]
    scale = SWA_HEAD_DIM ** -0.5
    kp = jnp.pad(k, ((0, 0), (BLOCK, 0), (0, 0), (0, 0)))
    vp = jnp.pad(v, ((0, 0), (BLOCK, 0), (0, 0), (0, 0)))
    validp = jnp.pad(k_valid, (BLOCK, 0))
    meta_k = k[:, N_FRONT:BLOCK]
    meta_v = v[:, N_FRONT:BLOCK]
    sink = sinks.reshape(SWA_KV_HEADS, SWA_GROUP)[None, :, :, None, None].astype(jnp.float32)
    band_off = jnp.arange(2 * BLOCK)
    q_off = jnp.arange(BLOCK)

    def block(args):
        qb, n = args
        start = n * BLOCK
        kb = lax.dynamic_slice_in_dim(kp, start, 2 * BLOCK, axis=1)
        vb = lax.dynamic_slice_in_dim(vp, start, 2 * BLOCK, axis=1)
        kvalid = lax.dynamic_slice_in_dim(validp, start, 2 * BLOCK, axis=0)
        q_idx = start + q_off
        k_idx = start - BLOCK + band_off
        keys = jnp.concatenate([meta_k, kb], axis=1)
        vals = jnp.concatenate([meta_v, vb], axis=1)
        s = jnp.einsum('bqhgd,bkhd->bhgqk', qb, keys).astype(jnp.float32) * scale
        dist = q_idx[:, None] - k_idx[None, :]
        is_meta = (k_idx >= N_FRONT) & (k_idx < BLOCK)
        band = kvalid[None, :] & (dist >= 0) & ((dist < WINDOW) | is_meta[None, :])
        extra = jnp.broadcast_to(n >= 2, (BLOCK, N_META))
        mask = jnp.concatenate([extra, band], axis=-1)
        s = jnp.where(mask, s, NEG_INF)
        m = jnp.maximum(jnp.max(s, axis=-1, keepdims=True), sink)
        e = jnp.exp(s - m)
        p = e / (jnp.sum(e, axis=-1, keepdims=True) + jnp.exp(sink - m))
        return jnp.einsum('bhgqk,bkhe->bqhge', p.astype(vals.dtype), vals)

    return from_blocks(lax.map(block, (to_blocks(q), jnp.arange(L // BLOCK))))


def ab_mixer(h, w_in, lq1, lk1, lq2, lk2, subln, q_norm, w_uq, kv_norm, w_ukv, w_out,
             lambda_init, cos_p, sin_p, cos_m, sin_m, k_valid):
    B, L, _ = h.shape
    proj = h @ w_in
    q_a, k_a, v_a, c_q, c_kv, k_r = jnp.split(proj, list(AB_SPLITS), axis=-1)
    q_a = partial_rope(q_a.reshape(B, L, DIFF_HEADS, 2, DIFF_HEAD_DIM), cos_p[:, None, None, :], sin_p[:, None, None, :])
    k_a = partial_rope(k_a.reshape(B, L, DIFF_HEADS, 2, DIFF_HEAD_DIM), cos_p[:, None, None, :], sin_p[:, None, None, :])
    v_a = v_a.reshape(B, L, DIFF_HEADS, DIFF_V_DIM)
    lam = (jnp.exp(jnp.sum(lq1.astype(jnp.float32) * lk1.astype(jnp.float32)))
           - jnp.exp(jnp.sum(lq2.astype(jnp.float32) * lk2.astype(jnp.float32))) + lambda_init)
    o_a = diff_attention(q_a, k_a, v_a, lam, k_valid)
    o_a = (rms_norm(o_a, subln, DIFF_EPS) * (1.0 - lambda_init)).reshape(B, L, DIFF_V_COLS)
    q_b = (rms_norm(c_q, q_norm) @ w_uq).reshape(B, L, MLA_HEADS, MLA_NOPE + MLA_ROPE)
    qn = q_b[..., :MLA_NOPE]
    qr = apply_rope(q_b[..., MLA_NOPE:], cos_m[:, None, :], sin_m[:, None, :])
    kv = (rms_norm(c_kv, kv_norm) @ w_ukv).reshape(B, L, MLA_HEADS, MLA_NOPE + MLA_V)
    kn = kv[..., :MLA_NOPE]
    v_b = kv[..., MLA_NOPE:]
    kr = apply_rope(k_r, cos_m, sin_m)
    o_b = mla_attention(qn, qr, kn, kr, v_b, k_valid).reshape(B, L, MLA_HEADS * MLA_V)
    return jnp.concatenate([o_a, o_b], axis=-1) @ w_out


def swa_mixer(h, w_qkv, b_qkv, sinks, w_out, b_out, cos_p, sin_p, k_valid):
    B, L, _ = h.shape
    proj = h @ w_qkv + b_qkv
    q, k, v = jnp.split(proj, [SWA_Q_COLS, SWA_Q_COLS + SWA_KV_COLS], axis=-1)
    q = partial_rope(q.reshape(B, L, SWA_HEADS, SWA_HEAD_DIM), cos_p[:, None, :], sin_p[:, None, :])
    k = partial_rope(k.reshape(B, L, SWA_KV_HEADS, SWA_HEAD_DIM), cos_p[:, None, :], sin_p[:, None, :])
    v = v.reshape(B, L, SWA_KV_HEADS, SWA_HEAD_DIM)
    q = q.reshape(B, L, SWA_KV_HEADS, SWA_GROUP, SWA_HEAD_DIM)
    o = swa_sink_attention(q, k, v, sinks, k_valid).reshape(B, L, SWA_Q_COLS)
    return o @ w_out + b_out


def setup_inputs(seed: int = 0) -> dict:
    key = jax.random.key(seed)
    ks = iter(jax.random.split(key, 40))

    def nrm(shape, scale):
        return jax.random.normal(next(ks), shape, jnp.float32) * scale

    def gain(shape):
        return 1.0 + nrm(shape, 0.02)

    D, F = D_MODEL, D_FF
    return {
        "x": nrm((BATCH, SEQ, D), 1.0),
        "meta_tokens": nrm((N_META, D), 1.0),
        "ffn1_norm": gain((DEPTH, D)),
        "ffn1_w_gate": nrm((DEPTH, D, F), D ** -0.5),
        "ffn1_w_up": nrm((DEPTH, D, F), D ** -0.5),
        "ffn1_w_down": nrm((DEPTH, F, D), F ** -0.5),
        "mix_norm": gain((DEPTH, D)),
        "ab_w_in": nrm((N_EVEN, D, AB_IN_COLS), D ** -0.5),
        "diff_lambda_q1": nrm((N_EVEN, DIFF_HEAD_DIM), 0.1),
        "diff_lambda_k1": nrm((N_EVEN, DIFF_HEAD_DIM), 0.1),
        "diff_lambda_q2": nrm((N_EVEN, DIFF_HEAD_DIM), 0.1),
        "diff_lambda_k2": nrm((N_EVEN, DIFF_HEAD_DIM), 0.1),
        "diff_subln": gain((N_EVEN, DIFF_V_DIM)),
        "mla_q_norm": gain((N_EVEN, MLA_Q_RANK)),
        "mla_w_uq": nrm((N_EVEN, MLA_Q_RANK, MLA_HEADS * (MLA_NOPE + MLA_ROPE)), MLA_Q_RANK ** -0.5),
        "mla_kv_norm": gain((N_EVEN, MLA_KV_RANK)),
        "mla_w_ukv": nrm((N_EVEN, MLA_KV_RANK, MLA_HEADS * (MLA_NOPE + MLA_V)), MLA_KV_RANK ** -0.5),
        "ab_w_out": nrm((N_EVEN, AB_OUT_COLS, D), AB_OUT_COLS ** -0.5),
        "swa_w_qkv": nrm((N_ODD, D, SWA_IN_COLS), D ** -0.5),
        "swa_b_qkv": nrm((N_ODD, SWA_IN_COLS), 0.02),
        "swa_sinks": nrm((N_ODD, SWA_HEADS), 0.5),
        "swa_w_out": nrm((N_ODD, SWA_Q_COLS, D), SWA_Q_COLS ** -0.5),
        "swa_b_out": nrm((N_ODD, D), 0.02),
        "ffn2_norm": gain((DEPTH, D)),
        "ffn2_w_gate": nrm((DEPTH, D, F), D ** -0.5),
        "ffn2_w_up": nrm((DEPTH, D, F), D ** -0.5),
        "ffn2_w_down": nrm((DEPTH, F, D), F ** -0.5),
        "final_norm": gain((D,)),
    }


def reference(x, meta_tokens, ffn1_norm, ffn1_w_gate, ffn1_w_up, ffn1_w_down, mix_norm,
              ab_w_in, diff_lambda_q1, diff_lambda_k1, diff_lambda_q2, diff_lambda_k2, diff_subln,
              mla_q_norm, mla_w_uq, mla_kv_norm, mla_w_ukv, ab_w_out,
              swa_w_qkv, swa_b_qkv, swa_sinks, swa_w_out, swa_b_out,
              ffn2_norm, ffn2_w_gate, ffn2_w_up, ffn2_w_down, final_norm):
    B = x.shape[0]
    meta = jnp.broadcast_to(meta_tokens[None].astype(x.dtype), (B, N_META, D_MODEL))
    h = jnp.concatenate([jnp.zeros((B, N_FRONT, D_MODEL), x.dtype), meta, x], axis=1)
    L = h.shape[1]
    idx = jnp.arange(L)
    k_valid = idx >= N_FRONT
    pos = jnp.maximum(idx - N_FRONT, 0)
    cos_p, sin_p = rope_tables(pos, DIFF_HEAD_DIM // ROT_FRAC_DIV)
    cos_m, sin_m = rope_tables(pos, MLA_ROPE)
    for l in range(DEPTH):
        h = h + 0.5 * swiglu(rms_norm(h, ffn1_norm[l]), ffn1_w_gate[l], ffn1_w_up[l], ffn1_w_down[l])
        hn = rms_norm(h, mix_norm[l])
        if l % 2 == 0:
            e = l // 2
            lambda_init = 0.8 - 0.6 * math.exp(-0.3 * l)
            h = h + ab_mixer(hn, ab_w_in[e], diff_lambda_q1[e], diff_lambda_k1[e], diff_lambda_q2[e],
                             diff_lambda_k2[e], diff_subln[e], mla_q_norm[e], mla_w_uq[e],
                             mla_kv_norm[e], mla_w_ukv[e], ab_w_out[e], lambda_init,
                             cos_p, sin_p, cos_m, sin_m, k_valid)
        else:
            o = l // 2
            h = h + swa_mixer(hn, swa_w_qkv[o], swa_b_qkv[o], swa_sinks[o], swa_w_out[o], swa_b_out[o],
                              cos_p, sin_p, k_valid)
        h = h + 0.5 * swiglu(rms_norm(h, ffn2_norm[l]), ffn2_w_gate[l], ffn2_w_up[l], ffn2_w_down[l])
    h = rms_norm(h, final_norm)
    return h[:, BLOCK:]
```

```python
import functools
import math

import jax
import jax.numpy as jnp
from jax import lax
from jax.experimental import pallas as pl
from jax.experimental.pallas import tpu as pltpu

F32 = jnp.float32
BF16 = jnp.bfloat16

D_MODEL = 1024
N_META = 16
BLOCK = 128
N_FRONT = BLOCK - N_META
ROPE_THETA = 500000.0
EPS = 1e-6
NEG_INF = -1e30
D_FF = 2816

DIFF_HEADS = 4
DIFF_HEAD_DIM = 64
DIFF_V_DIM = 2 * DIFF_HEAD_DIM
DIFF_EPS = 1e-5
MLA_HEADS = 4
MLA_NOPE = 128
MLA_ROPE = 64
MLA_V = 128
MLA_Q_RANK = 256
MLA_KV_RANK = 256
SWA_HEADS = 16
SWA_KV_HEADS = 2
SWA_GROUP = SWA_HEADS // SWA_KV_HEADS
SWA_HEAD_DIM = 64
WINDOW = 128

PART_ROT = DIFF_HEAD_DIM // 4
LANES = 128
VMEM_LIMIT = 56 * 1024 * 1024

FFN_TM = 640
FFN_FC = 256
PROJ_TM = 640
ATT_T = 512
SWA_TQ = 256


def _cparams(n_axes):
    return pltpu.CompilerParams(dimension_semantics=("arbitrary",) * n_axes,
                                vmem_limit_bytes=VMEM_LIMIT)


def _resident(shape):
    nd = len(shape)
    return pl.BlockSpec(shape, lambda *_: (0,) * nd, pipeline_mode=pl.Buffered(1))


def _rms(x, g, eps):
    return x * lax.rsqrt(jnp.mean(x * x, axis=-1, keepdims=True) + eps) * g


def _dot(a, b):
    return jnp.dot(a, b, preferred_element_type=F32)


def _dot_nt(a, b):
    return lax.dot_general(a, b, (((1,), (1,)), ((), ())), preferred_element_type=F32)


def _ffn_kernel(x_ref, g_ref, wg_ref, wu_ref, wd_ref, *rest, final):
    if final:
        fg_ref, o_ref, act_ref = rest
    else:
        o_ref, act_ref = rest
    x = x_ref[...].reshape(x_ref.shape[-2:])
    xn = _rms(x, g_ref[...], EPS).astype(BF16)
    for c in range(D_FF // FFN_FC):
        sl = slice(c * FFN_FC, (c + 1) * FFN_FC)
        g = _dot(xn, wg_ref[:, sl])
        u = _dot(xn, wu_ref[:, sl])
        act_ref[:, sl] = (g * (1.0 / (1.0 + jnp.exp(-g))) * u).astype(BF16)
    y = x + 0.5 * _dot(act_ref[...], wd_ref[...])
    if final:
        y = _rms(y, fg_ref[...], EPS)
    o_ref[...] = y.reshape(o_ref.shape)


def _ffn(h, g, wg, wu, wd):
    n = h.shape[0]
    tm = FFN_TM
    assert n % tm == 0
    row = pl.BlockSpec((tm, D_MODEL), lambda i: (i, 0))
    return pl.pallas_call(
        functools.partial(_ffn_kernel, final=False),
        grid=(n // tm,),
        in_specs=[row, _resident((1, D_MODEL)), _resident((D_MODEL, D_FF)), _resident((D_MODEL, D_FF)),
                  _resident((D_FF, D_MODEL))],
        out_specs=row,
        out_shape=jax.ShapeDtypeStruct((n, D_MODEL), F32),
        scratch_shapes=[pltpu.VMEM((tm, D_FF), BF16)],
        compiler_params=_cparams(1),
        name="ffn",
    )(h, g, wg, wu, wd)


def _ffn_final(h3, g, wg, wu, wd, fg, seq):
    b = h3.shape[0]
    tm = ATT_T
    assert seq % tm == 0
    row = pl.BlockSpec((1, tm, D_MODEL), lambda bb, i: (bb, i, 0))
    return pl.pallas_call(
        functools.partial(_ffn_kernel, final=True),
        grid=(b, seq // tm),
        in_specs=[row, _resident((1, D_MODEL)), _resident((D_MODEL, D_FF)), _resident((D_MODEL, D_FF)),
                  _resident((D_FF, D_MODEL)), _resident((1, D_MODEL))],
        out_specs=row,
        out_shape=jax.ShapeDtypeStruct((b, seq, D_MODEL), F32),
        scratch_shapes=[pltpu.VMEM((tm, D_FF), BF16)],
        compiler_params=_cparams(2),
        name="ffn_final",
    )(h3, g, wg, wu, wd, fg)


def _rope(y, c, s, half):
    w = y.shape[1]
    reps = w // LANES
    if reps > 1:
        c = jnp.concatenate([c] * reps, axis=1)
        s = jnp.concatenate([s] * reps, axis=1)
    lane = lax.broadcasted_iota(jnp.int32, y.shape, 1)
    first = (lane & 63) < half
    partner = jnp.where(first, pltpu.roll(y, w - half, 1), pltpu.roll(y, half, 1))
    return y * c + partner * s


def _rope_tables(pos, rot_dim):
    half = rot_dim // 2
    inv = ROPE_THETA ** (-jnp.arange(0, rot_dim, 2, dtype=F32) / rot_dim)
    ang = pos.astype(F32)[:, None] * inv[None, :]
    cos, sin = jnp.cos(ang), jnp.sin(ang)
    n = pos.shape[0]
    pad = 64 - rot_dim
    c64 = jnp.concatenate([cos, cos, jnp.ones((n, pad), F32)], axis=1)
    s64 = jnp.concatenate([-sin, sin, jnp.zeros((n, pad), F32)], axis=1)
    return jnp.concatenate([c64, c64], axis=1), jnp.concatenate([s64, s64], axis=1)


AB_IN_PAD = 3 * 512 + MLA_Q_RANK + MLA_KV_RANK + LANES


def _ab_proj_kernel(h_ref, g_ref, win_ref, qn_ref, wuq_ref, kvn_ref, wukv_ref, ca_ref, sa_ref, cm_ref, sm_ref,
                    qa_ref, ka_ref, va_ref, qb_ref, kb_ref, vb_ref):
    xn = _rms(h_ref[0], g_ref[...], EPS).astype(BF16)
    ca, sa, cm, sm = ca_ref[...], sa_ref[...], cm_ref[...], sm_ref[...]
    half_a = PART_ROT // 2
    half_m = MLA_ROPE // 2
    qa = _rope(_dot(xn, win_ref[:, 0:512]), ca, sa, half_a)
    qa_ref[0] = (qa * DIFF_HEAD_DIM ** -0.5).astype(BF16)
    ka_ref[0] = _rope(_dot(xn, win_ref[:, 512:1024]), ca, sa, half_a).astype(BF16)
    va_ref[0] = _dot(xn, win_ref[:, 1024:1536]).astype(BF16)
    cq = _dot(xn, win_ref[:, 1536:1792])
    ckv = _dot(xn, win_ref[:, 1792:2048])
    kr = _rope(_dot(xn, win_ref[:, 2048:2176]), cm, sm, half_m).astype(BF16)
    qb = _dot(_rms(cq, qn_ref[...], EPS).astype(BF16), wuq_ref[...])
    scale = (MLA_NOPE + MLA_ROPE) ** -0.5
    for hh in range(MLA_HEADS):
        o = 2 * LANES * hh
        qb_ref[0, :, o:o + LANES] = (qb[:, o:o + LANES] * scale).astype(BF16)
        qb_ref[0, :, o + LANES:o + 2 * LANES] = (_rope(qb[:, o + LANES:o + 2 * LANES], cm, sm, half_m) * scale).astype(BF16)
    kv = _dot(_rms(ckv, kvn_ref[...], EPS).astype(BF16), wukv_ref[...])
    for hh in range(MLA_HEADS):
        o = 2 * LANES * hh
        kb_ref[0, :, o:o + LANES] = kv[:, LANES * hh:LANES * (hh + 1)].astype(BF16)
        kb_ref[0, :, o + LANES:o + 2 * LANES] = kr
    vb_ref[0] = kv[:, MLA_HEADS * MLA_NOPE:].astype(BF16)


def _ab_proj(h3, g, win, qn, wuq, kvn, wukv, tabs):
    b, l, _ = h3.shape
    tm = PROJ_TM
    assert l % tm == 0
    row = lambda w: pl.BlockSpec((1, tm, w), lambda j, bb: (bb, j, 0))
    tab = pl.BlockSpec((tm, LANES), lambda j, bb: (j, 0))
    outw = (512, 512, 512, 1024, 1024, 512)
    return pl.pallas_call(
        _ab_proj_kernel,
        grid=(l // tm, b),
        in_specs=[row(D_MODEL), _resident((1, D_MODEL)), _resident((D_MODEL, AB_IN_PAD)),
                  _resident((1, MLA_Q_RANK)), _resident((MLA_Q_RANK, 1024)),
                  _resident((1, MLA_KV_RANK)), _resident((MLA_KV_RANK, 1024)), tab, tab, tab, tab],
        out_specs=[row(w) for w in outw],
        out_shape=[jax.ShapeDtypeStruct((b, l, w), BF16) for w in outw],
        compiler_params=_cparams(2),
        name="ab_proj",
    )(h3, g, win, qn, wuq, kvn, wukv, *tabs)


def _ab_out_kernel(h_ref, oa_ref, ob_ref, wa_ref, wb_ref, o_ref):
    o_ref[...] = h_ref[...] + _dot(oa_ref[...], wa_ref[...]) + _dot(ob_ref[...], wb_ref[...])


def _ab_out(h, oa, ob, wa, wb):
    n = h.shape[0]
    tm = FFN_TM
    row = lambda w: pl.BlockSpec((tm, w), lambda i: (i, 0))
    return pl.pallas_call(
        _ab_out_kernel,
        grid=(n // tm,),
        in_specs=[row(D_MODEL), row(512), row(512), _resident((512, D_MODEL)), _resident((512, D_MODEL))],
        out_specs=row(D_MODEL),
        out_shape=jax.ShapeDtypeStruct((n, D_MODEL), F32),
        compiler_params=_cparams(1),
        name="ab_out",
    )(h, oa, ob, wa, wb)


def _flash_step(q, k, v, m_ref, l_ref, acc_ref, mask, first):
    s = _dot_nt(q, k)
    if mask is not None:
        s = jnp.where(mask, s, NEG_INF)
    smax = jnp.max(s, axis=-1, keepdims=True)
    if first:
        m_new = smax
        p = jnp.exp(s - m_new)
        l_ref[...] = jnp.sum(p, axis=-1, keepdims=True)
        acc_ref[...] = _dot(p.astype(BF16), v)
    else:
        m_prev = m_ref[...]
        m_new = jnp.maximum(m_prev, smax)
        alpha = jnp.exp(m_prev - m_new)
        p = jnp.exp(s - m_new)
        l_ref[...] = alpha * l_ref[...] + jnp.sum(p, axis=-1, keepdims=True)
        acc_ref[...] = alpha * acc_ref[...] + _dot(p.astype(BF16), v)
    m_ref[...] = m_new


def _causal_attend(q, k_ref, v_ref, m_ref, l_ref, acc_ref, *, t, reps, seq, front):
    rows = q.shape[0]
    kf = k_ref[0, seq:seq + BLOCK, :]
    vf = v_ref[0, seq:seq + BLOCK, :]
    col = lax.broadcasted_iota(jnp.int32, (rows, BLOCK), 1)
    valid = col >= N_FRONT
    if front:
        r = lax.broadcasted_iota(jnp.int32, (rows, BLOCK), 0)
        if reps > 1:
            r = r & (t - 1)
        _flash_step(q, kf, vf, m_ref, l_ref, acc_ref, valid & (col <= r), True)
        return
    i = pl.program_id(2)
    _flash_step(q, kf, vf, m_ref, l_ref, acc_ref, valid, True)

    def body(j, carry):
        st = pl.multiple_of(j * t, t)
        _flash_step(q, k_ref[0, pl.ds(st, t), :], v_ref[0, pl.ds(st, t), :], m_ref, l_ref, acc_ref, None, False)
        return carry

    lax.fori_loop(0, i, body, 0)
    st = pl.multiple_of(i * t, t)
    r = lax.broadcasted_iota(jnp.int32, (rows, t), 0)
    if reps > 1:
        r = r & (t - 1)
    c = lax.broadcasted_iota(jnp.int32, (rows, t), 1)
    _flash_step(q, k_ref[0, pl.ds(st, t), :], v_ref[0, pl.ds(st, t), :], m_ref, l_ref, acc_ref, c <= r, False)


def _diff_kernel(q_ref, k_ref, v_ref, lq1_ref, lk1_ref, lq2_ref, lk2_ref, sub_ref, *rest,
                 t, seq, front, lambda_init):
    if front:
        _, o_ref, m_ref, l_ref, acc_ref = rest
    else:
        o_ref, m_ref, l_ref, acc_ref = rest
    q = q_ref[0]
    lane = lax.broadcasted_iota(jnp.int32, q.shape, 1)
    zero = jnp.zeros_like(q)
    qs = jnp.concatenate([jnp.where(lane < DIFF_HEAD_DIM, q, zero), jnp.where(lane >= DIFF_HEAD_DIM, q, zero)], axis=0)
    _causal_attend(qs, k_ref, v_ref, m_ref, l_ref, acc_ref, t=t, reps=2, seq=seq, front=front)
    o = acc_ref[...] / l_ref[...]
    lam = (jnp.exp(jnp.sum(lq1_ref[...] * lk1_ref[...], keepdims=True))
           - jnp.exp(jnp.sum(lq2_ref[...] * lk2_ref[...], keepdims=True)) + lambda_init)
    w = o[:t] - lam * o[t:]
    o_ref[0] = (_rms(w, sub_ref[...], DIFF_EPS) * (1.0 - lambda_init)).astype(BF16)


def _mla_kernel(q_ref, k_ref, v_ref, *rest, t, seq, front):
    if front:
        _, o_ref, m_ref, l_ref, acc_ref = rest
    else:
        o_ref, m_ref, l_ref, acc_ref = rest
    _causal_attend(q_ref[0], k_ref, v_ref, m_ref, l_ref, acc_ref, t=t, reps=1, seq=seq, front=front)
    o_ref[0] = (acc_ref[...] / l_ref[...]).astype(BF16)


def _causal_attention(kernel, q, k, v, extra, *, heads, dk, reps, seq, name):
    b, l, _ = q.shape
    dv = LANES
    outs = None
    for front in (False, True):
        t = BLOCK if front else ATT_T
        if front:
            grid = (b, heads, 1)
            qmap = lambda bb, hh, i: (bb, seq // BLOCK, hh)
        else:
            grid = (b, heads, seq // t)
            qmap = lambda bb, hh, i: (bb, i, hh)
        kvmap = lambda bb, hh, i: (bb, 0, hh)
        in_specs = [pl.BlockSpec((1, t, dk), qmap), pl.BlockSpec((1, l, dk), kvmap), pl.BlockSpec((1, l, dv), kvmap)]
        in_specs += [_resident(e.shape) for e in extra]
        args = [q, k, v, *extra]
        aliases = {}
        if front:
            in_specs.append(pl.BlockSpec(memory_space=pl.ANY))
            aliases = {len(args): 0}
            args.append(outs)
        outs = pl.pallas_call(
            functools.partial(kernel, t=t, seq=seq, front=front),
            grid=grid,
            in_specs=in_specs,
            out_specs=pl.BlockSpec((1, t, dv), qmap),
            out_shape=jax.ShapeDtypeStruct((b, l, heads * dv), BF16),
            scratch_shapes=[pltpu.VMEM((reps * t, 1), F32), pltpu.VMEM((reps * t, 1), F32),
                            pltpu.VMEM((reps * t, dv), F32)],
            input_output_aliases=aliases,
            compiler_params=_cparams(3),
            name=name + ("_front" if front else ""),
        )(*args)
    return outs


SWA_IN_PAD = SWA_HEADS * SWA_HEAD_DIM + 2 * SWA_KV_HEADS * LANES


def _swa_proj_kernel(h_ref, g_ref, w_ref, b_ref, ca_ref, sa_ref, q_ref, k_ref, v_ref):
    xn = _rms(h_ref[0], g_ref[...], EPS).astype(BF16)
    ca, sa = ca_ref[...], sa_ref[...]
    half = PART_ROT // 2
    nq = SWA_HEADS * SWA_HEAD_DIM
    nk = SWA_KV_HEADS * LANES
    q = _rope(_dot(xn, w_ref[:, 0:nq]) + b_ref[:, 0:nq], ca, sa, half)
    q_ref[0] = (q * SWA_HEAD_DIM ** -0.5).astype(BF16)
    k_ref[0] = _rope(_dot(xn, w_ref[:, nq:nq + nk]) + b_ref[:, nq:nq + nk], ca, sa, half).astype(BF16)
    v_ref[0] = (_dot(xn, w_ref[:, nq + nk:]) + b_ref[:, nq + nk:]).astype(BF16)


def _swa_proj(h3, g, w, bias, tabs):
    b, l, _ = h3.shape
    tm = PROJ_TM
    row = lambda wd: pl.BlockSpec((1, tm, wd), lambda j, bb: (bb, j, 0))
    tab = pl.BlockSpec((tm, LANES), lambda j, bb: (j, 0))
    outw = (SWA_HEADS * SWA_HEAD_DIM, SWA_KV_HEADS * LANES, SWA_KV_HEADS * LANES)
    return pl.pallas_call(
        _swa_proj_kernel,
        grid=(l // tm, b),
        in_specs=[row(D_MODEL), _resident((1, D_MODEL)), _resident((D_MODEL, SWA_IN_PAD)),
                  _resident((1, SWA_IN_PAD)), tab, tab],
        out_specs=[row(w_) for w_ in outw],
        out_shape=[jax.ShapeDtypeStruct((b, l, w_), BF16) for w_ in outw],
        compiler_params=_cparams(2),
        name="swa_proj",
    )(h3, g, w, bias, *tabs)


def _swa_kernel(sinks_ref, q_ref, k_ref, v_ref, *rest, tq, seq, front):
    if front:
        _, o_ref, p_ref = rest
    else:
        o_ref, p_ref = rest
    if front:
        nk = BLOCK
        r = lax.broadcasted_iota(jnp.int32, (tq, nk), 0)
        c = lax.broadcasted_iota(jnp.int32, (tq, nk), 1)
        mask = (c >= N_FRONT) & (c <= r)
    else:
        i = pl.program_id(1)
        bw = tq + WINDOW
        nk = BLOCK + bw
        bs = pl.multiple_of(jnp.maximum(i * tq - WINDOW, 0), BLOCK)
        r = lax.broadcasted_iota(jnp.int32, (tq, nk), 0)
        c = lax.broadcasted_iota(jnp.int32, (tq, nk), 1)
        dist = (i * tq + r) - (bs + c - BLOCK)
        mask = ((c < BLOCK) & (c >= N_FRONT)) | ((c >= BLOCK) & (dist >= 0) & (dist < WINDOW))
    lane = lax.broadcasted_iota(jnp.int32, (tq, LANES), 1)
    low = lane < SWA_HEAD_DIM
    high = lane >= SWA_HEAD_DIM
    for g in range(SWA_KV_HEADS):
        gs = slice(g * LANES, (g + 1) * LANES)
        kf = k_ref[0, seq:seq + BLOCK, gs]
        vf = v_ref[0, seq:seq + BLOCK, gs]
        if front:
            kc, vc = kf, vf
        else:
            kc = jnp.concatenate([kf, k_ref[0, pl.ds(bs, bw), gs]], axis=0)
            vc = jnp.concatenate([vf, v_ref[0, pl.ds(bs, bw), gs]], axis=0)
        qz = []
        for rr in range(SWA_GROUP):
            hd = g * SWA_GROUP + rr
            slab = q_ref[0, :, (hd // 2) * LANES:(hd // 2 + 1) * LANES]
            qz.append(jnp.where(low if hd % 2 == 0 else high, slab, jnp.zeros_like(slab)))
        s = _dot_nt(jnp.concatenate(qz, axis=0), kc)
        for rr in range(SWA_GROUP):
            sink = sinks_ref[g * SWA_GROUP + rr]
            sr = jnp.where(mask, s[rr * tq:(rr + 1) * tq], NEG_INF)
            m = jnp.maximum(jnp.max(sr, axis=-1, keepdims=True), sink)
            e = jnp.exp(sr - m)
            den = jnp.sum(e, axis=-1, keepdims=True) + jnp.exp(sink - m)
            p_ref[rr * tq:(rr + 1) * tq, :] = (e / den).astype(BF16)
        o = _dot(p_ref[...], vc)
        for pp in range(SWA_GROUP // 2):
            ev = o[(2 * pp) * tq:(2 * pp + 1) * tq]
            od = o[(2 * pp + 1) * tq:(2 * pp + 2) * tq]
            col = (g * SWA_GROUP // 2 + pp) * LANES
            o_ref[0, :, col:col + LANES] = jnp.where(low, ev, od).astype(BF16)


def _swa_attention(q, k, v, sinks, *, seq):
    b, l, _ = q.shape
    nq = SWA_HEADS * SWA_HEAD_DIM
    outs = None
    for front in (False, True):
        tq = BLOCK if front else SWA_TQ
        nk = BLOCK if front else 2 * BLOCK + tq
        if front:
            grid = (b, 1)
            qmap = lambda bb, i: (bb, seq // BLOCK, 0)
        else:
            grid = (b, seq // tq)
            qmap = lambda bb, i: (bb, i, 0)
        kvmap = lambda bb, i: (bb, 0, 0)
        in_specs = [pl.BlockSpec(memory_space=pltpu.SMEM), pl.BlockSpec((1, tq, nq), qmap),
                    pl.BlockSpec((1, l, k.shape[2]), kvmap), pl.BlockSpec((1, l, v.shape[2]), kvmap)]
        args = [sinks, q, k, v]
        aliases = {}
        if front:
            in_specs.append(pl.BlockSpec(memory_space=pl.ANY))
            aliases = {len(args): 0}
            args.append(outs)
        outs = pl.pallas_call(
            functools.partial(_swa_kernel, tq=tq, seq=seq, front=front),
            grid=grid,
            in_specs=in_specs,
            out_specs=pl.BlockSpec((1, tq, nq), qmap),
            out_shape=jax.ShapeDtypeStruct((b, l, nq), BF16),
            scratch_shapes=[pltpu.VMEM((SWA_GROUP * tq, nk), BF16)],
            input_output_aliases=aliases,
            compiler_params=_cparams(2),
            name="swa_attn" + ("_front" if front else ""),
        )(*args)
    return outs


def _swa_out_kernel(h_ref, o_ref_in, w_ref, b_ref, o_ref):
    o_ref[...] = h_ref[...] + _dot(o_ref_in[...], w_ref[...]) + b_ref[...]


def _swa_out(h, o, w, bias):
    n = h.shape[0]
    tm = FFN_TM
    row = lambda wd: pl.BlockSpec((tm, wd), lambda i: (i, 0))
    return pl.pallas_call(
        _swa_out_kernel,
        grid=(n // tm,),
        in_specs=[row(D_MODEL), row(D_MODEL), _resident((D_MODEL, D_MODEL)), _resident((1, D_MODEL))],
        out_specs=row(D_MODEL),
        out_shape=jax.ShapeDtypeStruct((n, D_MODEL), F32),
        compiler_params=_cparams(1),
        name="swa_out",
    )(h, o, w, bias)


def _ab_weights(w_in, w_uq, w_ukv):
    win = jnp.pad(w_in, ((0, 0), (0, AB_IN_PAD - w_in.shape[1]))).astype(BF16)
    wq = w_uq.reshape(MLA_Q_RANK, MLA_HEADS, MLA_NOPE + MLA_ROPE)
    wq = jnp.pad(wq, ((0, 0), (0, 0), (0, 2 * LANES - MLA_NOPE - MLA_ROPE))).reshape(MLA_Q_RANK, MLA_HEADS * 2 * LANES)
    wkv = w_ukv.reshape(MLA_KV_RANK, MLA_HEADS, 2, MLA_NOPE).transpose(0, 2, 1, 3).reshape(MLA_KV_RANK, -1)
    return win, wq.astype(BF16), wkv.astype(BF16)


def _swa_weights(w_qkv, b_qkv):
    nq = SWA_HEADS * SWA_HEAD_DIM
    nkv = SWA_KV_HEADS * SWA_HEAD_DIM

    def dup(a):
        a = a.reshape(a.shape[:-1] + (SWA_KV_HEADS, 1, SWA_HEAD_DIM))
        return jnp.broadcast_to(a, a.shape[:-2] + (2, SWA_HEAD_DIM)).reshape(a.shape[:-3] + (SWA_KV_HEADS * LANES,))

    parts = lambda a: jnp.concatenate([a[..., :nq], dup(a[..., nq:nq + nkv]), dup(a[..., nq + nkv:])], axis=-1)
    return parts(w_qkv).astype(BF16), parts(b_qkv)[None, :]


def kernel(x, meta_tokens, ffn1_norm, ffn1_w_gate, ffn1_w_up, ffn1_w_down, mix_norm, ab_w_in, diff_lambda_q1, diff_lambda_k1, diff_lambda_q2, diff_lambda_k2, diff_subln, mla_q_norm, mla_w_uq, mla_kv_norm, mla_w_ukv, ab_w_out, swa_w_qkv, swa_b_qkv, swa_sinks, swa_w_out, swa_b_out, ffn2_norm, ffn2_w_gate, ffn2_w_up, ffn2_w_down, final_norm):
    b, seq, d = x.shape
    depth = ffn1_norm.shape[0]
    l = seq + BLOCK
    n = b * l
    meta = jnp.broadcast_to(meta_tokens[None].astype(x.dtype), (b, N_META, d))
    h = jnp.concatenate([x, jnp.zeros((b, N_FRONT, d), x.dtype), meta], axis=1).reshape(n, d)

    pos = jnp.concatenate([jnp.arange(seq) + N_META, jnp.maximum(jnp.arange(BLOCK) - N_FRONT, 0)])
    tabs_p = _rope_tables(pos, PART_ROT)
    tabs_m = _rope_tables(pos, MLA_ROPE)
    row2 = lambda a: a.reshape(1, -1)

    out = None
    for ly in range(depth):
        h = _ffn(h, row2(ffn1_norm[ly]), ffn1_w_gate[ly].astype(BF16), ffn1_w_up[ly].astype(BF16),
                 ffn1_w_down[ly].astype(BF16))
        h3 = h.reshape(b, l, d)
        if ly % 2 == 0:
            e = ly // 2
            lambda_init = 0.8 - 0.6 * math.exp(-0.3 * ly)
            win, wuq, wukv = _ab_weights(ab_w_in[e], mla_w_uq[e], mla_w_ukv[e])
            qa, ka, va, qb, kb, vb = _ab_proj(h3, row2(mix_norm[ly]), win, row2(mla_q_norm[e]), wuq,
                                              row2(mla_kv_norm[e]), wukv, tabs_p + tabs_m)
            extra = [row2(diff_lambda_q1[e]), row2(diff_lambda_k1[e]), row2(diff_lambda_q2[e]),
                     row2(diff_lambda_k2[e]), row2(diff_subln[e])]
            oa = _causal_attention(functools.partial(_diff_kernel, lambda_init=lambda_init), qa, ka, va, extra,
                                   heads=DIFF_HEADS, dk=LANES, reps=2, seq=seq, name="diff_attn")
            ob = _causal_attention(_mla_kernel, qb, kb, vb, [], heads=MLA_HEADS, dk=2 * LANES, reps=1, seq=seq,
                                   name="mla_attn")
            wo = ab_w_out[e].astype(BF16)
            h = _ab_out(h, oa.reshape(n, -1), ob.reshape(n, -1), wo[:512], wo[512:])
        else:
            o = ly // 2
            w, bias = _swa_weights(swa_w_qkv[o], swa_b_qkv[o])
            q, k, v = _swa_proj(h3, row2(mix_norm[ly]), w, bias, tabs_p)
            att = _swa_attention(q, k, v, swa_sinks[o], seq=seq)
            h = _swa_out(h, att.reshape(n, -1), swa_w_out[o].astype(BF16), row2(swa_b_out[o]))
        wg, wu, wd = ffn2_w_gate[ly].astype(BF16), ffn2_w_up[ly].astype(BF16), ffn2_w_down[ly].astype(BF16)
        if ly == depth - 1:
            out = _ffn_final(h.reshape(b, l, d), row2(ffn2_norm[ly]), wg, wu, wd, row2(final_norm), seq)
        else:
            h = _ffn(h, row2(ffn2_norm[ly]), wg, wu, wd)
    return out
```

```python
import functools
import math

import jax
import jax.numpy as jnp
from jax import lax
from jax.experimental import pallas as pl
from jax.experimental.pallas import tpu as pltpu

F32 = jnp.float32
BF16 = jnp.bfloat16

D_MODEL = 1024
N_META = 16
BLOCK = 128
N_FRONT = BLOCK - N_META
ROPE_THETA = 500000.0
EPS = 1e-6
NEG_INF = -1e30
D_FF = 2816

DIFF_HEADS = 4
DIFF_HEAD_DIM = 64
DIFF_V_DIM = 2 * DIFF_HEAD_DIM
DIFF_EPS = 1e-5
MLA_HEADS = 4
MLA_NOPE = 128
MLA_ROPE = 64
MLA_V = 128
MLA_Q_RANK = 256
MLA_KV_RANK = 256
SWA_HEADS = 16
SWA_KV_HEADS = 2
SWA_GROUP = SWA_HEADS // SWA_KV_HEADS
SWA_HEAD_DIM = 64
WINDOW = 128

PART_ROT = DIFF_HEAD_DIM // 4
LANES = 128
VMEM_LIMIT = 56 * 1024 * 1024

FFN_TM = 640
FFN_FC = 256
PROJ_TM = 640
ATT_T = 512
ATT_TK = 512
MLA_TQ = 512
LOG2E = 1.4426950408889634
SWA_TQ = 256


def _cparams(n_axes):
    return pltpu.CompilerParams(dimension_semantics=("arbitrary",) * n_axes,
                                vmem_limit_bytes=VMEM_LIMIT)


def _resident(shape):
    nd = len(shape)
    return pl.BlockSpec(shape, lambda *_: (0,) * nd, pipeline_mode=pl.Buffered(1))


def _rms(x, g, eps):
    return x * lax.rsqrt(jnp.mean(x * x, axis=-1, keepdims=True) + eps) * g


def _dot(a, b):
    return jnp.dot(a, b, preferred_element_type=F32)


def _dot_nt(a, b):
    return lax.dot_general(a, b, (((1,), (1,)), ((), ())), preferred_element_type=F32)


def _ffn_kernel(x_ref, g_ref, wg_ref, wu_ref, wd_ref, *rest, final):
    if final:
        fg_ref, o_ref, act_ref = rest
    else:
        o_ref, act_ref = rest
    x = x_ref[...].reshape(x_ref.shape[-2:])
    xn = _rms(x, g_ref[...], EPS).astype(BF16)
    for c in range(D_FF // FFN_FC):
        sl = slice(c * FFN_FC, (c + 1) * FFN_FC)
        g = _dot(xn, wg_ref[:, sl])
        u = _dot(xn, wu_ref[:, sl])
        act_ref[:, sl] = (g * (1.0 / (1.0 + jnp.exp(-g))) * u).astype(BF16)
    y = x + 0.5 * _dot(act_ref[...], wd_ref[...])
    if final:
        y = _rms(y, fg_ref[...], EPS)
    o_ref[...] = y.reshape(o_ref.shape)


def _ffn(h, g, wg, wu, wd):
    n = h.shape[0]
    tm = FFN_TM
    assert n % tm == 0
    row = pl.BlockSpec((tm, D_MODEL), lambda i: (i, 0))
    return pl.pallas_call(
        functools.partial(_ffn_kernel, final=False),
        grid=(n // tm,),
        in_specs=[row, _resident((1, D_MODEL)), _resident((D_MODEL, D_FF)), _resident((D_MODEL, D_FF)),
                  _resident((D_FF, D_MODEL))],
        out_specs=row,
        out_shape=jax.ShapeDtypeStruct((n, D_MODEL), F32),
        scratch_shapes=[pltpu.VMEM((tm, D_FF), BF16)],
        compiler_params=_cparams(1),
        name="ffn",
    )(h, g, wg, wu, wd)


def _ffn_final(h3, g, wg, wu, wd, fg, seq):
    b = h3.shape[0]
    tm = ATT_T
    assert seq % tm == 0
    row = pl.BlockSpec((1, tm, D_MODEL), lambda bb, i: (bb, i, 0))
    return pl.pallas_call(
        functools.partial(_ffn_kernel, final=True),
        grid=(b, seq // tm),
        in_specs=[row, _resident((1, D_MODEL)), _resident((D_MODEL, D_FF)), _resident((D_MODEL, D_FF)),
                  _resident((D_FF, D_MODEL)), _resident((1, D_MODEL))],
        out_specs=row,
        out_shape=jax.ShapeDtypeStruct((b, seq, D_MODEL), F32),
        scratch_shapes=[pltpu.VMEM((tm, D_FF), BF16)],
        compiler_params=_cparams(2),
        name="ffn_final",
    )(h3, g, wg, wu, wd, fg)


def _rope(y, c, s, half):
    w = y.shape[1]
    reps = w // LANES
    if reps > 1:
        c = jnp.concatenate([c] * reps, axis=1)
        s = jnp.concatenate([s] * reps, axis=1)
    lane = lax.broadcasted_iota(jnp.int32, y.shape, 1)
    first = (lane & 63) < half
    partner = jnp.where(first, pltpu.roll(y, w - half, 1), pltpu.roll(y, half, 1))
    return y * c + partner * s


def _rope_tables(pos, rot_dim):
    half = rot_dim // 2
    inv = ROPE_THETA ** (-jnp.arange(0, rot_dim, 2, dtype=F32) / rot_dim)
    ang = pos.astype(F32)[:, None] * inv[None, :]
    cos, sin = jnp.cos(ang), jnp.sin(ang)
    n = pos.shape[0]
    pad = 64 - rot_dim
    c64 = jnp.concatenate([cos, cos, jnp.ones((n, pad), F32)], axis=1)
    s64 = jnp.concatenate([-sin, sin, jnp.zeros((n, pad), F32)], axis=1)
    return jnp.concatenate([c64, c64], axis=1), jnp.concatenate([s64, s64], axis=1)


AB_IN_PAD = 3 * 512 + MLA_Q_RANK + MLA_KV_RANK + LANES


def _ab_proj_kernel(h_ref, g_ref, win_ref, qn_ref, wuq_ref, kvn_ref, wukv_ref, ca_ref, sa_ref, cm_ref, sm_ref,
                    qa_ref, ka_ref, va_ref, qb_ref, kb_ref, vb_ref):
    xn = _rms(h_ref[0], g_ref[...], EPS).astype(BF16)
    ca, sa, cm, sm = ca_ref[...], sa_ref[...], cm_ref[...], sm_ref[...]
    half_a = PART_ROT // 2
    half_m = MLA_ROPE // 2
    qa = _rope(_dot(xn, win_ref[:, 0:512]), ca, sa, half_a)
    qa_ref[0] = (qa * (DIFF_HEAD_DIM ** -0.5 * LOG2E)).astype(BF16)
    ka_ref[0] = _rope(_dot(xn, win_ref[:, 512:1024]), ca, sa, half_a).astype(BF16)
    va_ref[0] = _dot(xn, win_ref[:, 1024:1536]).astype(BF16)
    cq = _dot(xn, win_ref[:, 1536:1792])
    ckv = _dot(xn, win_ref[:, 1792:2048])
    kr = _rope(_dot(xn, win_ref[:, 2048:2176]), cm, sm, half_m).astype(BF16)
    qb = _dot(_rms(cq, qn_ref[...], EPS).astype(BF16), wuq_ref[...])
    scale = (MLA_NOPE + MLA_ROPE) ** -0.5 * LOG2E
    for hh in range(MLA_HEADS):
        o = 2 * LANES * hh
        qb_ref[0, :, o:o + LANES] = (qb[:, o:o + LANES] * scale).astype(BF16)
        qb_ref[0, :, o + LANES:o + 2 * LANES] = (_rope(qb[:, o + LANES:o + 2 * LANES], cm, sm, half_m) * scale).astype(BF16)
    kv = _dot(_rms(ckv, kvn_ref[...], EPS).astype(BF16), wukv_ref[...])
    for hh in range(MLA_HEADS):
        o = 2 * LANES * hh
        kb_ref[0, :, o:o + LANES] = kv[:, LANES * hh:LANES * (hh + 1)].astype(BF16)
        kb_ref[0, :, o + LANES:o + 2 * LANES] = kr
    vb_ref[0] = kv[:, MLA_HEADS * MLA_NOPE:].astype(BF16)


def _ab_proj(h3, g, win, qn, wuq, kvn, wukv, tabs):
    b, l, _ = h3.shape
    tm = PROJ_TM
    assert l % tm == 0
    row = lambda w: pl.BlockSpec((1, tm, w), lambda j, bb: (bb, j, 0))
    tab = pl.BlockSpec((tm, LANES), lambda j, bb: (j, 0))
    outw = (512, 512, 512, 1024, 1024, 512)
    return pl.pallas_call(
        _ab_proj_kernel,
        grid=(l // tm, b),
        in_specs=[row(D_MODEL), _resident((1, D_MODEL)), _resident((D_MODEL, AB_IN_PAD)),
                  _resident((1, MLA_Q_RANK)), _resident((MLA_Q_RANK, 1024)),
                  _resident((1, MLA_KV_RANK)), _resident((MLA_KV_RANK, 1024)), tab, tab, tab, tab],
        out_specs=[row(w) for w in outw],
        out_shape=[jax.ShapeDtypeStruct((b, l, w), BF16) for w in outw],
        compiler_params=_cparams(2),
        name="ab_proj",
    )(h3, g, win, qn, wuq, kvn, wukv, *tabs)


def _ab_out_kernel(h_ref, oa_ref, ob_ref, wa_ref, wb_ref, o_ref):
    o_ref[...] = h_ref[...] + _dot(oa_ref[...], wa_ref[...]) + _dot(ob_ref[...], wb_ref[...])


def _ab_out(h, oa, ob, wa, wb):
    n = h.shape[0]
    tm = FFN_TM
    row = lambda w: pl.BlockSpec((tm, w), lambda i: (i, 0))
    return pl.pallas_call(
        _ab_out_kernel,
        grid=(n // tm,),
        in_specs=[row(D_MODEL), row(512), row(512), _resident((512, D_MODEL)), _resident((512, D_MODEL))],
        out_specs=row(D_MODEL),
        out_shape=jax.ShapeDtypeStruct((n, D_MODEL), F32),
        compiler_params=_cparams(1),
        name="ab_out",
    )(h, oa, ob, wa, wb)


def _transpose(x):
    return x.astype(F32).T.astype(x.dtype)


def _flash_step(k, vt, qt, m_ref, l_ref, acc_ref, mask, first):
    s = _dot(k, qt)
    if mask is not None:
        s = jnp.where(mask, s, NEG_INF)
    smax = jnp.max(s, axis=0, keepdims=True)
    if first:
        m_new = smax
        p = jnp.exp2(s - m_new)
        l_ref[...] = jnp.sum(p, axis=0, keepdims=True)
        acc_ref[...] = _dot(vt, p.astype(BF16))
    else:
        m_prev = m_ref[...]
        m_new = jnp.maximum(m_prev, smax)
        alpha = jnp.exp2(m_prev - m_new)
        p = jnp.exp2(s - m_new)
        l_ref[...] = alpha * l_ref[...] + jnp.sum(p, axis=0, keepdims=True)
        acc_ref[...] = alpha * acc_ref[...] + _dot(vt, p.astype(BF16))
    m_ref[...] = m_new


def _causal_attend(qt, k_ref, v_ref, vt_ref, m_ref, l_ref, acc_ref, *, tq, tk, reps, seq, front):
    r_all = qt.shape[1]
    state = (m_ref, l_ref, acc_ref)
    kf = k_ref[0, seq:seq + BLOCK, :]
    key = lax.broadcasted_iota(jnp.int32, (BLOCK, r_all), 0)
    valid = key >= N_FRONT
    if front:
        qry = lax.broadcasted_iota(jnp.int32, (BLOCK, r_all), 1) & (tq - 1)
        _flash_step(kf, _transpose(v_ref[0, seq:seq + BLOCK, :]), qt, *state, valid & (key <= qry), True)
        return
    i = pl.program_id(2)
    nblk = v_ref.shape[1] // BLOCK
    sub = tk // BLOCK

    @pl.when(i == 0)
    def _fill():
        def fill(c, carry):
            st = pl.multiple_of(c * BLOCK, BLOCK)
            vt_ref[c] = _transpose(v_ref[0, pl.ds(st, BLOCK), :])
            return carry
        lax.fori_loop(0, nblk, fill, 0)

    def vt_block(j):
        return jnp.concatenate([vt_ref[j * sub + u] for u in range(sub)], axis=1)

    _flash_step(kf, vt_ref[seq // BLOCK], qt, *state, valid, True)

    def body(j, carry):
        st = pl.multiple_of(j * tk, tk)
        _flash_step(k_ref[0, pl.ds(st, tk), :], vt_block(j), qt, *state, None, False)
        return carry

    per = tq // tk
    lax.fori_loop(0, i * per, body, 0)
    qry = lax.broadcasted_iota(jnp.int32, (tk, r_all), 1) & (tq - 1)
    for u in range(per):
        j = i * per + u
        st = pl.multiple_of(j * tk, tk)
        key = lax.broadcasted_iota(jnp.int32, (tk, r_all), 0) + u * tk
        _flash_step(k_ref[0, pl.ds(st, tk), :], vt_block(j), qt, *state, key <= qry, False)


def _diff_kernel(q_ref, k_ref, v_ref, lq1_ref, lk1_ref, lq2_ref, lk2_ref, sub_ref, *rest,
                 tq, tk, seq, front, lambda_init):
    if front:
        _, o_ref, vt_ref, m_ref, l_ref, acc_ref = rest
    else:
        o_ref, vt_ref, m_ref, l_ref, acc_ref = rest
    qt = _transpose(q_ref[0])
    d = lax.broadcasted_iota(jnp.int32, qt.shape, 0)
    zero = jnp.zeros_like(qt)
    qs = jnp.concatenate([jnp.where(d < DIFF_HEAD_DIM, qt, zero), jnp.where(d >= DIFF_HEAD_DIM, qt, zero)], axis=1)
    _causal_attend(qs, k_ref, v_ref, vt_ref, m_ref, l_ref, acc_ref, tq=tq, tk=tk, reps=2, seq=seq, front=front)
    o = acc_ref[...] / l_ref[...]
    lam = (jnp.exp(jnp.sum(lq1_ref[...] * lk1_ref[...], keepdims=True))
           - jnp.exp(jnp.sum(lq2_ref[...] * lk2_ref[...], keepdims=True)) + lambda_init)
    w = o[:, :tq] - lam * o[:, tq:]
    w = w * lax.rsqrt(jnp.mean(w * w, axis=0, keepdims=True) + DIFF_EPS) * sub_ref[...]
    o_ref[0] = (w * (1.0 - lambda_init)).T.astype(BF16)


def _mla_kernel(q_ref, k_ref, v_ref, *rest, tq, tk, seq, front):
    if front:
        _, o_ref, vt_ref, m_ref, l_ref, acc_ref = rest
    else:
        o_ref, vt_ref, m_ref, l_ref, acc_ref = rest
    _causal_attend(_transpose(q_ref[0]), k_ref, v_ref, vt_ref, m_ref, l_ref, acc_ref, tq=tq, tk=tk, reps=1,
                   seq=seq, front=front)
    o_ref[0] = (acc_ref[...] / l_ref[...]).T.astype(BF16)


def _causal_attention(kernel, q, k, v, extra, *, heads, dk, reps, tq, tk, seq, name):
    b, l, _ = q.shape
    dv = LANES
    outs = None
    for front in (False, True):
        t = BLOCK if front else tq
        if front:
            grid = (b, heads, 1)
            qmap = lambda bb, hh, i: (bb, seq // BLOCK, hh)
        else:
            grid = (b, heads, seq // t)
            qmap = lambda bb, hh, i: (bb, i, hh)
        kvmap = lambda bb, hh, i: (bb, 0, hh)
        in_specs = [pl.BlockSpec((1, t, dk), qmap), pl.BlockSpec((1, l, dk), kvmap), pl.BlockSpec((1, l, dv), kvmap)]
        in_specs += [_resident(e.shape) for e in extra]
        args = [q, k, v, *extra]
        aliases = {}
        if front:
            in_specs.append(pl.BlockSpec(memory_space=pl.ANY))
            aliases = {len(args): 0}
            args.append(outs)
        outs = pl.pallas_call(
            functools.partial(kernel, tq=t, tk=min(t, tk), seq=seq, front=front),
            grid=grid,
            in_specs=in_specs,
            out_specs=pl.BlockSpec((1, t, dv), qmap),
            out_shape=jax.ShapeDtypeStruct((b, l, heads * dv), BF16),
            scratch_shapes=[pltpu.VMEM((1 if front else l // BLOCK, dv, BLOCK), BF16),
                            pltpu.VMEM((1, reps * t), F32), pltpu.VMEM((1, reps * t), F32),
                            pltpu.VMEM((dv, reps * t), F32)],
            input_output_aliases=aliases,
            compiler_params=_cparams(3),
            name=name + ("_front" if front else ""),
        )(*args)
    return outs


SWA_IN_PAD = SWA_HEADS * SWA_HEAD_DIM + 2 * SWA_KV_HEADS * LANES


def _swa_proj_kernel(h_ref, g_ref, w_ref, b_ref, ca_ref, sa_ref, q_ref, k_ref, v_ref):
    xn = _rms(h_ref[0], g_ref[...], EPS).astype(BF16)
    ca, sa = ca_ref[...], sa_ref[...]
    half = PART_ROT // 2
    nq = SWA_HEADS * SWA_HEAD_DIM
    nk = SWA_KV_HEADS * LANES
    q = _rope(_dot(xn, w_ref[:, 0:nq]) + b_ref[:, 0:nq], ca, sa, half)
    q_ref[0] = (q * SWA_HEAD_DIM ** -0.5).astype(BF16)
    k_ref[0] = _rope(_dot(xn, w_ref[:, nq:nq + nk]) + b_ref[:, nq:nq + nk], ca, sa, half).astype(BF16)
    v_ref[0] = (_dot(xn, w_ref[:, nq + nk:]) + b_ref[:, nq + nk:]).astype(BF16)


def _swa_proj(h3, g, w, bias, tabs):
    b, l, _ = h3.shape
    tm = PROJ_TM
    row = lambda wd: pl.BlockSpec((1, tm, wd), lambda j, bb: (bb, j, 0))
    tab = pl.BlockSpec((tm, LANES), lambda j, bb: (j, 0))
    outw = (SWA_HEADS * SWA_HEAD_DIM, SWA_KV_HEADS * LANES, SWA_KV_HEADS * LANES)
    return pl.pallas_call(
        _swa_proj_kernel,
        grid=(l // tm, b),
        in_specs=[row(D_MODEL), _resident((1, D_MODEL)), _resident((D_MODEL, SWA_IN_PAD)),
                  _resident((1, SWA_IN_PAD)), tab, tab],
        out_specs=[row(w_) for w_ in outw],
        out_shape=[jax.ShapeDtypeStruct((b, l, w_), BF16) for w_ in outw],
        compiler_params=_cparams(2),
        name="swa_proj",
    )(h3, g, w, bias, *tabs)


def _swa_kernel(sinks_ref, q_ref, k_ref, v_ref, *rest, tq, seq, front):
    if front:
        _, o_ref, p_ref = rest
    else:
        o_ref, p_ref = rest
    if front:
        nk = BLOCK
        r = lax.broadcasted_iota(jnp.int32, (tq, nk), 0)
        c = lax.broadcasted_iota(jnp.int32, (tq, nk), 1)
        mask = (c >= N_FRONT) & (c <= r)
    else:
        i = pl.program_id(1)
        bw = tq + WINDOW
        nk = BLOCK + bw
        bs = pl.multiple_of(jnp.maximum(i * tq - WINDOW, 0), BLOCK)
        r = lax.broadcasted_iota(jnp.int32, (tq, nk), 0)
        c = lax.broadcasted_iota(jnp.int32, (tq, nk), 1)
        dist = (i * tq + r) - (bs + c - BLOCK)
        mask = ((c < BLOCK) & (c >= N_FRONT)) | ((c >= BLOCK) & (dist >= 0) & (dist < WINDOW))
    lane = lax.broadcasted_iota(jnp.int32, (tq, LANES), 1)
    low = lane < SWA_HEAD_DIM
    high = lane >= SWA_HEAD_DIM
    for g in range(SWA_KV_HEADS):
        gs = slice(g * LANES, (g + 1) * LANES)
        kf = k_ref[0, seq:seq + BLOCK, gs]
        vf = v_ref[0, seq:seq + BLOCK, gs]
        if front:
            kc, vc = kf, vf
        else:
            kc = jnp.concatenate([kf, k_ref[0, pl.ds(bs, bw), gs]], axis=0)
            vc = jnp.concatenate([vf, v_ref[0, pl.ds(bs, bw), gs]], axis=0)
        qz = []
        for rr in range(SWA_GROUP):
            hd = g * SWA_GROUP + rr
            slab = q_ref[0, :, (hd // 2) * LANES:(hd // 2 + 1) * LANES]
            qz.append(jnp.where(low if hd % 2 == 0 else high, slab, jnp.zeros_like(slab)))
        s = _dot_nt(jnp.concatenate(qz, axis=0), kc)
        for rr in range(SWA_GROUP):
            sink = sinks_ref[g * SWA_GROUP + rr]
            sr = jnp.where(mask, s[rr * tq:(rr + 1) * tq], NEG_INF)
            m = jnp.maximum(jnp.max(sr, axis=-1, keepdims=True), sink)
            e = jnp.exp(sr - m)
            den = jnp.sum(e, axis=-1, keepdims=True) + jnp.exp(sink - m)
            p_ref[rr * tq:(rr + 1) * tq, :] = (e / den).astype(BF16)
        o = _dot(p_ref[...], vc)
        for pp in range(SWA_GROUP // 2):
            ev = o[(2 * pp) * tq:(2 * pp + 1) * tq]
            od = o[(2 * pp + 1) * tq:(2 * pp + 2) * tq]
            col = (g * SWA_GROUP // 2 + pp) * LANES
            o_ref[0, :, col:col + LANES] = jnp.where(low, ev, od).astype(BF16)


def _swa_attention(q, k, v, sinks, *, seq):
    b, l, _ = q.shape
    nq = SWA_HEADS * SWA_HEAD_DIM
    outs = None
    for front in (False, True):
        tq = BLOCK if front else SWA_TQ
        nk = BLOCK if front else 2 * BLOCK + tq
        if front:
            grid = (b, 1)
            qmap = lambda bb, i: (bb, seq // BLOCK, 0)
        else:
            grid = (b, seq // tq)
            qmap = lambda bb, i: (bb, i, 0)
        kvmap = lambda bb, i: (bb, 0, 0)
        in_specs = [pl.BlockSpec(memory_space=pltpu.SMEM), pl.BlockSpec((1, tq, nq), qmap),
                    pl.BlockSpec((1, l, k.shape[2]), kvmap), pl.BlockSpec((1, l, v.shape[2]), kvmap)]
        args = [sinks, q, k, v]
        aliases = {}
        if front:
            in_specs.append(pl.BlockSpec(memory_space=pl.ANY))
            aliases = {len(args): 0}
            args.append(outs)
        outs = pl.pallas_call(
            functools.partial(_swa_kernel, tq=tq, seq=seq, front=front),
            grid=grid,
            in_specs=in_specs,
            out_specs=pl.BlockSpec((1, tq, nq), qmap),
            out_shape=jax.ShapeDtypeStruct((b, l, nq), BF16),
            scratch_shapes=[pltpu.VMEM((SWA_GROUP * tq, nk), BF16)],
            input_output_aliases=aliases,
            compiler_params=_cparams(2),
            name="swa_attn" + ("_front" if front else ""),
        )(*args)
    return outs


def _swa_out_kernel(h_ref, o_ref_in, w_ref, b_ref, o_ref):
    o_ref[...] = h_ref[...] + _dot(o_ref_in[...], w_ref[...]) + b_ref[...]


def _swa_out(h, o, w, bias):
    n = h.shape[0]
    tm = FFN_TM
    row = lambda wd: pl.BlockSpec((tm, wd), lambda i: (i, 0))
    return pl.pallas_call(
        _swa_out_kernel,
        grid=(n // tm,),
        in_specs=[row(D_MODEL), row(D_MODEL), _resident((D_MODEL, D_MODEL)), _resident((1, D_MODEL))],
        out_specs=row(D_MODEL),
        out_shape=jax.ShapeDtypeStruct((n, D_MODEL), F32),
        compiler_params=_cparams(1),
        name="swa_out",
    )(h, o, w, bias)


def _ab_weights(w_in, w_uq, w_ukv):
    win = jnp.pad(w_in, ((0, 0), (0, AB_IN_PAD - w_in.shape[1]))).astype(BF16)
    wq = w_uq.reshape(MLA_Q_RANK, MLA_HEADS, MLA_NOPE + MLA_ROPE)
    wq = jnp.pad(wq, ((0, 0), (0, 0), (0, 2 * LANES - MLA_NOPE - MLA_ROPE))).reshape(MLA_Q_RANK, MLA_HEADS * 2 * LANES)
    wkv = w_ukv.reshape(MLA_KV_RANK, MLA_HEADS, 2, MLA_NOPE).transpose(0, 2, 1, 3).reshape(MLA_KV_RANK, -1)
    return win, wq.astype(BF16), wkv.astype(BF16)


def _swa_weights(w_qkv, b_qkv):
    nq = SWA_HEADS * SWA_HEAD_DIM
    nkv = SWA_KV_HEADS * SWA_HEAD_DIM

    def dup(a):
        a = a.reshape(a.shape[:-1] + (SWA_KV_HEADS, 1, SWA_HEAD_DIM))
        return jnp.broadcast_to(a, a.shape[:-2] + (2, SWA_HEAD_DIM)).reshape(a.shape[:-3] + (SWA_KV_HEADS * LANES,))

    parts = lambda a: jnp.concatenate([a[..., :nq], dup(a[..., nq:nq + nkv]), dup(a[..., nq + nkv:])], axis=-1)
    return parts(w_qkv).astype(BF16), parts(b_qkv)[None, :]


def kernel(x, meta_tokens, ffn1_norm, ffn1_w_gate, ffn1_w_up, ffn1_w_down, mix_norm, ab_w_in, diff_lambda_q1, diff_lambda_k1, diff_lambda_q2, diff_lambda_k2, diff_subln, mla_q_norm, mla_w_uq, mla_kv_norm, mla_w_ukv, ab_w_out, swa_w_qkv, swa_b_qkv, swa_sinks, swa_w_out, swa_b_out, ffn2_norm, ffn2_w_gate, ffn2_w_up, ffn2_w_down, final_norm):
    b, seq, d = x.shape
    depth = ffn1_norm.shape[0]
    l = seq + BLOCK
    n = b * l
    meta = jnp.broadcast_to(meta_tokens[None].astype(x.dtype), (b, N_META, d))
    h = jnp.concatenate([x, jnp.zeros((b, N_FRONT, d), x.dtype), meta], axis=1).reshape(n, d)

    pos = jnp.concatenate([jnp.arange(seq) + N_META, jnp.maximum(jnp.arange(BLOCK) - N_FRONT, 0)])
    tabs_p = _rope_tables(pos, PART_ROT)
    tabs_m = _rope_tables(pos, MLA_ROPE)
    row2 = lambda a: a.reshape(1, -1)

    out = None
    for ly in range(depth):
        h = _ffn(h, row2(ffn1_norm[ly]), ffn1_w_gate[ly].astype(BF16), ffn1_w_up[ly].astype(BF16),
                 ffn1_w_down[ly].astype(BF16))
        h3 = h.reshape(b, l, d)
        if ly % 2 == 0:
            e = ly // 2
            lambda_init = 0.8 - 0.6 * math.exp(-0.3 * ly)
            win, wuq, wukv = _ab_weights(ab_w_in[e], mla_w_uq[e], mla_w_ukv[e])
            qa, ka, va, qb, kb, vb = _ab_proj(h3, row2(mix_norm[ly]), win, row2(mla_q_norm[e]), wuq,
                                              row2(mla_kv_norm[e]), wukv, tabs_p + tabs_m)
            extra = [row2(diff_lambda_q1[e]), row2(diff_lambda_k1[e]), row2(diff_lambda_q2[e]),
                     row2(diff_lambda_k2[e]), diff_subln[e].reshape(-1, 1)]
            oa = _causal_attention(functools.partial(_diff_kernel, lambda_init=lambda_init), qa, ka, va, extra,
                                   heads=DIFF_HEADS, dk=LANES, reps=2, tq=ATT_T, tk=ATT_TK, seq=seq,
                                   name="diff_attn")
            ob = _causal_attention(_mla_kernel, qb, kb, vb, [], heads=MLA_HEADS, dk=2 * LANES, reps=1,
                                   tq=MLA_TQ, tk=ATT_TK, seq=seq, name="mla_attn")
            wo = ab_w_out[e].astype(BF16)
            h = _ab_out(h, oa.reshape(n, -1), ob.reshape(n, -1), wo[:512], wo[512:])
        else:
            o = ly // 2
            w, bias = _swa_weights(swa_w_qkv[o], swa_b_qkv[o])
            q, k, v = _swa_proj(h3, row2(mix_norm[ly]), w, bias, tabs_p)
            att = _swa_attention(q, k, v, swa_sinks[o], seq=seq)
            h = _swa_out(h, att.reshape(n, -1), swa_w_out[o].astype(BF16), row2(swa_b_out[o]))
        wg, wu, wd = ffn2_w_gate[ly].astype(BF16), ffn2_w_up[ly].astype(BF16), ffn2_w_down[ly].astype(BF16)
        if ly == depth - 1:
            out = _ffn_final(h.reshape(b, l, d), row2(ffn2_norm[ly]), wg, wu, wd, row2(final_norm), seq)
        else:
            h = _ffn(h, row2(ffn2_norm[ly]), wg, wu, wd)
    return out
```

```python
import functools
import math

import jax
import jax.numpy as jnp
from jax import lax
from jax.experimental import pallas as pl
from jax.experimental.pallas import tpu as pltpu

F32 = jnp.float32
BF16 = jnp.bfloat16

D_MODEL = 1024
N_META = 16
BLOCK = 128
N_FRONT = BLOCK - N_META
ROPE_THETA = 500000.0
EPS = 1e-6
NEG_INF = -1e30
D_FF = 2816

DIFF_HEADS = 4
DIFF_HEAD_DIM = 64
DIFF_V_DIM = 2 * DIFF_HEAD_DIM
DIFF_EPS = 1e-5
MLA_HEADS = 4
MLA_NOPE = 128
MLA_ROPE = 64
MLA_V = 128
MLA_Q_RANK = 256
MLA_KV_RANK = 256
SWA_HEADS = 16
SWA_KV_HEADS = 2
SWA_GROUP = SWA_HEADS // SWA_KV_HEADS
SWA_HEAD_DIM = 64
WINDOW = 128

PART_ROT = DIFF_HEAD_DIM // 4
LANES = 128
VMEM_LIMIT = 56 * 1024 * 1024

FFN_TM = 640
FFN_FC = 256
PROJ_TM = 640
FFN_FINAL_TM = 512
ATT_TK = 512
ATT_TQ = 2 * ATT_TK
ATT_CW = 256
LOG2E = 1.4426950408889634
SWA_TQ = 256


def _cparams(n_axes):
    return pltpu.CompilerParams(dimension_semantics=("arbitrary",) * n_axes,
                                vmem_limit_bytes=VMEM_LIMIT)


def _resident(shape):
    nd = len(shape)
    return pl.BlockSpec(shape, lambda *_: (0,) * nd, pipeline_mode=pl.Buffered(1))


def _rms(x, g, eps):
    return x * lax.rsqrt(jnp.mean(x * x, axis=-1, keepdims=True) + eps) * g


def _dot(a, b):
    return jnp.dot(a, b, preferred_element_type=F32)


def _dot_nt(a, b):
    return lax.dot_general(a, b, (((1,), (1,)), ((), ())), preferred_element_type=F32)


def _ffn_kernel(x_ref, g_ref, wg_ref, wu_ref, wd_ref, *rest, final):
    if final:
        fg_ref, o_ref, act_ref = rest
    else:
        o_ref, act_ref = rest
    x = x_ref[...].reshape(x_ref.shape[-2:])
    xn = _rms(x, g_ref[...], EPS).astype(BF16)
    for c in range(D_FF // FFN_FC):
        sl = slice(c * FFN_FC, (c + 1) * FFN_FC)
        g = _dot(xn, wg_ref[:, sl])
        u = _dot(xn, wu_ref[:, sl])
        act_ref[:, sl] = (g * (1.0 / (1.0 + jnp.exp(-g))) * u).astype(BF16)
    y = x + 0.5 * _dot(act_ref[...], wd_ref[...])
    if final:
        y = _rms(y, fg_ref[...], EPS)
    o_ref[...] = y.reshape(o_ref.shape)


def _ffn(h, g, wg, wu, wd):
    n = h.shape[0]
    tm = FFN_TM
    assert n % tm == 0
    row = pl.BlockSpec((tm, D_MODEL), lambda i: (i, 0))
    return pl.pallas_call(
        functools.partial(_ffn_kernel, final=False),
        grid=(n // tm,),
        in_specs=[row, _resident((1, D_MODEL)), _resident((D_MODEL, D_FF)), _resident((D_MODEL, D_FF)),
                  _resident((D_FF, D_MODEL))],
        out_specs=row,
        out_shape=jax.ShapeDtypeStruct((n, D_MODEL), F32),
        scratch_shapes=[pltpu.VMEM((tm, D_FF), BF16)],
        compiler_params=_cparams(1),
        name="ffn",
    )(h, g, wg, wu, wd)


def _ffn_final(h3, g, wg, wu, wd, fg, seq):
    b = h3.shape[0]
    tm = FFN_FINAL_TM
    assert seq % tm == 0
    row = pl.BlockSpec((1, tm, D_MODEL), lambda bb, i: (bb, i, 0))
    return pl.pallas_call(
        functools.partial(_ffn_kernel, final=True),
        grid=(b, seq // tm),
        in_specs=[row, _resident((1, D_MODEL)), _resident((D_MODEL, D_FF)), _resident((D_MODEL, D_FF)),
                  _resident((D_FF, D_MODEL)), _resident((1, D_MODEL))],
        out_specs=row,
        out_shape=jax.ShapeDtypeStruct((b, seq, D_MODEL), F32),
        scratch_shapes=[pltpu.VMEM((tm, D_FF), BF16)],
        compiler_params=_cparams(2),
        name="ffn_final",
    )(h3, g, wg, wu, wd, fg)


def _rope(y, c, s, half):
    w = y.shape[1]
    reps = w // LANES
    if reps > 1:
        c = jnp.concatenate([c] * reps, axis=1)
        s = jnp.concatenate([s] * reps, axis=1)
    lane = lax.broadcasted_iota(jnp.int32, y.shape, 1)
    first = (lane & 63) < half
    partner = jnp.where(first, pltpu.roll(y, w - half, 1), pltpu.roll(y, half, 1))
    return y * c + partner * s


def _rope_tables(pos, rot_dim):
    half = rot_dim // 2
    inv = ROPE_THETA ** (-jnp.arange(0, rot_dim, 2, dtype=F32) / rot_dim)
    ang = pos.astype(F32)[:, None] * inv[None, :]
    cos, sin = jnp.cos(ang), jnp.sin(ang)
    n = pos.shape[0]
    pad = 64 - rot_dim
    c64 = jnp.concatenate([cos, cos, jnp.ones((n, pad), F32)], axis=1)
    s64 = jnp.concatenate([-sin, sin, jnp.zeros((n, pad), F32)], axis=1)
    return jnp.concatenate([c64, c64], axis=1), jnp.concatenate([s64, s64], axis=1)


AB_IN_PAD = 3 * 512 + MLA_Q_RANK + MLA_KV_RANK + LANES


def _ab_proj_kernel(h_ref, g_ref, win_ref, qn_ref, wuq_ref, kvn_ref, wukv_ref, ca_ref, sa_ref, cm_ref, sm_ref,
                    qa_ref, ka_ref, va_ref, qb_ref, kb_ref, vb_ref):
    xn = _rms(h_ref[0], g_ref[...], EPS).astype(BF16)
    ca, sa, cm, sm = ca_ref[...], sa_ref[...], cm_ref[...], sm_ref[...]
    half_a = PART_ROT // 2
    half_m = MLA_ROPE // 2
    qa = _rope(_dot(xn, win_ref[:, 0:512]), ca, sa, half_a)
    qa_ref[0] = (qa * (DIFF_HEAD_DIM ** -0.5 * LOG2E)).astype(BF16)
    ka_ref[0] = _rope(_dot(xn, win_ref[:, 512:1024]), ca, sa, half_a).astype(BF16)
    va_ref[0] = _dot(xn, win_ref[:, 1024:1536]).astype(BF16)
    cq = _dot(xn, win_ref[:, 1536:1792])
    ckv = _dot(xn, win_ref[:, 1792:2048])
    kr = _rope(_dot(xn, win_ref[:, 2048:2176]), cm, sm, half_m).astype(BF16)
    qb = _dot(_rms(cq, qn_ref[...], EPS).astype(BF16), wuq_ref[...])
    scale = (MLA_NOPE + MLA_ROPE) ** -0.5 * LOG2E
    for hh in range(MLA_HEADS):
        o = 2 * LANES * hh
        qb_ref[0, :, o:o + LANES] = (qb[:, o:o + LANES] * scale).astype(BF16)
        qb_ref[0, :, o + LANES:o + 2 * LANES] = (_rope(qb[:, o + LANES:o + 2 * LANES], cm, sm, half_m) * scale).astype(BF16)
    kv = _dot(_rms(ckv, kvn_ref[...], EPS).astype(BF16), wukv_ref[...])
    for hh in range(MLA_HEADS):
        o = 2 * LANES * hh
        kb_ref[0, :, o:o + LANES] = kv[:, LANES * hh:LANES * (hh + 1)].astype(BF16)
        kb_ref[0, :, o + LANES:o + 2 * LANES] = kr
    vb_ref[0] = kv[:, MLA_HEADS * MLA_NOPE:].astype(BF16)


def _ab_proj(h3, g, win, qn, wuq, kvn, wukv, tabs):
    b, l, _ = h3.shape
    tm = PROJ_TM
    assert l % tm == 0
    row = lambda w: pl.BlockSpec((1, tm, w), lambda j, bb: (bb, j, 0))
    tab = pl.BlockSpec((tm, LANES), lambda j, bb: (j, 0))
    outw = (512, 512, 512, 1024, 1024, 512)
    return pl.pallas_call(
        _ab_proj_kernel,
        grid=(l // tm, b),
        in_specs=[row(D_MODEL), _resident((1, D_MODEL)), _resident((D_MODEL, AB_IN_PAD)),
                  _resident((1, MLA_Q_RANK)), _resident((MLA_Q_RANK, 1024)),
                  _resident((1, MLA_KV_RANK)), _resident((MLA_KV_RANK, 1024)), tab, tab, tab, tab],
        out_specs=[row(w) for w in outw],
        out_shape=[jax.ShapeDtypeStruct((b, l, w), BF16) for w in outw],
        compiler_params=_cparams(2),
        name="ab_proj",
    )(h3, g, win, qn, wuq, kvn, wukv, *tabs)


def _ab_out_kernel(h_ref, oa_ref, ob_ref, wa_ref, wb_ref, o_ref):
    o_ref[...] = h_ref[...] + _dot(oa_ref[...], wa_ref[...]) + _dot(ob_ref[...], wb_ref[...])


def _ab_out(h, oa, ob, wa, wb):
    n = h.shape[0]
    tm = FFN_TM
    row = lambda w: pl.BlockSpec((tm, w), lambda i: (i, 0))
    return pl.pallas_call(
        _ab_out_kernel,
        grid=(n // tm,),
        in_specs=[row(D_MODEL), row(512), row(512), _resident((512, D_MODEL)), _resident((512, D_MODEL))],
        out_specs=row(D_MODEL),
        out_shape=jax.ShapeDtypeStruct((n, D_MODEL), F32),
        compiler_params=_cparams(1),
        name="ab_out",
    )(h, oa, ob, wa, wb)


def _transpose(x):
    return x.astype(F32).T.astype(x.dtype)


def _flash_step(k, vt, qt, m_ref, l_ref, acc_ref, mask):
    cw = min(ATT_CW, qt.shape[1])
    for c0 in range(0, qt.shape[1], cw):
        cs = slice(c0, c0 + cw)
        s = _dot(k, qt[:, cs])
        s = jnp.where(mask(c0, s.shape), s, NEG_INF)
        m_new = jnp.max(s, axis=0, keepdims=True)
        p = jnp.exp2(s - m_new)
        l_ref[:, cs] = jnp.sum(p, axis=0, keepdims=True)
        acc_ref[:, cs] = _dot(vt, p.astype(BF16))
        m_ref[:, cs] = m_new


def _produce(k, qt, s_ref, x_ref, chunks):
    for c0 in chunks:
        cs = slice(c0, c0 + ATT_CW)
        s = _dot(k, qt[:, cs])
        s_ref[:, cs] = s
        x_ref[:, cs] = jnp.max(s, axis=0, keepdims=True)


def _consume(s_ref, x_ref, vt, m_ref, l_ref, acc_ref, mask, chunks):
    for c0 in chunks:
        cs = slice(c0, c0 + ATT_CW)
        s = s_ref[:, cs]
        if mask is None:
            smax = x_ref[:, cs]
        else:
            s = jnp.where(mask(c0, s.shape), s, NEG_INF)
            smax = jnp.max(s, axis=0, keepdims=True)
        m_prev = m_ref[:, cs]
        m_new = jnp.maximum(m_prev, smax)
        alpha = jnp.exp2(m_prev - m_new)
        p = jnp.exp2(s - m_new)
        l_ref[:, cs] = alpha * l_ref[:, cs] + jnp.sum(p, axis=0, keepdims=True)
        acc_ref[:, cs] = alpha * acc_ref[:, cs] + _dot(vt, p.astype(BF16))
        m_ref[:, cs] = m_new


def _causal_attend(qt, k_ref, v_ref, scr, *, tq, tk, seq, front):
    vt_ref, m_ref, l_ref, acc_ref, s0, s1, x0, x1 = scr
    state = (m_ref, l_ref, acc_ref)
    kf = k_ref[0, seq:seq + BLOCK, :]

    def valid(c0, shape):
        return lax.broadcasted_iota(jnp.int32, shape, 0) >= N_FRONT

    def causal(key0):
        def mask(c0, shape):
            key = lax.broadcasted_iota(jnp.int32, shape, 0) + key0
            qry = (lax.broadcasted_iota(jnp.int32, shape, 1) + c0) & (tq - 1)
            return key <= qry
        return mask

    if front:
        both = lambda c0, shape: valid(c0, shape) & causal(0)(c0, shape)
        _flash_step(kf, _transpose(v_ref[0, seq:seq + BLOCK, :]), qt, *state, both)
        return
    assert tq == 2 * tk
    i = pl.program_id(2)
    nblk = v_ref.shape[1] // BLOCK
    sub = tk // BLOCK
    chunks = list(range(0, qt.shape[1], ATT_CW))
    late = [c0 for c0 in chunks if (c0 & (tq - 1)) + ATT_CW > tk]

    @pl.when(i == 0)
    def _fill():
        def fill(c, carry):
            st = pl.multiple_of(c * BLOCK, BLOCK)
            vt_ref[c] = _transpose(v_ref[0, pl.ds(st, BLOCK), :])
            return carry
        lax.fori_loop(0, nblk, fill, 0)

    def k_block(j):
        return k_ref[0, pl.ds(pl.multiple_of(j * tk, tk), tk), :]

    def vt_block(j):
        return jnp.concatenate([vt_ref[j * sub + u] for u in range(sub)], axis=1)

    _produce(k_block(0), qt, s0, x0, chunks)
    _flash_step(kf, vt_ref[seq // BLOCK], qt, *state, valid)

    def body(p, carry):
        _produce(k_block(2 * p + 1), qt, s1, x1, chunks)
        _consume(s0, x0, vt_block(2 * p), *state, None, chunks)
        _produce(k_block(2 * p + 2), qt, s0, x0, chunks)
        _consume(s1, x1, vt_block(2 * p + 1), *state, None, chunks)
        return carry

    lax.fori_loop(0, i, body, 0)
    _produce(k_block(2 * i + 1), qt, s1, x1, late)
    _consume(s0, x0, vt_block(2 * i), *state, causal(0), chunks)
    _consume(s1, x1, vt_block(2 * i + 1), *state, causal(tk), late)


def _diff_kernel(q_ref, k_ref, v_ref, lq1_ref, lk1_ref, lq2_ref, lk2_ref, sub_ref, *rest,
                 tq, tk, seq, front, lambda_init):
    if front:
        _, o_ref, *scr = rest
    else:
        o_ref, *scr = rest
    m_ref, l_ref, acc_ref = scr[1:4]
    qt = _transpose(q_ref[0])
    d = lax.broadcasted_iota(jnp.int32, qt.shape, 0)
    zero = jnp.zeros_like(qt)
    qs = jnp.concatenate([jnp.where(d < DIFF_HEAD_DIM, qt, zero), jnp.where(d >= DIFF_HEAD_DIM, qt, zero)], axis=1)
    _causal_attend(qs, k_ref, v_ref, scr, tq=tq, tk=tk, seq=seq, front=front)
    o = acc_ref[...] / l_ref[...]
    lam = (jnp.exp(jnp.sum(lq1_ref[...] * lk1_ref[...], keepdims=True))
           - jnp.exp(jnp.sum(lq2_ref[...] * lk2_ref[...], keepdims=True)) + lambda_init)
    w = o[:, :tq] - lam * o[:, tq:]
    w = w * lax.rsqrt(jnp.mean(w * w, axis=0, keepdims=True) + DIFF_EPS) * sub_ref[...]
    o_ref[0] = (w * (1.0 - lambda_init)).T.astype(BF16)


def _mla_kernel(q_ref, k_ref, v_ref, *rest, tq, tk, seq, front):
    if front:
        _, o_ref, *scr = rest
    else:
        o_ref, *scr = rest
    m_ref, l_ref, acc_ref = scr[1:4]
    _causal_attend(_transpose(q_ref[0]), k_ref, v_ref, scr, tq=tq, tk=tk, seq=seq, front=front)
    o_ref[0] = (acc_ref[...] / l_ref[...]).T.astype(BF16)


def _causal_attention(kernel, q, k, v, extra, *, heads, dk, reps, tq, tk, seq, name):
    b, l, _ = q.shape
    dv = LANES
    outs = None
    for front in (False, True):
        t = BLOCK if front else tq
        if front:
            grid = (b, heads, 1)
            qmap = lambda bb, hh, i: (bb, seq // BLOCK, hh)
        else:
            grid = (b, heads, seq // t)
            qmap = lambda bb, hh, i: (bb, i, hh)
        kvmap = lambda bb, hh, i: (bb, 0, hh)
        in_specs = [pl.BlockSpec((1, t, dk), qmap), pl.BlockSpec((1, l, dk), kvmap), pl.BlockSpec((1, l, dv), kvmap)]
        in_specs += [_resident(e.shape) for e in extra]
        args = [q, k, v, *extra]
        aliases = {}
        if front:
            in_specs.append(pl.BlockSpec(memory_space=pl.ANY))
            aliases = {len(args): 0}
            args.append(outs)
        outs = pl.pallas_call(
            functools.partial(kernel, tq=t, tk=min(t, tk), seq=seq, front=front),
            grid=grid,
            in_specs=in_specs,
            out_specs=pl.BlockSpec((1, t, dv), qmap),
            out_shape=jax.ShapeDtypeStruct((b, l, heads * dv), BF16),
            scratch_shapes=[pltpu.VMEM((1 if front else l // BLOCK, dv, BLOCK), BF16),
                            pltpu.VMEM((1, reps * t), F32), pltpu.VMEM((1, reps * t), F32),
                            pltpu.VMEM((dv, reps * t), F32)]
                           + [pltpu.VMEM((8 if front else tk, reps * t), F32)] * 2
                           + [pltpu.VMEM((1, reps * t), F32)] * 2,
            input_output_aliases=aliases,
            compiler_params=_cparams(3),
            name=name + ("_front" if front else ""),
        )(*args)
    return outs


SWA_IN_PAD = SWA_HEADS * SWA_HEAD_DIM + 2 * SWA_KV_HEADS * LANES


def _swa_proj_kernel(h_ref, g_ref, w_ref, b_ref, ca_ref, sa_ref, q_ref, k_ref, v_ref):
    xn = _rms(h_ref[0], g_ref[...], EPS).astype(BF16)
    ca, sa = ca_ref[...], sa_ref[...]
    half = PART_ROT // 2
    nq = SWA_HEADS * SWA_HEAD_DIM
    nk = SWA_KV_HEADS * LANES
    q = _rope(_dot(xn, w_ref[:, 0:nq]) + b_ref[:, 0:nq], ca, sa, half)
    q_ref[0] = (q * SWA_HEAD_DIM ** -0.5).astype(BF16)
    k_ref[0] = _rope(_dot(xn, w_ref[:, nq:nq + nk]) + b_ref[:, nq:nq + nk], ca, sa, half).astype(BF16)
    v_ref[0] = (_dot(xn, w_ref[:, nq + nk:]) + b_ref[:, nq + nk:]).astype(BF16)


def _swa_proj(h3, g, w, bias, tabs):
    b, l, _ = h3.shape
    tm = PROJ_TM
    row = lambda wd: pl.BlockSpec((1, tm, wd), lambda j, bb: (bb, j, 0))
    tab = pl.BlockSpec((tm, LANES), lambda j, bb: (j, 0))
    outw = (SWA_HEADS * SWA_HEAD_DIM, SWA_KV_HEADS * LANES, SWA_KV_HEADS * LANES)
    return pl.pallas_call(
        _swa_proj_kernel,
        grid=(l // tm, b),
        in_specs=[row(D_MODEL), _resident((1, D_MODEL)), _resident((D_MODEL, SWA_IN_PAD)),
                  _resident((1, SWA_IN_PAD)), tab, tab],
        out_specs=[row(w_) for w_ in outw],
        out_shape=[jax.ShapeDtypeStruct((b, l, w_), BF16) for w_ in outw],
        compiler_params=_cparams(2),
        name="swa_proj",
    )(h3, g, w, bias, *tabs)


def _swa_kernel(sinks_ref, q_ref, k_ref, v_ref, *rest, tq, seq, front):
    if front:
        _, o_ref, p_ref = rest
    else:
        o_ref, p_ref = rest
    if front:
        nk = BLOCK
        r = lax.broadcasted_iota(jnp.int32, (tq, nk), 0)
        c = lax.broadcasted_iota(jnp.int32, (tq, nk), 1)
        mask = (c >= N_FRONT) & (c <= r)
    else:
        i = pl.program_id(1)
        bw = tq + WINDOW
        nk = BLOCK + bw
        bs = pl.multiple_of(jnp.maximum(i * tq - WINDOW, 0), BLOCK)
        r = lax.broadcasted_iota(jnp.int32, (tq, nk), 0)
        c = lax.broadcasted_iota(jnp.int32, (tq, nk), 1)
        dist = (i * tq + r) - (bs + c - BLOCK)
        mask = ((c < BLOCK) & (c >= N_FRONT)) | ((c >= BLOCK) & (dist >= 0) & (dist < WINDOW))
    lane = lax.broadcasted_iota(jnp.int32, (tq, LANES), 1)
    low = lane < SWA_HEAD_DIM
    high = lane >= SWA_HEAD_DIM
    for g in range(SWA_KV_HEADS):
        gs = slice(g * LANES, (g + 1) * LANES)
        kf = k_ref[0, seq:seq + BLOCK, gs]
        vf = v_ref[0, seq:seq + BLOCK, gs]
        if front:
            kc, vc = kf, vf
        else:
            kc = jnp.concatenate([kf, k_ref[0, pl.ds(bs, bw), gs]], axis=0)
            vc = jnp.concatenate([vf, v_ref[0, pl.ds(bs, bw), gs]], axis=0)
        qz = []
        for rr in range(SWA_GROUP):
            hd = g * SWA_GROUP + rr
            slab = q_ref[0, :, (hd // 2) * LANES:(hd // 2 + 1) * LANES]
            qz.append(jnp.where(low if hd % 2 == 0 else high, slab, jnp.zeros_like(slab)))
        s = _dot_nt(jnp.concatenate(qz, axis=0), kc)
        for rr in range(SWA_GROUP):
            sink = sinks_ref[g * SWA_GROUP + rr]
            sr = jnp.where(mask, s[rr * tq:(rr + 1) * tq], NEG_INF)
            m = jnp.maximum(jnp.max(sr, axis=-1, keepdims=True), sink)
            e = jnp.exp(sr - m)
            den = jnp.sum(e, axis=-1, keepdims=True) + jnp.exp(sink - m)
            p_ref[rr * tq:(rr + 1) * tq, :] = (e / den).astype(BF16)
        o = _dot(p_ref[...], vc)
        for pp in range(SWA_GROUP // 2):
            ev = o[(2 * pp) * tq:(2 * pp + 1) * tq]
            od = o[(2 * pp + 1) * tq:(2 * pp + 2) * tq]
            col = (g * SWA_GROUP // 2 + pp) * LANES
            o_ref[0, :, col:col + LANES] = jnp.where(low, ev, od).astype(BF16)


def _swa_attention(q, k, v, sinks, *, seq):
    b, l, _ = q.shape
    nq = SWA_HEADS * SWA_HEAD_DIM
    outs = None
    for front in (False, True):
        tq = BLOCK if front else SWA_TQ
        nk = BLOCK if front else 2 * BLOCK + tq
        if front:
            grid = (b, 1)
            qmap = lambda bb, i: (bb, seq // BLOCK, 0)
        else:
            grid = (b, seq // tq)
            qmap = lambda bb, i: (bb, i, 0)
        kvmap = lambda bb, i: (bb, 0, 0)
        in_specs = [pl.BlockSpec(memory_space=pltpu.SMEM), pl.BlockSpec((1, tq, nq), qmap),
                    pl.BlockSpec((1, l, k.shape[2]), kvmap), pl.BlockSpec((1, l, v.shape[2]), kvmap)]
        args = [sinks, q, k, v]
        aliases = {}
        if front:
            in_specs.append(pl.BlockSpec(memory_space=pl.ANY))
            aliases = {len(args): 0}
            args.append(outs)
        outs = pl.pallas_call(
            functools.partial(_swa_kernel, tq=tq, seq=seq, front=front),
            grid=grid,
            in_specs=in_specs,
            out_specs=pl.BlockSpec((1, tq, nq), qmap),
            out_shape=jax.ShapeDtypeStruct((b, l, nq), BF16),
            scratch_shapes=[pltpu.VMEM((SWA_GROUP * tq, nk), BF16)],
            input_output_aliases=aliases,
            compiler_params=_cparams(2),
            name="swa_attn" + ("_front" if front else ""),
        )(*args)
    return outs


def _swa_out_kernel(h_ref, o_ref_in, w_ref, b_ref, o_ref):
    o_ref[...] = h_ref[...] + _dot(o_ref_in[...], w_ref[...]) + b_ref[...]


def _swa_out(h, o, w, bias):
    n = h.shape[0]
    tm = FFN_TM
    row = lambda wd: pl.BlockSpec((tm, wd), lambda i: (i, 0))
    return pl.pallas_call(
        _swa_out_kernel,
        grid=(n // tm,),
        in_specs=[row(D_MODEL), row(D_MODEL), _resident((D_MODEL, D_MODEL)), _resident((1, D_MODEL))],
        out_specs=row(D_MODEL),
        out_shape=jax.ShapeDtypeStruct((n, D_MODEL), F32),
        compiler_params=_cparams(1),
        name="swa_out",
    )(h, o, w, bias)


def _ab_weights(w_in, w_uq, w_ukv):
    win = jnp.pad(w_in, ((0, 0), (0, AB_IN_PAD - w_in.shape[1]))).astype(BF16)
    wq = w_uq.reshape(MLA_Q_RANK, MLA_HEADS, MLA_NOPE + MLA_ROPE)
    wq = jnp.pad(wq, ((0, 0), (0, 0), (0, 2 * LANES - MLA_NOPE - MLA_ROPE))).reshape(MLA_Q_RANK, MLA_HEADS * 2 * LANES)
    wkv = w_ukv.reshape(MLA_KV_RANK, MLA_HEADS, 2, MLA_NOPE).transpose(0, 2, 1, 3).reshape(MLA_KV_RANK, -1)
    return win, wq.astype(BF16), wkv.astype(BF16)


def _swa_weights(w_qkv, b_qkv):
    nq = SWA_HEADS * SWA_HEAD_DIM
    nkv = SWA_KV_HEADS * SWA_HEAD_DIM

    def dup(a):
        a = a.reshape(a.shape[:-1] + (SWA_KV_HEADS, 1, SWA_HEAD_DIM))
        return jnp.broadcast_to(a, a.shape[:-2] + (2, SWA_HEAD_DIM)).reshape(a.shape[:-3] + (SWA_KV_HEADS * LANES,))

    parts = lambda a: jnp.concatenate([a[..., :nq], dup(a[..., nq:nq + nkv]), dup(a[..., nq + nkv:])], axis=-1)
    return parts(w_qkv).astype(BF16), parts(b_qkv)[None, :]


def kernel(x, meta_tokens, ffn1_norm, ffn1_w_gate, ffn1_w_up, ffn1_w_down, mix_norm, ab_w_in, diff_lambda_q1, diff_lambda_k1, diff_lambda_q2, diff_lambda_k2, diff_subln, mla_q_norm, mla_w_uq, mla_kv_norm, mla_w_ukv, ab_w_out, swa_w_qkv, swa_b_qkv, swa_sinks, swa_w_out, swa_b_out, ffn2_norm, ffn2_w_gate, ffn2_w_up, ffn2_w_down, final_norm):
    b, seq, d = x.shape
    depth = ffn1_norm.shape[0]
    l = seq + BLOCK
    n = b * l
    meta = jnp.broadcast_to(meta_tokens[None].astype(x.dtype), (b, N_META, d))
    h = jnp.concatenate([x, jnp.zeros((b, N_FRONT, d), x.dtype), meta], axis=1).reshape(n, d)

    pos = jnp.concatenate([jnp.arange(seq) + N_META, jnp.maximum(jnp.arange(BLOCK) - N_FRONT, 0)])
    tabs_p = _rope_tables(pos, PART_ROT)
    tabs_m = _rope_tables(pos, MLA_ROPE)
    row2 = lambda a: a.reshape(1, -1)

    out = None
    for ly in range(depth):
        h = _ffn(h, row2(ffn1_norm[ly]), ffn1_w_gate[ly].astype(BF16), ffn1_w_up[ly].astype(BF16),
                 ffn1_w_down[ly].astype(BF16))
        h3 = h.reshape(b, l, d)
        if ly % 2 == 0:
            e = ly // 2
            lambda_init = 0.8 - 0.6 * math.exp(-0.3 * ly)
            win, wuq, wukv = _ab_weights(ab_w_in[e], mla_w_uq[e], mla_w_ukv[e])
            qa, ka, va, qb, kb, vb = _ab_proj(h3, row2(mix_norm[ly]), win, row2(mla_q_norm[e]), wuq,
                                              row2(mla_kv_norm[e]), wukv, tabs_p + tabs_m)
            extra = [row2(diff_lambda_q1[e]), row2(diff_lambda_k1[e]), row2(diff_lambda_q2[e]),
                     row2(diff_lambda_k2[e]), diff_subln[e].reshape(-1, 1)]
            oa = _causal_attention(functools.partial(_diff_kernel, lambda_init=lambda_init), qa, ka, va, extra,
                                   heads=DIFF_HEADS, dk=LANES, reps=2, tq=ATT_TQ, tk=ATT_TK, seq=seq,
                                   name="diff_attn")
            ob = _causal_attention(_mla_kernel, qb, kb, vb, [], heads=MLA_HEADS, dk=2 * LANES, reps=1,
                                   tq=ATT_TQ, tk=ATT_TK, seq=seq, name="mla_attn")
            wo = ab_w_out[e].astype(BF16)
            h = _ab_out(h, oa.reshape(n, -1), ob.reshape(n, -1), wo[:512], wo[512:])
        else:
            o = ly // 2
            w, bias = _swa_weights(swa_w_qkv[o], swa_b_qkv[o])
            q, k, v = _swa_proj(h3, row2(mix_norm[ly]), w, bias, tabs_p)
            att = _swa_attention(q, k, v, swa_sinks[o], seq=seq)
            h = _swa_out(h, att.reshape(n, -1), swa_w_out[o].astype(BF16), row2(swa_b_out[o]))
        wg, wu, wd = ffn2_w_gate[ly].astype(BF16), ffn2_w_up[ly].astype(BF16), ffn2_w_down[ly].astype(BF16)
        if ly == depth - 1:
            out = _ffn_final(h.reshape(b, l, d), row2(ffn2_norm[ly]), wg, wu, wd, row2(final_norm), seq)
        else:
            h = _ffn(h, row2(ffn2_norm[ly]), wg, wu, wd)
    return out
```

```python
import functools
import math

import jax
import jax.numpy as jnp
from jax import lax
from jax.experimental import pallas as pl
from jax.experimental.pallas import tpu as pltpu

F32 = jnp.float32
BF16 = jnp.bfloat16

D_MODEL = 1024
N_META = 16
BLOCK = 128
N_FRONT = BLOCK - N_META
ROPE_THETA = 500000.0
EPS = 1e-6
NEG_INF = -1e30
D_FF = 2816

DIFF_HEADS = 4
DIFF_HEAD_DIM = 64
DIFF_V_DIM = 2 * DIFF_HEAD_DIM
DIFF_EPS = 1e-5
MLA_HEADS = 4
MLA_NOPE = 128
MLA_ROPE = 64
MLA_V = 128
MLA_Q_RANK = 256
MLA_KV_RANK = 256
SWA_HEADS = 16
SWA_KV_HEADS = 2
SWA_GROUP = SWA_HEADS // SWA_KV_HEADS
SWA_HEAD_DIM = 64
WINDOW = 128

PART_ROT = DIFF_HEAD_DIM // 4
LANES = 128
VMEM_LIMIT = 56 * 1024 * 1024

FFN_TM = 640
FFN_FC = 256
PROJ_TM = 640
FFN_FINAL_TM = 512
ATT_TK = 512
ATT_TQ = 2 * ATT_TK
ATT_CW = 256
LOG2E = 1.4426950408889634
SWA_TQ = 256


def _cparams(n_axes):
    return pltpu.CompilerParams(dimension_semantics=("arbitrary",) * n_axes,
                                vmem_limit_bytes=VMEM_LIMIT)


def _resident(shape):
    nd = len(shape)
    return pl.BlockSpec(shape, lambda *_: (0,) * nd, pipeline_mode=pl.Buffered(1))


def _rms(x, g, eps):
    return x * lax.rsqrt(jnp.mean(x * x, axis=-1, keepdims=True) + eps) * g


def _dot(a, b):
    return jnp.dot(a, b, preferred_element_type=F32)


def _dot_nt(a, b):
    return lax.dot_general(a, b, (((1,), (1,)), ((), ())), preferred_element_type=F32)


def _ffn_kernel(x_ref, g_ref, wg_ref, wu_ref, wd_ref, *rest, final=False, aliased=False):
    if final:
        fg_ref, o_ref, act_ref = rest
    elif aliased:
        _, o_ref, act_ref = rest
    else:
        o_ref, act_ref = rest
    x = x_ref[...].reshape(x_ref.shape[-2:])
    xn = _rms(x, g_ref[...], EPS).astype(BF16)
    for c in range(D_FF // FFN_FC):
        sl = slice(c * FFN_FC, (c + 1) * FFN_FC)
        g = _dot(xn, wg_ref[:, sl])
        u = _dot(xn, wu_ref[:, sl])
        act_ref[:, sl] = (g * (1.0 / (1.0 + jnp.exp(-g))) * u).astype(BF16)
    y = x + 0.5 * _dot(act_ref[...], wd_ref[...])
    if final:
        y = _rms(y, fg_ref[...], EPS)
    o_ref[...] = y.reshape(o_ref.shape)


def _ffn(h, g, wg, wu, wd):
    n = h.shape[0]
    tm = FFN_TM
    assert n % tm == 0
    row = pl.BlockSpec((tm, D_MODEL), lambda i: (i, 0))
    return pl.pallas_call(
        functools.partial(_ffn_kernel, final=False),
        grid=(n // tm,),
        in_specs=[row, _resident((1, D_MODEL)), _resident((D_MODEL, D_FF)), _resident((D_MODEL, D_FF)),
                  _resident((D_FF, D_MODEL))],
        out_specs=row,
        out_shape=jax.ShapeDtypeStruct((n, D_MODEL), F32),
        scratch_shapes=[pltpu.VMEM((tm, D_FF), BF16)],
        compiler_params=_cparams(1),
        name="ffn",
    )(h, g, wg, wu, wd)


def _ffn_first(x, front, g, wg, wu, wd):
    b, seq, d = x.shape
    tm = FFN_FINAL_TM
    assert seq % tm == 0
    weights = [_resident((1, D_MODEL)), _resident((D_MODEL, D_FF)), _resident((D_MODEL, D_FF)),
               _resident((D_FF, D_MODEL))]
    row = pl.BlockSpec((1, tm, d), lambda bb, i: (bb, i, 0))
    h3 = pl.pallas_call(
        _ffn_kernel,
        grid=(b, seq // tm),
        in_specs=[row] + weights,
        out_specs=row,
        out_shape=jax.ShapeDtypeStruct((b, seq + BLOCK, d), F32),
        scratch_shapes=[pltpu.VMEM((tm, D_FF), BF16)],
        compiler_params=_cparams(2),
        name="ffn_first",
    )(x, g, wg, wu, wd)
    return pl.pallas_call(
        functools.partial(_ffn_kernel, aliased=True),
        grid=(b,),
        in_specs=[pl.BlockSpec((BLOCK, d), lambda bb: (0, 0))] + weights + [pl.BlockSpec(memory_space=pl.ANY)],
        out_specs=pl.BlockSpec((1, BLOCK, d), lambda bb: (bb, seq // BLOCK, 0)),
        out_shape=jax.ShapeDtypeStruct(h3.shape, F32),
        scratch_shapes=[pltpu.VMEM((BLOCK, D_FF), BF16)],
        input_output_aliases={5: 0},
        compiler_params=_cparams(1),
        name="ffn_first_front",
    )(front, g, wg, wu, wd, h3)


def _ffn_final(h3, g, wg, wu, wd, fg, seq):
    b = h3.shape[0]
    tm = FFN_FINAL_TM
    assert seq % tm == 0
    row = pl.BlockSpec((1, tm, D_MODEL), lambda bb, i: (bb, i, 0))
    return pl.pallas_call(
        functools.partial(_ffn_kernel, final=True),
        grid=(b, seq // tm),
        in_specs=[row, _resident((1, D_MODEL)), _resident((D_MODEL, D_FF)), _resident((D_MODEL, D_FF)),
                  _resident((D_FF, D_MODEL)), _resident((1, D_MODEL))],
        out_specs=row,
        out_shape=jax.ShapeDtypeStruct((b, seq, D_MODEL), F32),
        scratch_shapes=[pltpu.VMEM((tm, D_FF), BF16)],
        compiler_params=_cparams(2),
        name="ffn_final",
    )(h3, g, wg, wu, wd, fg)


def _rope(y, c, s, half):
    w = y.shape[1]
    reps = w // LANES
    if reps > 1:
        c = jnp.concatenate([c] * reps, axis=1)
        s = jnp.concatenate([s] * reps, axis=1)
    lane = lax.broadcasted_iota(jnp.int32, y.shape, 1)
    first = (lane & 63) < half
    partner = jnp.where(first, pltpu.roll(y, w - half, 1), pltpu.roll(y, half, 1))
    return y * c + partner * s


def _rope_tables(pos, rot_dim):
    half = rot_dim // 2
    g = jnp.arange(LANES) & 63
    inv = ROPE_THETA ** (-(2 * (g % half)).astype(F32) / rot_dim)
    ang = pos.astype(F32)[:, None] * inv[None, :]
    rotated = (g < rot_dim)[None, :]
    cos = jnp.where(rotated, jnp.cos(ang), 1.0)
    sin = jnp.where(rotated, jnp.where((g < half)[None, :], -jnp.sin(ang), jnp.sin(ang)), 0.0)
    return cos, sin


AB_IN_PAD = 3 * 512 + MLA_Q_RANK + MLA_KV_RANK + LANES


def _ab_proj_kernel(h_ref, g_ref, win_ref, qn_ref, wuq_ref, kvn_ref, wukv_ref, ca_ref, sa_ref, cm_ref, sm_ref,
                    qa_ref, ka_ref, va_ref, qb_ref, kb_ref, vb_ref):
    xn = _rms(h_ref[0], g_ref[...], EPS).astype(BF16)
    ca, sa, cm, sm = ca_ref[...], sa_ref[...], cm_ref[...], sm_ref[...]
    half_a = PART_ROT // 2
    half_m = MLA_ROPE // 2
    qa = _rope(_dot(xn, win_ref[:, 0:512]), ca, sa, half_a)
    qa_ref[0] = (qa * (DIFF_HEAD_DIM ** -0.5 * LOG2E)).astype(BF16)
    ka_ref[0] = _rope(_dot(xn, win_ref[:, 512:1024]), ca, sa, half_a).astype(BF16)
    va_ref[0] = _dot(xn, win_ref[:, 1024:1536]).astype(BF16)
    cq = _dot(xn, win_ref[:, 1536:1792])
    ckv = _dot(xn, win_ref[:, 1792:2048])
    kr = _rope(_dot(xn, win_ref[:, 2048:2176]), cm, sm, half_m).astype(BF16)
    qb = _dot(_rms(cq, qn_ref[...], EPS).astype(BF16), wuq_ref[...])
    scale = (MLA_NOPE + MLA_ROPE) ** -0.5 * LOG2E
    for hh in range(MLA_HEADS):
        o = 2 * LANES * hh
        qb_ref[0, :, o:o + LANES] = (qb[:, o:o + LANES] * scale).astype(BF16)
        qb_ref[0, :, o + LANES:o + 2 * LANES] = (_rope(qb[:, o + LANES:o + 2 * LANES], cm, sm, half_m) * scale).astype(BF16)
    kv = _dot(_rms(ckv, kvn_ref[...], EPS).astype(BF16), wukv_ref[...])
    for hh in range(MLA_HEADS):
        o = 2 * LANES * hh
        kb_ref[0, :, o:o + LANES] = kv[:, LANES * hh:LANES * (hh + 1)].astype(BF16)
        kb_ref[0, :, o + LANES:o + 2 * LANES] = kr
    vb_ref[0] = kv[:, MLA_HEADS * MLA_NOPE:].astype(BF16)


def _ab_proj(h3, g, win, qn, wuq, kvn, wukv, tabs):
    b, l, _ = h3.shape
    tm = PROJ_TM
    assert l % tm == 0
    row = lambda w: pl.BlockSpec((1, tm, w), lambda j, bb: (bb, j, 0))
    tab = pl.BlockSpec((tm, LANES), lambda j, bb: (j, 0))
    outw = (512, 512, 512, 1024, 1024, 512)
    return pl.pallas_call(
        _ab_proj_kernel,
        grid=(l // tm, b),
        in_specs=[row(D_MODEL), _resident((1, D_MODEL)), _resident((D_MODEL, AB_IN_PAD)),
                  _resident((1, MLA_Q_RANK)), _resident((MLA_Q_RANK, 1024)),
                  _resident((1, MLA_KV_RANK)), _resident((MLA_KV_RANK, 1024)), tab, tab, tab, tab],
        out_specs=[row(w) for w in outw],
        out_shape=[jax.ShapeDtypeStruct((b, l, w), BF16) for w in outw],
        compiler_params=_cparams(2),
        name="ab_proj",
    )(h3, g, win, qn, wuq, kvn, wukv, *tabs)


def _ab_out_kernel(h_ref, oa_ref, ob_ref, wa_ref, wb_ref, o_ref):
    o_ref[...] = h_ref[...] + _dot(oa_ref[...], wa_ref[...]) + _dot(ob_ref[...], wb_ref[...])


def _ab_out(h, oa, ob, wa, wb):
    n = h.shape[0]
    tm = FFN_TM
    row = lambda w: pl.BlockSpec((tm, w), lambda i: (i, 0))
    return pl.pallas_call(
        _ab_out_kernel,
        grid=(n // tm,),
        in_specs=[row(D_MODEL), row(512), row(512), _resident((512, D_MODEL)), _resident((512, D_MODEL))],
        out_specs=row(D_MODEL),
        out_shape=jax.ShapeDtypeStruct((n, D_MODEL), F32),
        compiler_params=_cparams(1),
        name="ab_out",
    )(h, oa, ob, wa, wb)


def _transpose(x):
    return x.astype(F32).T.astype(x.dtype)


def _flash_step(k, vt, qt, m_ref, l_ref, acc_ref, mask):
    cw = min(ATT_CW, qt.shape[1])
    for c0 in range(0, qt.shape[1], cw):
        cs = slice(c0, c0 + cw)
        s = _dot(k, qt[:, cs])
        s = jnp.where(mask(c0, s.shape), s, NEG_INF)
        m_new = jnp.max(s, axis=0, keepdims=True)
        p = jnp.exp2(s - m_new)
        l_ref[:, cs] = jnp.sum(p, axis=0, keepdims=True)
        acc_ref[:, cs] = _dot(vt, p.astype(BF16))
        m_ref[:, cs] = m_new


def _produce(k, qt, s_ref, x_ref, chunks):
    for c0 in chunks:
        cs = slice(c0, c0 + ATT_CW)
        s = _dot(k, qt[:, cs])
        s_ref[:, cs] = s
        x_ref[:, cs] = jnp.max(s, axis=0, keepdims=True)


def _consume(s_ref, x_ref, vt, m_ref, l_ref, acc_ref, mask, chunks):
    for c0 in chunks:
        cs = slice(c0, c0 + ATT_CW)
        s = s_ref[:, cs]
        if mask is None:
            smax = x_ref[:, cs]
        else:
            s = jnp.where(mask(c0, s.shape), s, NEG_INF)
            smax = jnp.max(s, axis=0, keepdims=True)
        m_prev = m_ref[:, cs]
        m_new = jnp.maximum(m_prev, smax)
        alpha = jnp.exp2(m_prev - m_new)
        p = jnp.exp2(s - m_new)
        l_ref[:, cs] = alpha * l_ref[:, cs] + jnp.sum(p, axis=0, keepdims=True)
        acc_ref[:, cs] = alpha * acc_ref[:, cs] + _dot(vt, p.astype(BF16))
        m_ref[:, cs] = m_new


def _causal_attend(qt, k_ref, v_ref, scr, *, tq, tk, seq, front):
    vt_ref, m_ref, l_ref, acc_ref, s0, s1, x0, x1 = scr
    state = (m_ref, l_ref, acc_ref)
    f0 = 0 if front else seq
    kf = k_ref[0, f0:f0 + BLOCK, :]

    def valid(c0, shape):
        return lax.broadcasted_iota(jnp.int32, shape, 0) >= N_FRONT

    def causal(key0):
        def mask(c0, shape):
            key = lax.broadcasted_iota(jnp.int32, shape, 0) + key0
            qry = (lax.broadcasted_iota(jnp.int32, shape, 1) + c0) & (tq - 1)
            return key <= qry
        return mask

    if front:
        both = lambda c0, shape: valid(c0, shape) & causal(0)(c0, shape)
        _flash_step(kf, _transpose(v_ref[0, f0:f0 + BLOCK, :]), qt, *state, both)
        return
    assert tq == 2 * tk
    i = pl.program_id(2)
    nblk = v_ref.shape[1] // BLOCK
    sub = tk // BLOCK
    chunks = list(range(0, qt.shape[1], ATT_CW))
    late = [c0 for c0 in chunks if (c0 & (tq - 1)) + ATT_CW > tk]

    @pl.when(i == 0)
    def _fill():
        def fill(c, carry):
            st = pl.multiple_of(c * BLOCK, BLOCK)
            vt_ref[c] = _transpose(v_ref[0, pl.ds(st, BLOCK), :])
            return carry
        lax.fori_loop(0, nblk, fill, 0)

    def k_block(j):
        return k_ref[0, pl.ds(pl.multiple_of(j * tk, tk), tk), :]

    def vt_block(j):
        return jnp.concatenate([vt_ref[j * sub + u] for u in range(sub)], axis=1)

    _produce(k_block(0), qt, s0, x0, chunks)
    _flash_step(kf, vt_ref[seq // BLOCK], qt, *state, valid)

    def body(p, carry):
        _produce(k_block(2 * p + 1), qt, s1, x1, chunks)
        _consume(s0, x0, vt_block(2 * p), *state, None, chunks)
        _produce(k_block(2 * p + 2), qt, s0, x0, chunks)
        _consume(s1, x1, vt_block(2 * p + 1), *state, None, chunks)
        return carry

    lax.fori_loop(0, i, body, 0)
    _produce(k_block(2 * i + 1), qt, s1, x1, late)
    _consume(s0, x0, vt_block(2 * i), *state, causal(0), chunks)
    _consume(s1, x1, vt_block(2 * i + 1), *state, causal(tk), late)


def _diff_kernel(q_ref, k_ref, v_ref, lq1_ref, lk1_ref, lq2_ref, lk2_ref, sub_ref, *rest,
                 tq, tk, seq, front, lambda_init):
    if front:
        _, o_ref, *scr = rest
    else:
        o_ref, *scr = rest
    m_ref, l_ref, acc_ref = scr[1:4]
    qt = _transpose(q_ref[0])
    d = lax.broadcasted_iota(jnp.int32, qt.shape, 0)
    zero = jnp.zeros_like(qt)
    qs = jnp.concatenate([jnp.where(d < DIFF_HEAD_DIM, qt, zero), jnp.where(d >= DIFF_HEAD_DIM, qt, zero)], axis=1)
    _causal_attend(qs, k_ref, v_ref, scr, tq=tq, tk=tk, seq=seq, front=front)
    o = acc_ref[...] / l_ref[...]
    lam = (jnp.exp(jnp.sum(lq1_ref[...] * lk1_ref[...], keepdims=True))
           - jnp.exp(jnp.sum(lq2_ref[...] * lk2_ref[...], keepdims=True)) + lambda_init)
    w = o[:, :tq] - lam * o[:, tq:]
    w = w * lax.rsqrt(jnp.mean(w * w, axis=0, keepdims=True) + DIFF_EPS) * sub_ref[...]
    o_ref[0] = (w * (1.0 - lambda_init)).T.astype(BF16)


def _mla_kernel(q_ref, k_ref, v_ref, *rest, tq, tk, seq, front):
    if front:
        _, o_ref, *scr = rest
    else:
        o_ref, *scr = rest
    m_ref, l_ref, acc_ref = scr[1:4]
    _causal_attend(_transpose(q_ref[0]), k_ref, v_ref, scr, tq=tq, tk=tk, seq=seq, front=front)
    o_ref[0] = (acc_ref[...] / l_ref[...]).T.astype(BF16)


def _causal_attention(kernel, q, k, v, extra, *, heads, dk, reps, tq, tk, seq, name):
    b, l, _ = q.shape
    dv = LANES
    outs = None
    for front in (False, True):
        t = BLOCK if front else tq
        if front:
            grid = (b, heads, 1)
            qmap = lambda bb, hh, i: (bb, seq // BLOCK, hh)
        else:
            grid = (b, heads, seq // t)
            qmap = lambda bb, hh, i: (bb, i, hh)
        if front:
            kvl, kvmap = BLOCK, qmap
        else:
            kvl, kvmap = l, lambda bb, hh, i: (bb, 0, hh)
        in_specs = [pl.BlockSpec((1, t, dk), qmap), pl.BlockSpec((1, kvl, dk), kvmap),
                    pl.BlockSpec((1, kvl, dv), kvmap)]
        in_specs += [_resident(e.shape) for e in extra]
        args = [q, k, v, *extra]
        aliases = {}
        if front:
            in_specs.append(pl.BlockSpec(memory_space=pl.ANY))
            aliases = {len(args): 0}
            args.append(outs)
        outs = pl.pallas_call(
            functools.partial(kernel, tq=t, tk=min(t, tk), seq=seq, front=front),
            grid=grid,
            in_specs=in_specs,
            out_specs=pl.BlockSpec((1, t, dv), qmap),
            out_shape=jax.ShapeDtypeStruct((b, l, heads * dv), BF16),
            scratch_shapes=[pltpu.VMEM((1 if front else l // BLOCK, dv, BLOCK), BF16),
                            pltpu.VMEM((1, reps * t), F32), pltpu.VMEM((1, reps * t), F32),
                            pltpu.VMEM((dv, reps * t), F32)]
                           + [pltpu.VMEM((8 if front else tk, reps * t), F32)] * 2
                           + [pltpu.VMEM((1, reps * t), F32)] * 2,
            input_output_aliases=aliases,
            compiler_params=_cparams(3),
            name=name + ("_front" if front else ""),
        )(*args)
    return outs


SWA_Q_COLS = SWA_HEADS * SWA_HEAD_DIM
SWA_KV_COLS = SWA_KV_HEADS * SWA_HEAD_DIM
SWA_IN_COLS = SWA_Q_COLS + 2 * SWA_KV_COLS


def _swa_proj_kernel(h_ref, g_ref, w_ref, b_ref, ca_ref, sa_ref, q_ref, k_ref, v_ref):
    xn = _rms(h_ref[0], g_ref[...], EPS).astype(BF16)
    ca, sa = ca_ref[...], sa_ref[...]
    half = PART_ROT // 2
    nq = SWA_Q_COLS
    nk = SWA_KV_COLS
    q = _rope(_dot(xn, w_ref[:, 0:nq]) + b_ref[:, 0:nq], ca, sa, half)
    q_ref[0] = (q * (SWA_HEAD_DIM ** -0.5 * LOG2E)).astype(BF16)
    k_ref[0] = _rope(_dot(xn, w_ref[:, nq:nq + nk]) + b_ref[:, nq:nq + nk], ca, sa, half).astype(BF16)
    v_ref[0] = (_dot(xn, w_ref[:, nq + nk:]) + b_ref[:, nq + nk:]).astype(BF16)


def _swa_proj(h3, g, w, bias, tabs):
    b, l, _ = h3.shape
    tm = PROJ_TM
    row = lambda wd: pl.BlockSpec((1, tm, wd), lambda j, bb: (bb, j, 0))
    tab = pl.BlockSpec((tm, LANES), lambda j, bb: (j, 0))
    outw = (SWA_Q_COLS, SWA_KV_COLS, SWA_KV_COLS)
    return pl.pallas_call(
        _swa_proj_kernel,
        grid=(l // tm, b),
        in_specs=[row(D_MODEL), _resident((1, D_MODEL)), _resident((D_MODEL, SWA_IN_COLS)),
                  _resident((1, SWA_IN_COLS)), tab, tab],
        out_specs=[row(w_) for w_ in outw],
        out_shape=[jax.ShapeDtypeStruct((b, l, w_), BF16) for w_ in outw],
        compiler_params=_cparams(2),
        name="swa_proj",
    )(h3, g, w, bias, *tabs)


def _swa_kernel(sinks_ref, q_ref, k_ref, v_ref, *rest, tq, seq, front):
    if front:
        _, o_ref, s_a, s_b = rest
    else:
        o_ref, s_a, s_b = rest
    hd_dim = SWA_HEAD_DIM
    f0 = 0 if front else seq
    meta_k = k_ref[0, f0 + N_FRONT:f0 + BLOCK, :]
    meta_v = v_ref[0, f0 + N_FRONT:f0 + BLOCK, :]
    if front:
        nk = N_META
        kc, vc = meta_k, meta_v
        key = lax.broadcasted_iota(jnp.int32, (nk, tq), 0) + N_FRONT
        qry = lax.broadcasted_iota(jnp.int32, (nk, tq), 1)
        allowed = key <= qry
    else:
        i = pl.program_id(1)
        bw = tq + WINDOW
        nk = bw + N_META
        bs = pl.multiple_of(jnp.maximum(i * tq - WINDOW, 0), BLOCK)
        kc = jnp.concatenate([k_ref[0, pl.ds(bs, bw), :], meta_k], axis=0)
        vc = jnp.concatenate([v_ref[0, pl.ds(bs, bw), :], meta_v], axis=0)
        key = lax.broadcasted_iota(jnp.int32, (nk, tq), 0)
        qry = lax.broadcasted_iota(jnp.int32, (nk, tq), 1)
        dist = (i * tq + qry) - (bs + key)
        allowed = (key >= bw) | ((dist >= 0) & (dist < WINDOW))
    bias = jnp.where(allowed, 0.0, NEG_INF)
    nkp = -(-nk // LANES) * LANES
    vt = _transpose(jnp.concatenate([vc, jnp.zeros((nkp - nk, vc.shape[1]), vc.dtype)], axis=0))
    qt = _transpose(q_ref[0])
    zh = jnp.zeros((hd_dim, tq), BF16)
    sbuf = (s_a, s_b)

    def produce(g, rr):
        hd = g * SWA_GROUP + rr
        qh = qt[hd * hd_dim:(hd + 1) * hd_dim, :]
        rhs = jnp.concatenate([qh, zh] if g == 0 else [zh, qh], axis=0)
        sbuf[g][:, rr * tq:(rr + 1) * tq] = _dot(kc, rhs)

    outs = [None] * SWA_HEADS

    def consume(g, rr):
        hd = g * SWA_GROUP + rr
        sink = sinks_ref[hd] * LOG2E
        s = sbuf[g][:, rr * tq:(rr + 1) * tq] + bias
        m = jnp.maximum(jnp.max(s, axis=0, keepdims=True), sink)
        e = jnp.exp2(s - m)
        den = jnp.sum(e, axis=0, keepdims=True) + jnp.exp2(sink - m)
        eb = jnp.concatenate([e.astype(BF16), jnp.zeros((nkp - nk, tq), BF16)], axis=0)
        o = _dot(vt, eb)
        outs[hd] = o[g * hd_dim:(g + 1) * hd_dim] / den

    for rr in range(SWA_GROUP):
        produce(0, rr)
    for rr in range(SWA_GROUP):
        produce(1, rr)
        consume(0, rr)
    for rr in range(SWA_GROUP):
        consume(1, rr)
    o_ref[0] = jnp.concatenate(outs, axis=0).T.astype(BF16)


def _swa_attention(q, k, v, sinks, *, seq):
    b, l, _ = q.shape
    nq = SWA_Q_COLS
    outs = None
    for front in (False, True):
        tq = BLOCK if front else SWA_TQ
        nk = N_META if front else tq + WINDOW + N_META
        if front:
            grid = (b, 1)
            qmap = lambda bb, i: (bb, seq // BLOCK, 0)
            kvl, kvmap = BLOCK, qmap
        else:
            grid = (b, seq // tq)
            qmap = lambda bb, i: (bb, i, 0)
            kvl, kvmap = l, lambda bb, i: (bb, 0, 0)
        in_specs = [pl.BlockSpec(memory_space=pltpu.SMEM), pl.BlockSpec((1, tq, nq), qmap),
                    pl.BlockSpec((1, kvl, k.shape[2]), kvmap), pl.BlockSpec((1, kvl, v.shape[2]), kvmap)]
        args = [sinks, q, k, v]
        aliases = {}
        if front:
            in_specs.append(pl.BlockSpec(memory_space=pl.ANY))
            aliases = {len(args): 0}
            args.append(outs)
        outs = pl.pallas_call(
            functools.partial(_swa_kernel, tq=tq, seq=seq, front=front),
            grid=grid,
            in_specs=in_specs,
            out_specs=pl.BlockSpec((1, tq, nq), qmap),
            out_shape=jax.ShapeDtypeStruct((b, l, nq), BF16),
            scratch_shapes=[pltpu.VMEM((nk, SWA_GROUP * tq), F32)] * 2,
            input_output_aliases=aliases,
            compiler_params=_cparams(2),
            name="swa_attn" + ("_front" if front else ""),
        )(*args)
    return outs


def _swa_out_kernel(h_ref, o_ref_in, w_ref, b_ref, o_ref):
    o_ref[...] = h_ref[...] + _dot(o_ref_in[...], w_ref[...]) + b_ref[...]


def _swa_out(h, o, w, bias):
    n = h.shape[0]
    tm = FFN_TM
    row = lambda wd: pl.BlockSpec((tm, wd), lambda i: (i, 0))
    return pl.pallas_call(
        _swa_out_kernel,
        grid=(n // tm,),
        in_specs=[row(D_MODEL), row(D_MODEL), _resident((D_MODEL, D_MODEL)), _resident((1, D_MODEL))],
        out_specs=row(D_MODEL),
        out_shape=jax.ShapeDtypeStruct((n, D_MODEL), F32),
        compiler_params=_cparams(1),
        name="swa_out",
    )(h, o, w, bias)


def _ab_weights(w_in, w_uq, w_ukv):
    win = jnp.pad(w_in, ((0, 0), (0, AB_IN_PAD - w_in.shape[1]))).astype(BF16)
    wq = w_uq.reshape(MLA_Q_RANK, MLA_HEADS, MLA_NOPE + MLA_ROPE)
    wq = jnp.pad(wq, ((0, 0), (0, 0), (0, 2 * LANES - MLA_NOPE - MLA_ROPE))).reshape(MLA_Q_RANK, MLA_HEADS * 2 * LANES)
    wkv = w_ukv.reshape(MLA_KV_RANK, MLA_HEADS, 2, MLA_NOPE).transpose(0, 2, 1, 3).reshape(MLA_KV_RANK, -1)
    return win, wq.astype(BF16), wkv.astype(BF16)


def kernel(x, meta_tokens, ffn1_norm, ffn1_w_gate, ffn1_w_up, ffn1_w_down, mix_norm, ab_w_in, diff_lambda_q1, diff_lambda_k1, diff_lambda_q2, diff_lambda_k2, diff_subln, mla_q_norm, mla_w_uq, mla_kv_norm, mla_w_ukv, ab_w_out, swa_w_qkv, swa_b_qkv, swa_sinks, swa_w_out, swa_b_out, ffn2_norm, ffn2_w_gate, ffn2_w_up, ffn2_w_down, final_norm):
    b, seq, d = x.shape
    depth = ffn1_norm.shape[0]
    l = seq + BLOCK
    n = b * l
    front = jnp.concatenate([jnp.zeros((N_FRONT, d), x.dtype), meta_tokens.astype(x.dtype)], axis=0)

    pos = jnp.concatenate([jnp.arange(seq) + N_META, jnp.maximum(jnp.arange(BLOCK) - N_FRONT, 0)])
    tabs_p = _rope_tables(pos, PART_ROT)
    tabs_m = _rope_tables(pos, MLA_ROPE)
    row2 = lambda a: a.reshape(1, -1)

    out = None
    for ly in range(depth):
        w1 = (row2(ffn1_norm[ly]), ffn1_w_gate[ly].astype(BF16), ffn1_w_up[ly].astype(BF16),
              ffn1_w_down[ly].astype(BF16))
        h = _ffn_first(x, front, *w1).reshape(n, d) if ly == 0 else _ffn(h, *w1)
        h3 = h.reshape(b, l, d)
        if ly % 2 == 0:
            e = ly // 2
            lambda_init = 0.8 - 0.6 * math.exp(-0.3 * ly)
            win, wuq, wukv = _ab_weights(ab_w_in[e], mla_w_uq[e], mla_w_ukv[e])
            qa, ka, va, qb, kb, vb = _ab_proj(h3, row2(mix_norm[ly]), win, row2(mla_q_norm[e]), wuq,
                                              row2(mla_kv_norm[e]), wukv, tabs_p + tabs_m)
            extra = [row2(diff_lambda_q1[e]), row2(diff_lambda_k1[e]), row2(diff_lambda_q2[e]),
                     row2(diff_lambda_k2[e]), diff_subln[e].reshape(-1, 1)]
            oa = _causal_attention(functools.partial(_diff_kernel, lambda_init=lambda_init), qa, ka, va, extra,
                                   heads=DIFF_HEADS, dk=LANES, reps=2, tq=ATT_TQ, tk=ATT_TK, seq=seq,
                                   name="diff_attn")
            ob = _causal_attention(_mla_kernel, qb, kb, vb, [], heads=MLA_HEADS, dk=2 * LANES, reps=1,
                                   tq=ATT_TQ, tk=ATT_TK, seq=seq, name="mla_attn")
            wo = ab_w_out[e].astype(BF16)
            h = _ab_out(h, oa.reshape(n, -1), ob.reshape(n, -1), wo[:512], wo[512:])
        else:
            o = ly // 2
            q, k, v = _swa_proj(h3, row2(mix_norm[ly]), swa_w_qkv[o].astype(BF16), row2(swa_b_qkv[o]), tabs_p)
            att = _swa_attention(q, k, v, swa_sinks[o], seq=seq)
            h = _swa_out(h, att.reshape(n, -1), swa_w_out[o].astype(BF16), row2(swa_b_out[o]))
        wg, wu, wd = ffn2_w_gate[ly].astype(BF16), ffn2_w_up[ly].astype(BF16), ffn2_w_down[ly].astype(BF16)
        if ly == depth - 1:
            out = _ffn_final(h.reshape(b, l, d), row2(ffn2_norm[ly]), wg, wu, wd, row2(final_norm), seq)
        else:
            h = _ffn(h, row2(ffn2_norm[ly]), wg, wu, wd)
    return out
```

```python
import functools
import math

import jax
import jax.numpy as jnp
from jax import lax
from jax.experimental import pallas as pl
from jax.experimental.pallas import tpu as pltpu

F32 = jnp.float32
BF16 = jnp.bfloat16

D_MODEL = 1024
N_META = 16
BLOCK = 128
N_FRONT = BLOCK - N_META
ROPE_THETA = 500000.0
EPS = 1e-6
NEG_INF = -1e30
D_FF = 2816

DIFF_HEADS = 4
DIFF_HEAD_DIM = 64
DIFF_V_DIM = 2 * DIFF_HEAD_DIM
DIFF_EPS = 1e-5
MLA_HEADS = 4
MLA_NOPE = 128
MLA_ROPE = 64
MLA_V = 128
MLA_Q_RANK = 256
MLA_KV_RANK = 256
SWA_HEADS = 16
SWA_KV_HEADS = 2
SWA_GROUP = SWA_HEADS // SWA_KV_HEADS
SWA_HEAD_DIM = 64
WINDOW = 128

PART_ROT = DIFF_HEAD_DIM // 4
LANES = 128
VMEM_LIMIT = 56 * 1024 * 1024

FFN_TM = 640
FFN_FC = 256
PROJ_TM = 640
FFN_FINAL_TM = 512
ATT_TK = 512
ATT_TQ = 2 * ATT_TK
ATT_CW = 256
VT_FILL_UNROLL = 5
LOG2E = 1.4426950408889634
SWA_TQ = 256


def _cparams(n_axes):
    return pltpu.CompilerParams(dimension_semantics=("arbitrary",) * n_axes,
                                vmem_limit_bytes=VMEM_LIMIT)


def _resident(shape):
    nd = len(shape)
    return pl.BlockSpec(shape, lambda *_: (0,) * nd, pipeline_mode=pl.Buffered(1))


def _rms(x, g, eps):
    return x * lax.rsqrt(jnp.mean(x * x, axis=-1, keepdims=True) + eps) * g


def _dot(a, b):
    return jnp.dot(a, b, preferred_element_type=F32)


def _dot_nt(a, b):
    return lax.dot_general(a, b, (((1,), (1,)), ((), ())), preferred_element_type=F32)


def _ffn_kernel(x_ref, *refs, final=False, aliased=False, n_mix=0, mix_bias=False):
    mix, refs = refs[:2 * n_mix], refs[2 * n_mix:]
    if mix_bias:
        mb_ref, refs = refs[0], refs[1:]
    g_ref, wg_ref, wu_ref, wd_ref, *rest = refs
    if final:
        fg_ref, o_ref, act_ref = rest
    elif aliased:
        _, o_ref, act_ref = rest
    else:
        o_ref, act_ref = rest
    x = x_ref[...].reshape(x_ref.shape[-2:])
    for a_ref, w_ref in zip(mix[0::2], mix[1::2]):
        x = x + _dot(a_ref[...].reshape(a_ref.shape[-2:]), w_ref[...])
    if mix_bias:
        x = x + mb_ref[...]
    xn = _rms(x, g_ref[...], EPS).astype(BF16)
    for c in range(D_FF // FFN_FC):
        sl = slice(c * FFN_FC, (c + 1) * FFN_FC)
        g = _dot(xn, wg_ref[:, sl])
        u = _dot(xn, wu_ref[:, sl])
        act_ref[:, sl] = (g * (1.0 / (1.0 + jnp.exp(-g))) * u).astype(BF16)
    y = x + 0.5 * _dot(act_ref[...], wd_ref[...])
    if final:
        y = _rms(y, fg_ref[...], EPS)
    o_ref[...] = y.reshape(o_ref.shape)


def _mix_args(mix, bias, row):
    args, specs = [], []
    for a, w in mix:
        args += [a, w]
        specs += [row(a.shape[-1]), _resident(w.shape)]
    if bias is not None:
        args.append(bias)
        specs.append(_resident(bias.shape))
    return args, specs, dict(n_mix=len(mix), mix_bias=bias is not None)


def _ffn(h, g, wg, wu, wd, mix=(), bias=None):
    n = h.shape[0]
    tm = FFN_TM
    assert n % tm == 0
    row = lambda w: pl.BlockSpec((tm, w), lambda i: (i, 0))
    margs, mspecs, mkw = _mix_args(mix, bias, row)
    return pl.pallas_call(
        functools.partial(_ffn_kernel, **mkw),
        grid=(n // tm,),
        in_specs=[row(D_MODEL)] + mspecs + [_resident((1, D_MODEL)), _resident((D_MODEL, D_FF)),
                                            _resident((D_MODEL, D_FF)), _resident((D_FF, D_MODEL))],
        out_specs=row(D_MODEL),
        out_shape=jax.ShapeDtypeStruct((n, D_MODEL), F32),
        scratch_shapes=[pltpu.VMEM((tm, D_FF), BF16)],
        compiler_params=_cparams(1),
        name="ffn",
    )(h, *margs, g, wg, wu, wd)


def _ffn_first(x, front, g, wg, wu, wd):
    b, seq, d = x.shape
    tm = FFN_FINAL_TM
    assert seq % tm == 0
    weights = [_resident((1, D_MODEL)), _resident((D_MODEL, D_FF)), _resident((D_MODEL, D_FF)),
               _resident((D_FF, D_MODEL))]
    row = pl.BlockSpec((1, tm, d), lambda bb, i: (bb, i, 0))
    h3 = pl.pallas_call(
        _ffn_kernel,
        grid=(b, seq // tm),
        in_specs=[row] + weights,
        out_specs=row,
        out_shape=jax.ShapeDtypeStruct((b, seq + BLOCK, d), F32),
        scratch_shapes=[pltpu.VMEM((tm, D_FF), BF16)],
        compiler_params=_cparams(2),
        name="ffn_first",
    )(x, g, wg, wu, wd)
    return pl.pallas_call(
        functools.partial(_ffn_kernel, aliased=True),
        grid=(b,),
        in_specs=[pl.BlockSpec((BLOCK, d), lambda bb: (0, 0))] + weights + [pl.BlockSpec(memory_space=pl.ANY)],
        out_specs=pl.BlockSpec((1, BLOCK, d), lambda bb: (bb, seq // BLOCK, 0)),
        out_shape=jax.ShapeDtypeStruct(h3.shape, F32),
        scratch_shapes=[pltpu.VMEM((BLOCK, D_FF), BF16)],
        input_output_aliases={5: 0},
        compiler_params=_cparams(1),
        name="ffn_first_front",
    )(front, g, wg, wu, wd, h3)


def _ffn_final(h3, g, wg, wu, wd, fg, seq, mix=(), bias=None):
    b = h3.shape[0]
    tm = FFN_FINAL_TM
    assert seq % tm == 0
    row = lambda w: pl.BlockSpec((1, tm, w), lambda bb, i: (bb, i, 0))
    margs, mspecs, mkw = _mix_args(mix, bias, row)
    return pl.pallas_call(
        functools.partial(_ffn_kernel, final=True, **mkw),
        grid=(b, seq // tm),
        in_specs=[row(D_MODEL)] + mspecs + [_resident((1, D_MODEL)), _resident((D_MODEL, D_FF)),
                                            _resident((D_MODEL, D_FF)), _resident((D_FF, D_MODEL)),
                                            _resident((1, D_MODEL))],
        out_specs=row(D_MODEL),
        out_shape=jax.ShapeDtypeStruct((b, seq, D_MODEL), F32),
        scratch_shapes=[pltpu.VMEM((tm, D_FF), BF16)],
        compiler_params=_cparams(2),
        name="ffn_final",
    )(h3, *margs, g, wg, wu, wd, fg)


def _rope(y, c, s, half):
    w = y.shape[1]
    reps = w // LANES
    if reps > 1:
        c = jnp.concatenate([c] * reps, axis=1)
        s = jnp.concatenate([s] * reps, axis=1)
    lane = lax.broadcasted_iota(jnp.int32, y.shape, 1)
    first = (lane & 63) < half
    partner = jnp.where(first, pltpu.roll(y, w - half, 1), pltpu.roll(y, half, 1))
    return y * c + partner * s


def _rope_tables(pos, rot_dim):
    half = rot_dim // 2
    g = jnp.arange(LANES) & 63
    inv = ROPE_THETA ** (-(2 * (g % half)).astype(F32) / rot_dim)
    ang = pos.astype(F32)[:, None] * inv[None, :]
    rotated = (g < rot_dim)[None, :]
    cos = jnp.where(rotated, jnp.cos(ang), 1.0)
    sin = jnp.where(rotated, jnp.where((g < half)[None, :], -jnp.sin(ang), jnp.sin(ang)), 0.0)
    return cos, sin


AB_IN_PAD = 3 * 512 + MLA_Q_RANK + MLA_KV_RANK + LANES


def _ab_proj_kernel(h_ref, g_ref, win_ref, qn_ref, wuq_ref, kvn_ref, wukv_ref, ca_ref, sa_ref, cm_ref, sm_ref,
                    qa_ref, ka_ref, va_ref, qb_ref, kb_ref, vb_ref):
    xn = _rms(h_ref[0], g_ref[...], EPS).astype(BF16)
    ca, sa, cm, sm = ca_ref[...], sa_ref[...], cm_ref[...], sm_ref[...]
    half_a = PART_ROT // 2
    half_m = MLA_ROPE // 2
    qa = _rope(_dot(xn, win_ref[:, 0:512]), ca, sa, half_a)
    qa_ref[0] = (qa * (DIFF_HEAD_DIM ** -0.5 * LOG2E)).astype(BF16)
    ka_ref[0] = _rope(_dot(xn, win_ref[:, 512:1024]), ca, sa, half_a).astype(BF16)
    va_ref[0] = _dot(xn, win_ref[:, 1024:1536]).astype(BF16)
    cq = _dot(xn, win_ref[:, 1536:1792])
    ckv = _dot(xn, win_ref[:, 1792:2048])
    kr = _rope(_dot(xn, win_ref[:, 2048:2176]), cm, sm, half_m).astype(BF16)
    qb = _dot(_rms(cq, qn_ref[...], EPS).astype(BF16), wuq_ref[...])
    scale = (MLA_NOPE + MLA_ROPE) ** -0.5 * LOG2E
    for hh in range(MLA_HEADS):
        o = 2 * LANES * hh
        qb_ref[0, :, o:o + LANES] = (qb[:, o:o + LANES] * scale).astype(BF16)
        qb_ref[0, :, o + LANES:o + 2 * LANES] = (_rope(qb[:, o + LANES:o + 2 * LANES], cm, sm, half_m) * scale).astype(BF16)
    kv = _dot(_rms(ckv, kvn_ref[...], EPS).astype(BF16), wukv_ref[...])
    for hh in range(MLA_HEADS):
        o = 2 * LANES * hh
        kb_ref[0, :, o:o + LANES] = kv[:, LANES * hh:LANES * (hh + 1)].astype(BF16)
        kb_ref[0, :, o + LANES:o + 2 * LANES] = kr
    vb_ref[0] = kv[:, MLA_HEADS * MLA_NOPE:].astype(BF16)


def _ab_proj(h3, g, win, qn, wuq, kvn, wukv, tabs):
    b, l, _ = h3.shape
    tm = PROJ_TM
    assert l % tm == 0
    row = lambda w: pl.BlockSpec((1, tm, w), lambda j, bb: (bb, j, 0))
    tab = pl.BlockSpec((tm, LANES), lambda j, bb: (j, 0))
    outw = (512, 512, 512, 1024, 1024, 512)
    return pl.pallas_call(
        _ab_proj_kernel,
        grid=(l // tm, b),
        in_specs=[row(D_MODEL), _resident((1, D_MODEL)), _resident((D_MODEL, AB_IN_PAD)),
                  _resident((1, MLA_Q_RANK)), _resident((MLA_Q_RANK, 1024)),
                  _resident((1, MLA_KV_RANK)), _resident((MLA_KV_RANK, 1024)), tab, tab, tab, tab],
        out_specs=[row(w) for w in outw],
        out_shape=[jax.ShapeDtypeStruct((b, l, w), BF16) for w in outw],
        compiler_params=_cparams(2),
        name="ab_proj",
    )(h3, g, win, qn, wuq, kvn, wukv, *tabs)


def _transpose(x):
    return x.astype(F32).T.astype(x.dtype)


def _flash_step(k, vt, qt, m_ref, l_ref, acc_ref, mask):
    cw = min(ATT_CW, qt.shape[1])
    for c0 in range(0, qt.shape[1], cw):
        cs = slice(c0, c0 + cw)
        s = _dot(k, qt[:, cs])
        s = jnp.where(mask(c0, s.shape), s, NEG_INF)
        m_new = jnp.max(s, axis=0, keepdims=True)
        p = jnp.exp2(s - m_new)
        l_ref[:, cs] = jnp.sum(p, axis=0, keepdims=True)
        acc_ref[:, cs] = _dot(vt, p.astype(BF16))
        m_ref[:, cs] = m_new


def _produce(k, qt, s_ref, x_ref, chunks):
    for c0 in chunks:
        cs = slice(c0, c0 + ATT_CW)
        s = _dot(k, qt[:, cs])
        s_ref[:, cs] = s
        x_ref[:, cs] = jnp.max(s, axis=0, keepdims=True)


def _consume(s_ref, x_ref, vt, m_ref, l_ref, acc_ref, mask, chunks):
    for c0 in chunks:
        cs = slice(c0, c0 + ATT_CW)
        s = s_ref[:, cs]
        if mask is None:
            smax = x_ref[:, cs]
        else:
            s = jnp.where(mask(c0, s.shape), s, NEG_INF)
            smax = jnp.max(s, axis=0, keepdims=True)
        m_prev = m_ref[:, cs]
        m_new = jnp.maximum(m_prev, smax)
        alpha = jnp.exp2(m_prev - m_new)
        p = jnp.exp2(s - m_new)
        l_ref[:, cs] = alpha * l_ref[:, cs] + jnp.sum(p, axis=0, keepdims=True)
        acc_ref[:, cs] = alpha * acc_ref[:, cs] + _dot(vt, p.astype(BF16))
        m_ref[:, cs] = m_new


def _causal_attend(qt, k_ref, v_ref, scr, *, tq, tk, seq, front):
    vt_ref, m_ref, l_ref, acc_ref, s0, s1, x0, x1 = scr
    state = (m_ref, l_ref, acc_ref)
    f0 = 0 if front else seq
    kf = k_ref[0, f0:f0 + BLOCK, :]

    def valid(c0, shape):
        return lax.broadcasted_iota(jnp.int32, shape, 0) >= N_FRONT

    def causal(key0):
        def mask(c0, shape):
            key = lax.broadcasted_iota(jnp.int32, shape, 0) + key0
            qry = (lax.broadcasted_iota(jnp.int32, shape, 1) + c0) & (tq - 1)
            return key <= qry
        return mask

    if front:
        both = lambda c0, shape: valid(c0, shape) & causal(0)(c0, shape)
        _flash_step(kf, _transpose(v_ref[0, f0:f0 + BLOCK, :]), qt, *state, both)
        return
    assert tq == 2 * tk
    i = pl.program_id(2)
    nblk = v_ref.shape[1] // BLOCK
    sub = tk // BLOCK
    chunks = list(range(0, qt.shape[1], ATT_CW))
    late = [c0 for c0 in chunks if (c0 & (tq - 1)) + ATT_CW > tk]

    @pl.when(i == 0)
    def _fill():
        def fill(c, carry):
            st = pl.multiple_of(c * BLOCK, BLOCK)
            vt_ref[c] = _transpose(v_ref[0, pl.ds(st, BLOCK), :])
            return carry
        lax.fori_loop(0, nblk, fill, 0, unroll=VT_FILL_UNROLL)

    def k_block(j):
        return k_ref[0, pl.ds(pl.multiple_of(j * tk, tk), tk), :]

    def vt_block(j):
        return jnp.concatenate([vt_ref[j * sub + u] for u in range(sub)], axis=1)

    _produce(k_block(0), qt, s0, x0, chunks)
    _flash_step(kf, vt_ref[seq // BLOCK], qt, *state, valid)

    def body(p, carry):
        _produce(k_block(2 * p + 1), qt, s1, x1, chunks)
        _consume(s0, x0, vt_block(2 * p), *state, None, chunks)
        _produce(k_block(2 * p + 2), qt, s0, x0, chunks)
        _consume(s1, x1, vt_block(2 * p + 1), *state, None, chunks)
        return carry

    lax.fori_loop(0, i, body, 0)
    _produce(k_block(2 * i + 1), qt, s1, x1, late)
    _consume(s0, x0, vt_block(2 * i), *state, causal(0), chunks)
    _consume(s1, x1, vt_block(2 * i + 1), *state, causal(tk), late)


def _diff_kernel(q_ref, k_ref, v_ref, lq1_ref, lk1_ref, lq2_ref, lk2_ref, sub_ref, *rest,
                 tq, tk, seq, front, lambda_init):
    if front:
        _, o_ref, *scr = rest
    else:
        o_ref, *scr = rest
    m_ref, l_ref, acc_ref = scr[1:4]
    qt = _transpose(q_ref[0])
    d = lax.broadcasted_iota(jnp.int32, qt.shape, 0)
    zero = jnp.zeros_like(qt)
    qs = jnp.concatenate([jnp.where(d < DIFF_HEAD_DIM, qt, zero), jnp.where(d >= DIFF_HEAD_DIM, qt, zero)], axis=1)
    _causal_attend(qs, k_ref, v_ref, scr, tq=tq, tk=tk, seq=seq, front=front)
    o = acc_ref[...] / l_ref[...]
    lam = (jnp.exp(jnp.sum(lq1_ref[...] * lk1_ref[...], keepdims=True))
           - jnp.exp(jnp.sum(lq2_ref[...] * lk2_ref[...], keepdims=True)) + lambda_init)
    w = o[:, :tq] - lam * o[:, tq:]
    w = w * lax.rsqrt(jnp.mean(w * w, axis=0, keepdims=True) + DIFF_EPS) * sub_ref[...]
    o_ref[0] = (w * (1.0 - lambda_init)).T.astype(BF16)


def _mla_kernel(q_ref, k_ref, v_ref, *rest, tq, tk, seq, front):
    if front:
        _, o_ref, *scr = rest
    else:
        o_ref, *scr = rest
    m_ref, l_ref, acc_ref = scr[1:4]
    _causal_attend(_transpose(q_ref[0]), k_ref, v_ref, scr, tq=tq, tk=tk, seq=seq, front=front)
    o_ref[0] = (acc_ref[...] / l_ref[...]).T.astype(BF16)


def _causal_attention(kernel, q, k, v, extra, *, heads, dk, reps, tq, tk, seq, name):
    b, l, _ = q.shape
    dv = LANES
    outs = None
    for front in (False, True):
        t = BLOCK if front else tq
        if front:
            grid = (b, heads, 1)
            qmap = lambda bb, hh, i: (bb, seq // BLOCK, hh)
        else:
            grid = (b, heads, seq // t)
            qmap = lambda bb, hh, i: (bb, i, hh)
        if front:
            kvl, kvmap = BLOCK, qmap
        else:
            kvl, kvmap = l, lambda bb, hh, i: (bb, 0, hh)
        in_specs = [pl.BlockSpec((1, t, dk), qmap), pl.BlockSpec((1, kvl, dk), kvmap),
                    pl.BlockSpec((1, kvl, dv), kvmap)]
        in_specs += [_resident(e.shape) for e in extra]
        args = [q, k, v, *extra]
        aliases = {}
        if front:
            in_specs.append(pl.BlockSpec(memory_space=pl.ANY))
            aliases = {len(args): 0}
            args.append(outs)
        outs = pl.pallas_call(
            functools.partial(kernel, tq=t, tk=min(t, tk), seq=seq, front=front),
            grid=grid,
            in_specs=in_specs,
            out_specs=pl.BlockSpec((1, t, dv), qmap),
            out_shape=jax.ShapeDtypeStruct((b, l, heads * dv), BF16),
            scratch_shapes=[pltpu.VMEM((1 if front else l // BLOCK, dv, BLOCK), BF16),
                            pltpu.VMEM((1, reps * t), F32), pltpu.VMEM((1, reps * t), F32),
                            pltpu.VMEM((dv, reps * t), F32)]
                           + [pltpu.VMEM((8 if front else tk, reps * t), F32)] * 2
                           + [pltpu.VMEM((1, reps * t), F32)] * 2,
            input_output_aliases=aliases,
            compiler_params=_cparams(3),
            name=name + ("_front" if front else ""),
        )(*args)
    return outs


SWA_Q_COLS = SWA_HEADS * SWA_HEAD_DIM
SWA_KV_COLS = SWA_KV_HEADS * SWA_HEAD_DIM
SWA_IN_COLS = SWA_Q_COLS + 2 * SWA_KV_COLS


def _swa_proj_kernel(h_ref, g_ref, w_ref, b_ref, ca_ref, sa_ref, q_ref, k_ref, v_ref):
    xn = _rms(h_ref[0], g_ref[...], EPS).astype(BF16)
    ca, sa = ca_ref[...], sa_ref[...]
    half = PART_ROT // 2
    nq = SWA_Q_COLS
    nk = SWA_KV_COLS
    q = _rope(_dot(xn, w_ref[:, 0:nq]) + b_ref[:, 0:nq], ca, sa, half)
    q_ref[0] = (q * (SWA_HEAD_DIM ** -0.5 * LOG2E)).astype(BF16)
    k_ref[0] = _rope(_dot(xn, w_ref[:, nq:nq + nk]) + b_ref[:, nq:nq + nk], ca, sa, half).astype(BF16)
    v_ref[0] = (_dot(xn, w_ref[:, nq + nk:]) + b_ref[:, nq + nk:]).astype(BF16)


def _swa_proj(h3, g, w, bias, tabs):
    b, l, _ = h3.shape
    tm = PROJ_TM
    row = lambda wd: pl.BlockSpec((1, tm, wd), lambda j, bb: (bb, j, 0))
    tab = pl.BlockSpec((tm, LANES), lambda j, bb: (j, 0))
    outw = (SWA_Q_COLS, SWA_KV_COLS, SWA_KV_COLS)
    return pl.pallas_call(
        _swa_proj_kernel,
        grid=(l // tm, b),
        in_specs=[row(D_MODEL), _resident((1, D_MODEL)), _resident((D_MODEL, SWA_IN_COLS)),
                  _resident((1, SWA_IN_COLS)), tab, tab],
        out_specs=[row(w_) for w_ in outw],
        out_shape=[jax.ShapeDtypeStruct((b, l, w_), BF16) for w_ in outw],
        compiler_params=_cparams(2),
        name="swa_proj",
    )(h3, g, w, bias, *tabs)


def _swa_kernel(sinks_ref, q_ref, k_ref, v_ref, *rest, tq, seq, front):
    if front:
        _, o_ref, s_a, s_b = rest
    else:
        o_ref, s_a, s_b = rest
    hd_dim = SWA_HEAD_DIM
    f0 = 0 if front else seq
    meta_k = k_ref[0, f0 + N_FRONT:f0 + BLOCK, :]
    meta_v = v_ref[0, f0 + N_FRONT:f0 + BLOCK, :]
    if front:
        nk = N_META
        kc, vc = meta_k, meta_v
        key = lax.broadcasted_iota(jnp.int32, (nk, tq), 0) + N_FRONT
        qry = lax.broadcasted_iota(jnp.int32, (nk, tq), 1)
        allowed = key <= qry
    else:
        i = pl.program_id(1)
        bw = tq + WINDOW
        nk = bw + N_META
        bs = pl.multiple_of(jnp.maximum(i * tq - WINDOW, 0), BLOCK)
        kc = jnp.concatenate([k_ref[0, pl.ds(bs, bw), :], meta_k], axis=0)
        vc = jnp.concatenate([v_ref[0, pl.ds(bs, bw), :], meta_v], axis=0)
        key = lax.broadcasted_iota(jnp.int32, (nk, tq), 0)
        qry = lax.broadcasted_iota(jnp.int32, (nk, tq), 1)
        dist = (i * tq + qry) - (bs + key)
        allowed = (key >= bw) | ((dist >= 0) & (dist < WINDOW))
    bias = jnp.where(allowed, 0.0, NEG_INF)
    nkp = -(-nk // LANES) * LANES
    vt = _transpose(jnp.concatenate([vc, jnp.zeros((nkp - nk, vc.shape[1]), vc.dtype)], axis=0))
    qt = _transpose(q_ref[0])
    zh = jnp.zeros((hd_dim, tq), BF16)
    sbuf = (s_a, s_b)

    def produce(g, rr):
        hd = g * SWA_GROUP + rr
        qh = qt[hd * hd_dim:(hd + 1) * hd_dim, :]
        rhs = jnp.concatenate([qh, zh] if g == 0 else [zh, qh], axis=0)
        sbuf[g][:, rr * tq:(rr + 1) * tq] = _dot(kc, rhs)

    outs = [None] * SWA_HEADS

    def consume(g, rr):
        hd = g * SWA_GROUP + rr
        sink = sinks_ref[hd] * LOG2E
        s = sbuf[g][:, rr * tq:(rr + 1) * tq] + bias
        m = jnp.maximum(jnp.max(s, axis=0, keepdims=True), sink)
        e = jnp.exp2(s - m)
        den = jnp.sum(e, axis=0, keepdims=True) + jnp.exp2(sink - m)
        eb = jnp.concatenate([e.astype(BF16), jnp.zeros((nkp - nk, tq), BF16)], axis=0)
        o = _dot(vt, eb)
        outs[hd] = o[g * hd_dim:(g + 1) * hd_dim] / den

    for rr in range(SWA_GROUP):
        produce(0, rr)
    for rr in range(SWA_GROUP):
        produce(1, rr)
        consume(0, rr)
    for rr in range(SWA_GROUP):
        consume(1, rr)
    o_ref[0] = jnp.concatenate(outs, axis=0).T.astype(BF16)


def _swa_attention(q, k, v, sinks, *, seq):
    b, l, _ = q.shape
    nq = SWA_Q_COLS
    outs = None
    for front in (False, True):
        tq = BLOCK if front else SWA_TQ
        nk = N_META if front else tq + WINDOW + N_META
        if front:
            grid = (b, 1)
            qmap = lambda bb, i: (bb, seq // BLOCK, 0)
            kvl, kvmap = BLOCK, qmap
        else:
            grid = (b, seq // tq)
            qmap = lambda bb, i: (bb, i, 0)
            kvl, kvmap = l, lambda bb, i: (bb, 0, 0)
        in_specs = [pl.BlockSpec(memory_space=pltpu.SMEM), pl.BlockSpec((1, tq, nq), qmap),
                    pl.BlockSpec((1, kvl, k.shape[2]), kvmap), pl.BlockSpec((1, kvl, v.shape[2]), kvmap)]
        args = [sinks, q, k, v]
        aliases = {}
        if front:
            in_specs.append(pl.BlockSpec(memory_space=pl.ANY))
            aliases = {len(args): 0}
            args.append(outs)
        outs = pl.pallas_call(
            functools.partial(_swa_kernel, tq=tq, seq=seq, front=front),
            grid=grid,
            in_specs=in_specs,
            out_specs=pl.BlockSpec((1, tq, nq), qmap),
            out_shape=jax.ShapeDtypeStruct((b, l, nq), BF16),
            scratch_shapes=[pltpu.VMEM((nk, SWA_GROUP * tq), F32)] * 2,
            input_output_aliases=aliases,
            compiler_params=_cparams(2),
            name="swa_attn" + ("_front" if front else ""),
        )(*args)
    return outs


def _ab_weights(w_in, w_uq, w_ukv):
    win = jnp.pad(w_in, ((0, 0), (0, AB_IN_PAD - w_in.shape[1]))).astype(BF16)
    wq = w_uq.reshape(MLA_Q_RANK, MLA_HEADS, MLA_NOPE + MLA_ROPE)
    wq = jnp.pad(wq, ((0, 0), (0, 0), (0, 2 * LANES - MLA_NOPE - MLA_ROPE))).reshape(MLA_Q_RANK, MLA_HEADS * 2 * LANES)
    wkv = w_ukv.reshape(MLA_KV_RANK, MLA_HEADS, 2, MLA_NOPE).transpose(0, 2, 1, 3).reshape(MLA_KV_RANK, -1)
    return win, wq.astype(BF16), wkv.astype(BF16)


def kernel(x, meta_tokens, ffn1_norm, ffn1_w_gate, ffn1_w_up, ffn1_w_down, mix_norm, ab_w_in, diff_lambda_q1, diff_lambda_k1, diff_lambda_q2, diff_lambda_k2, diff_subln, mla_q_norm, mla_w_uq, mla_kv_norm, mla_w_ukv, ab_w_out, swa_w_qkv, swa_b_qkv, swa_sinks, swa_w_out, swa_b_out, ffn2_norm, ffn2_w_gate, ffn2_w_up, ffn2_w_down, final_norm):
    b, seq, d = x.shape
    depth = ffn1_norm.shape[0]
    l = seq + BLOCK
    n = b * l
    front = jnp.concatenate([jnp.zeros((N_FRONT, d), x.dtype), meta_tokens.astype(x.dtype)], axis=0)

    pos = jnp.concatenate([jnp.arange(seq) + N_META, jnp.maximum(jnp.arange(BLOCK) - N_FRONT, 0)])
    tabs_p = _rope_tables(pos, PART_ROT)
    tabs_m = _rope_tables(pos, MLA_ROPE)
    row2 = lambda a: a.reshape(1, -1)

    out = None
    for ly in range(depth):
        w1 = (row2(ffn1_norm[ly]), ffn1_w_gate[ly].astype(BF16), ffn1_w_up[ly].astype(BF16),
              ffn1_w_down[ly].astype(BF16))
        h = _ffn_first(x, front, *w1).reshape(n, d) if ly == 0 else _ffn(h, *w1)
        h3 = h.reshape(b, l, d)
        if ly % 2 == 0:
            e = ly // 2
            lambda_init = 0.8 - 0.6 * math.exp(-0.3 * ly)
            win, wuq, wukv = _ab_weights(ab_w_in[e], mla_w_uq[e], mla_w_ukv[e])
            qa, ka, va, qb, kb, vb = _ab_proj(h3, row2(mix_norm[ly]), win, row2(mla_q_norm[e]), wuq,
                                              row2(mla_kv_norm[e]), wukv, tabs_p + tabs_m)
            extra = [row2(diff_lambda_q1[e]), row2(diff_lambda_k1[e]), row2(diff_lambda_q2[e]),
                     row2(diff_lambda_k2[e]), diff_subln[e].reshape(-1, 1)]
            oa = _causal_attention(functools.partial(_diff_kernel, lambda_init=lambda_init), qa, ka, va, extra,
                                   heads=DIFF_HEADS, dk=LANES, reps=2, tq=ATT_TQ, tk=ATT_TK, seq=seq,
                                   name="diff_attn")
            ob = _causal_attention(_mla_kernel, qb, kb, vb, [], heads=MLA_HEADS, dk=2 * LANES, reps=1,
                                   tq=ATT_TQ, tk=ATT_TK, seq=seq, name="mla_attn")
            wo = ab_w_out[e].astype(BF16)
            mix, bias = [(oa, wo[:512]), (ob, wo[512:])], None
        else:
            o = ly // 2
            q, k, v = _swa_proj(h3, row2(mix_norm[ly]), swa_w_qkv[o].astype(BF16), row2(swa_b_qkv[o]), tabs_p)
            att = _swa_attention(q, k, v, swa_sinks[o], seq=seq)
            mix, bias = [(att, swa_w_out[o].astype(BF16))], row2(swa_b_out[o])
        w2 = (row2(ffn2_norm[ly]), ffn2_w_gate[ly].astype(BF16), ffn2_w_up[ly].astype(BF16),
              ffn2_w_down[ly].astype(BF16))
        if ly == depth - 1:
            out = _ffn_final(h3, *w2, row2(final_norm), seq, mix=mix, bias=bias)
        else:
            h = _ffn(h, *w2, mix=[(a.reshape(n, -1), w) for a, w in mix], bias=bias)
    return out
```

```python
import functools
import math

import jax
import jax.numpy as jnp
from jax import lax
from jax.experimental import pallas as pl
from jax.experimental.pallas import tpu as pltpu

F32 = jnp.float32
BF16 = jnp.bfloat16

D_MODEL = 1024
N_META = 16
BLOCK = 128
N_FRONT = BLOCK - N_META
ROPE_THETA = 500000.0
EPS = 1e-6
NEG_INF = -1e30
D_FF = 2816

DIFF_HEADS = 4
DIFF_HEAD_DIM = 64
DIFF_V_DIM = 2 * DIFF_HEAD_DIM
DIFF_EPS = 1e-5
MLA_HEADS = 4
MLA_NOPE = 128
MLA_ROPE = 64
MLA_V = 128
MLA_Q_RANK = 256
MLA_KV_RANK = 256
SWA_HEADS = 16
SWA_KV_HEADS = 2
SWA_GROUP = SWA_HEADS // SWA_KV_HEADS
SWA_HEAD_DIM = 64
WINDOW = 128

PART_ROT = DIFF_HEAD_DIM // 4
LANES = 128
VMEM_LIMIT = 56 * 1024 * 1024

FFN_TM = 640
FFN_FC = 256
PROJ_TM = 640
FFN_FINAL_TM = 512
ATT_TK = 512
ATT_TQ = 2 * ATT_TK
ATT_CW = 256
VT_FILL_UNROLL = 5
LOG2E = 1.4426950408889634
SWA_TQ = 256


def _cparams(n_axes):
    return pltpu.CompilerParams(dimension_semantics=("arbitrary",) * n_axes,
                                vmem_limit_bytes=VMEM_LIMIT)


def _resident(shape):
    nd = len(shape)
    return pl.BlockSpec(shape, lambda *_: (0,) * nd, pipeline_mode=pl.Buffered(1))


def _rms(x, g, eps):
    return x * lax.rsqrt(jnp.mean(x * x, axis=-1, keepdims=True) + eps) * g


def _dot(a, b):
    return jnp.dot(a, b, preferred_element_type=F32)


def _dot_nt(a, b):
    return lax.dot_general(a, b, (((1,), (1,)), ((), ())), preferred_element_type=F32)


def _ffn_kernel(x_ref, *refs, final=False, aliased=False, n_mix=0, mix_bias=False):
    mix, refs = refs[:2 * n_mix], refs[2 * n_mix:]
    if mix_bias:
        mb_ref, refs = refs[0], refs[1:]
    g_ref, wg_ref, wu_ref, wd_ref, *rest = refs
    if final:
        fg_ref, o_ref, act_ref = rest
    elif aliased:
        _, o_ref, act_ref = rest
    else:
        o_ref, act_ref = rest
    x = x_ref[...].reshape(x_ref.shape[-2:])
    for a_ref, w_ref in zip(mix[0::2], mix[1::2]):
        x = x + _dot(a_ref[...].reshape(a_ref.shape[-2:]), w_ref[...])
    if mix_bias:
        x = x + mb_ref[...]
    xn = _rms(x, g_ref[...], EPS).astype(BF16)
    for c in range(D_FF // FFN_FC):
        sl = slice(c * FFN_FC, (c + 1) * FFN_FC)
        g = _dot(xn, wg_ref[:, sl])
        u = _dot(xn, wu_ref[:, sl])
        act_ref[:, sl] = (g * (1.0 / (1.0 + jnp.exp(-g))) * u).astype(BF16)
    y = x + 0.5 * _dot(act_ref[...], wd_ref[...])
    if final:
        y = _rms(y, fg_ref[...], EPS)
    o_ref[...] = y.reshape(o_ref.shape)


def _mix_args(mix, bias, row):
    args, specs = [], []
    for a, w in mix:
        args += [a, w]
        specs += [row(a.shape[-1]), _resident(w.shape)]
    if bias is not None:
        args.append(bias)
        specs.append(_resident(bias.shape))
    return args, specs, dict(n_mix=len(mix), mix_bias=bias is not None)


def _ffn_wspecs(ly):
    mat = lambda r, c: pl.BlockSpec((None, r, c), lambda *_: (ly, 0, 0), pipeline_mode=pl.Buffered(1))
    return [_resident((1, D_MODEL)), mat(D_MODEL, D_FF), mat(D_MODEL, D_FF), mat(D_FF, D_MODEL)]


def _ffn(h, g, wg, wu, wd, ly, mix=(), bias=None):
    n = h.shape[0]
    tm = FFN_TM
    assert n % tm == 0
    row = lambda w: pl.BlockSpec((tm, w), lambda i: (i, 0))
    margs, mspecs, mkw = _mix_args(mix, bias, row)
    return pl.pallas_call(
        functools.partial(_ffn_kernel, **mkw),
        grid=(n // tm,),
        in_specs=[row(D_MODEL)] + mspecs + _ffn_wspecs(ly),
        out_specs=row(D_MODEL),
        out_shape=jax.ShapeDtypeStruct((n, D_MODEL), F32),
        scratch_shapes=[pltpu.VMEM((tm, D_FF), BF16)],
        compiler_params=_cparams(1),
        name="ffn",
    )(h, *margs, g, wg, wu, wd)


def _ffn_first(x, front, g, wg, wu, wd):
    b, seq, d = x.shape
    tm = FFN_FINAL_TM
    assert seq % tm == 0
    weights = _ffn_wspecs(0)
    row = pl.BlockSpec((1, tm, d), lambda bb, i: (bb, i, 0))
    h3 = pl.pallas_call(
        _ffn_kernel,
        grid=(b, seq // tm),
        in_specs=[row] + weights,
        out_specs=row,
        out_shape=jax.ShapeDtypeStruct((b, seq + BLOCK, d), F32),
        scratch_shapes=[pltpu.VMEM((tm, D_FF), BF16)],
        compiler_params=_cparams(2),
        name="ffn_first",
    )(x, g, wg, wu, wd)
    return pl.pallas_call(
        functools.partial(_ffn_kernel, aliased=True),
        grid=(b,),
        in_specs=[pl.BlockSpec((BLOCK, d), lambda bb: (0, 0))] + weights + [pl.BlockSpec(memory_space=pl.ANY)],
        out_specs=pl.BlockSpec((1, BLOCK, d), lambda bb: (bb, seq // BLOCK, 0)),
        out_shape=jax.ShapeDtypeStruct(h3.shape, F32),
        scratch_shapes=[pltpu.VMEM((BLOCK, D_FF), BF16)],
        input_output_aliases={5: 0},
        compiler_params=_cparams(1),
        name="ffn_first_front",
    )(front, g, wg, wu, wd, h3)


def _ffn_final(h3, g, wg, wu, wd, ly, fg, seq, mix=(), bias=None):
    b = h3.shape[0]
    tm = FFN_FINAL_TM
    assert seq % tm == 0
    row = lambda w: pl.BlockSpec((1, tm, w), lambda bb, i: (bb, i, 0))
    margs, mspecs, mkw = _mix_args(mix, bias, row)
    return pl.pallas_call(
        functools.partial(_ffn_kernel, final=True, **mkw),
        grid=(b, seq // tm),
        in_specs=[row(D_MODEL)] + mspecs + _ffn_wspecs(ly) + [_resident((1, D_MODEL))],
        out_specs=row(D_MODEL),
        out_shape=jax.ShapeDtypeStruct((b, seq, D_MODEL), F32),
        scratch_shapes=[pltpu.VMEM((tm, D_FF), BF16)],
        compiler_params=_cparams(2),
        name="ffn_final",
    )(h3, *margs, g, wg, wu, wd, fg)


def _rope(y, c, s, half):
    w = y.shape[1]
    reps = w // LANES
    if reps > 1:
        c = jnp.concatenate([c] * reps, axis=1)
        s = jnp.concatenate([s] * reps, axis=1)
    lane = lax.broadcasted_iota(jnp.int32, y.shape, 1)
    first = (lane & 63) < half
    partner = jnp.where(first, pltpu.roll(y, w - half, 1), pltpu.roll(y, half, 1))
    return y * c + partner * s


def _rope_tables(pos, rot_dim):
    half = rot_dim // 2
    g = jnp.arange(LANES) & 63
    inv = ROPE_THETA ** (-(2 * (g % half)).astype(F32) / rot_dim)
    ang = pos.astype(F32)[:, None] * inv[None, :]
    rotated = (g < rot_dim)[None, :]
    cos = jnp.where(rotated, jnp.cos(ang), 1.0)
    sin = jnp.where(rotated, jnp.where((g < half)[None, :], -jnp.sin(ang), jnp.sin(ang)), 0.0)
    return cos, sin


AB_IN_PAD = 3 * 512 + MLA_Q_RANK + MLA_KV_RANK + LANES


def _ab_proj_kernel(h_ref, g_ref, win_ref, qn_ref, wuq_ref, kvn_ref, wukv_ref, ca_ref, sa_ref, cm_ref, sm_ref,
                    qa_ref, ka_ref, va_ref, qb_ref, kb_ref, vb_ref):
    xn = _rms(h_ref[0], g_ref[...], EPS).astype(BF16)
    ca, sa, cm, sm = ca_ref[...], sa_ref[...], cm_ref[...], sm_ref[...]
    half_a = PART_ROT // 2
    half_m = MLA_ROPE // 2
    qa = _rope(_dot(xn, win_ref[:, 0:512]), ca, sa, half_a)
    qa_ref[0] = (qa * (DIFF_HEAD_DIM ** -0.5 * LOG2E)).astype(BF16)
    ka_ref[0] = _rope(_dot(xn, win_ref[:, 512:1024]), ca, sa, half_a).astype(BF16)
    va_ref[0] = _dot(xn, win_ref[:, 1024:1536]).astype(BF16)
    cq = _dot(xn, win_ref[:, 1536:1792])
    ckv = _dot(xn, win_ref[:, 1792:2048])
    kr = _rope(_dot(xn, win_ref[:, 2048:2176]), cm, sm, half_m).astype(BF16)
    qb = _dot(_rms(cq, qn_ref[...], EPS).astype(BF16), wuq_ref[...])
    scale = (MLA_NOPE + MLA_ROPE) ** -0.5 * LOG2E
    for hh in range(MLA_HEADS):
        o = 2 * LANES * hh
        qb_ref[0, :, o:o + LANES] = (qb[:, o:o + LANES] * scale).astype(BF16)
        qb_ref[0, :, o + LANES:o + 2 * LANES] = (_rope(qb[:, o + LANES:o + 2 * LANES], cm, sm, half_m) * scale).astype(BF16)
    kv = _dot(_rms(ckv, kvn_ref[...], EPS).astype(BF16), wukv_ref[...])
    for hh in range(MLA_HEADS):
        o = 2 * LANES * hh
        kb_ref[0, :, o:o + LANES] = kv[:, LANES * hh:LANES * (hh + 1)].astype(BF16)
        kb_ref[0, :, o + LANES:o + 2 * LANES] = kr
    vb_ref[0] = kv[:, MLA_HEADS * MLA_NOPE:].astype(BF16)


def _ab_proj(h3, g, win, qn, wuq, kvn, wukv, tabs):
    b, l, _ = h3.shape
    tm = PROJ_TM
    assert l % tm == 0
    row = lambda w: pl.BlockSpec((1, tm, w), lambda j, bb: (bb, j, 0))
    tab = pl.BlockSpec((tm, LANES), lambda j, bb: (j, 0))
    outw = (512, 512, 512, 1024, 1024, 512)
    return pl.pallas_call(
        _ab_proj_kernel,
        grid=(l // tm, b),
        in_specs=[row(D_MODEL), _resident((1, D_MODEL)), _resident((D_MODEL, AB_IN_PAD)),
                  _resident((1, MLA_Q_RANK)), _resident((MLA_Q_RANK, 1024)),
                  _resident((1, MLA_KV_RANK)), _resident((MLA_KV_RANK, 1024)), tab, tab, tab, tab],
        out_specs=[row(w) for w in outw],
        out_shape=[jax.ShapeDtypeStruct((b, l, w), BF16) for w in outw],
        compiler_params=_cparams(2),
        name="ab_proj",
    )(h3, g, win, qn, wuq, kvn, wukv, *tabs)


def _transpose(x):
    return x.T


def _flash_step(k, vt, qt, m_ref, l_ref, acc_ref, mask, between=None):
    cw = min(ATT_CW, qt.shape[1])
    probs = {}
    for c0 in range(0, qt.shape[1], cw):
        cs = slice(c0, c0 + cw)
        s = _dot(k, qt[:, cs])
        s = jnp.where(mask(c0, s.shape), s, NEG_INF)
        m_new = jnp.max(s, axis=0, keepdims=True)
        p = jnp.exp2(s - m_new)
        l_ref[:, cs] = jnp.sum(p, axis=0, keepdims=True)
        m_ref[:, cs] = m_new
        probs[c0] = p.astype(BF16)
    if between is not None:
        between()
    for c0, p in probs.items():
        acc_ref[:, c0:c0 + cw] = _dot(vt, p)


def _produce(k, qt, s_ref, x_ref, chunks):
    for c0 in chunks:
        cs = slice(c0, c0 + ATT_CW)
        s = _dot(k, qt[:, cs])
        s_ref[:, cs] = s
        x_ref[:, cs] = jnp.max(s, axis=0, keepdims=True)


def _consume(s_ref, x_ref, vt, m_ref, l_ref, acc_ref, chunks, diag=None):
    tk = s_ref.shape[0]
    for c0 in chunks:
        cs = slice(c0, c0 + ATT_CW)
        if diag is None:
            seen = tk
        else:
            key0, tq, tri = diag
            seen = min((c0 & (tq - 1)) - key0, tk)
            if seen + ATT_CW <= 0:
                continue
        if seen == tk:
            s = s_ref[:, cs]
            smax = x_ref[:, cs]
        else:
            s = s_ref[seen:seen + ATT_CW, cs] + tri
            if seen > 0:
                s = jnp.concatenate([s_ref[0:seen, cs], s], axis=0)
            smax = jnp.max(s, axis=0, keepdims=True)
        m_prev = m_ref[:, cs]
        m_new = jnp.maximum(m_prev, smax)
        alpha = jnp.exp2(m_prev - m_new)
        p = jnp.exp2(s - m_new)
        l_ref[:, cs] = alpha * l_ref[:, cs] + jnp.sum(p, axis=0, keepdims=True)
        acc_ref[:, cs] = alpha * acc_ref[:, cs] + _dot(vt[:, :s.shape[0]], p.astype(BF16))
        m_ref[:, cs] = m_new


def _causal_attend(qt, k_ref, v_ref, scr, *, tq, tk, seq, front):
    vt_ref, m_ref, l_ref, acc_ref, s0, s1, x0, x1 = scr
    state = (m_ref, l_ref, acc_ref)
    f0 = 0 if front else seq
    kf = k_ref[0, f0:f0 + BLOCK, :]

    def valid(c0, shape):
        return lax.broadcasted_iota(jnp.int32, shape, 0) >= N_FRONT

    def causal(key0):
        def mask(c0, shape):
            key = lax.broadcasted_iota(jnp.int32, shape, 0) + key0
            qry = (lax.broadcasted_iota(jnp.int32, shape, 1) + c0) & (tq - 1)
            return key <= qry
        return mask

    if front:
        both = lambda c0, shape: valid(c0, shape) & causal(0)(c0, shape)
        _flash_step(kf, _transpose(v_ref[0, f0:f0 + BLOCK, :]), qt, *state, both)
        return
    assert tq == 2 * tk
    i = pl.program_id(2)
    nblk = v_ref.shape[1] // BLOCK
    sub = tk // BLOCK
    chunks = list(range(0, qt.shape[1], ATT_CW))
    late = [c0 for c0 in chunks if (c0 & (tq - 1)) + ATT_CW > tk]

    @pl.when(i == 0)
    def _fill():
        def fill(c, carry):
            st = pl.multiple_of(c * BLOCK, BLOCK)
            vt_ref[c] = _transpose(v_ref[0, pl.ds(st, BLOCK), :])
            return carry
        lax.fori_loop(0, nblk, fill, 0, unroll=VT_FILL_UNROLL)

    def k_block(j):
        return k_ref[0, pl.ds(pl.multiple_of(j * tk, tk), tk), :]

    def vt_block(j):
        return jnp.concatenate([vt_ref[j * sub + u] for u in range(sub)], axis=1)

    _flash_step(kf, vt_ref[seq // BLOCK], qt, *state, valid,
                between=lambda: _produce(k_block(0), qt, s0, x0, chunks))

    def body(p, carry):
        _produce(k_block(2 * p + 1), qt, s1, x1, chunks)
        _consume(s0, x0, vt_block(2 * p), *state, chunks)
        _produce(k_block(2 * p + 2), qt, s0, x0, chunks)
        _consume(s1, x1, vt_block(2 * p + 1), *state, chunks)
        return carry

    lax.fori_loop(0, i, body, 0)
    tri_shape = (ATT_CW, ATT_CW)
    tri = jnp.where(lax.broadcasted_iota(jnp.int32, tri_shape, 0) <= lax.broadcasted_iota(jnp.int32, tri_shape, 1),
                    0.0, NEG_INF)
    _produce(k_block(2 * i + 1), qt, s1, x1, late)
    _consume(s0, x0, vt_block(2 * i), *state, chunks, diag=(0, tq, tri))
    _consume(s1, x1, vt_block(2 * i + 1), *state, chunks, diag=(tk, tq, tri))


def _diff_kernel(q_ref, k_ref, v_ref, lq1_ref, lk1_ref, lq2_ref, lk2_ref, sub_ref, *rest,
                 tq, tk, seq, front, lambda_init):
    if front:
        _, o_ref, *scr = rest
    else:
        o_ref, *scr = rest
    m_ref, l_ref, acc_ref = scr[1:4]
    qt = _transpose(q_ref[0])
    d = lax.broadcasted_iota(jnp.int32, qt.shape, 0)
    zero = jnp.zeros_like(qt)
    qs = jnp.concatenate([jnp.where(d < DIFF_HEAD_DIM, qt, zero), jnp.where(d >= DIFF_HEAD_DIM, qt, zero)], axis=1)
    _causal_attend(qs, k_ref, v_ref, scr, tq=tq, tk=tk, seq=seq, front=front)
    o = acc_ref[...] / l_ref[...]
    lam = (jnp.exp(jnp.sum(lq1_ref[...] * lk1_ref[...], keepdims=True))
           - jnp.exp(jnp.sum(lq2_ref[...] * lk2_ref[...], keepdims=True)) + lambda_init)
    w = o[:, :tq] - lam * o[:, tq:]
    w = w * lax.rsqrt(jnp.mean(w * w, axis=0, keepdims=True) + DIFF_EPS) * sub_ref[...]
    o_ref[0] = (w * (1.0 - lambda_init)).T.astype(BF16)


def _mla_kernel(q_ref, k_ref, v_ref, *rest, tq, tk, seq, front):
    if front:
        _, o_ref, *scr = rest
    else:
        o_ref, *scr = rest
    m_ref, l_ref, acc_ref = scr[1:4]
    _causal_attend(_transpose(q_ref[0]), k_ref, v_ref, scr, tq=tq, tk=tk, seq=seq, front=front)
    o_ref[0] = (acc_ref[...] / l_ref[...]).T.astype(BF16)


def _causal_attention(kernel, q, k, v, extra, *, heads, dk, reps, tq, tk, seq, name):
    b, l, _ = q.shape
    dv = LANES
    outs = None
    for front in (False, True):
        t = BLOCK if front else tq
        if front:
            grid = (b, heads, 1)
            qmap = lambda bb, hh, i: (bb, seq // BLOCK, hh)
        else:
            grid = (b, heads, seq // t)
            qmap = lambda bb, hh, i: (bb, i, hh)
        if front:
            kvl, kvmap = BLOCK, qmap
        else:
            kvl, kvmap = l, lambda bb, hh, i: (bb, 0, hh)
        in_specs = [pl.BlockSpec((1, t, dk), qmap), pl.BlockSpec((1, kvl, dk), kvmap),
                    pl.BlockSpec((1, kvl, dv), kvmap)]
        in_specs += [_resident(e.shape) for e in extra]
        args = [q, k, v, *extra]
        aliases = {}
        if front:
            in_specs.append(pl.BlockSpec(memory_space=pl.ANY))
            aliases = {len(args): 0}
            args.append(outs)
        outs = pl.pallas_call(
            functools.partial(kernel, tq=t, tk=min(t, tk), seq=seq, front=front),
            grid=grid,
            in_specs=in_specs,
            out_specs=pl.BlockSpec((1, t, dv), qmap),
            out_shape=jax.ShapeDtypeStruct((b, l, heads * dv), BF16),
            scratch_shapes=[pltpu.VMEM((1 if front else l // BLOCK, dv, BLOCK), BF16),
                            pltpu.VMEM((1, reps * t), F32), pltpu.VMEM((1, reps * t), F32),
                            pltpu.VMEM((dv, reps * t), F32)]
                           + [pltpu.VMEM((8 if front else tk, reps * t), F32)] * 2
                           + [pltpu.VMEM((1, reps * t), F32)] * 2,
            input_output_aliases=aliases,
            compiler_params=_cparams(3),
            name=name + ("_front" if front else ""),
        )(*args)
    return outs


SWA_Q_COLS = SWA_HEADS * SWA_HEAD_DIM
SWA_KV_COLS = SWA_KV_HEADS * SWA_HEAD_DIM
SWA_IN_COLS = SWA_Q_COLS + 2 * SWA_KV_COLS


def _swa_proj_kernel(h_ref, g_ref, w_ref, b_ref, ca_ref, sa_ref, q_ref, k_ref, v_ref):
    xn = _rms(h_ref[0], g_ref[...], EPS).astype(BF16)
    ca, sa = ca_ref[...], sa_ref[...]
    half = PART_ROT // 2
    nq = SWA_Q_COLS
    nk = SWA_KV_COLS
    q = _rope(_dot(xn, w_ref[:, 0:nq]) + b_ref[:, 0:nq], ca, sa, half)
    q_ref[0] = (q * (SWA_HEAD_DIM ** -0.5 * LOG2E)).astype(BF16)
    k_ref[0] = _rope(_dot(xn, w_ref[:, nq:nq + nk]) + b_ref[:, nq:nq + nk], ca, sa, half).astype(BF16)
    v_ref[0] = (_dot(xn, w_ref[:, nq + nk:]) + b_ref[:, nq + nk:]).astype(BF16)


def _swa_proj(h3, g, w, bias, tabs):
    b, l, _ = h3.shape
    tm = PROJ_TM
    row = lambda wd: pl.BlockSpec((1, tm, wd), lambda j, bb: (bb, j, 0))
    tab = pl.BlockSpec((tm, LANES), lambda j, bb: (j, 0))
    outw = (SWA_Q_COLS, SWA_KV_COLS, SWA_KV_COLS)
    return pl.pallas_call(
        _swa_proj_kernel,
        grid=(l // tm, b),
        in_specs=[row(D_MODEL), _resident((1, D_MODEL)), _resident((D_MODEL, SWA_IN_COLS)),
                  _resident((1, SWA_IN_COLS)), tab, tab],
        out_specs=[row(w_) for w_ in outw],
        out_shape=[jax.ShapeDtypeStruct((b, l, w_), BF16) for w_ in outw],
        compiler_params=_cparams(2),
        name="swa_proj",
    )(h3, g, w, bias, *tabs)


def _swa_kernel(sinks_ref, q_ref, k_ref, v_ref, *rest, tq, seq, front):
    if front:
        _, o_ref, s_a, s_b = rest
    else:
        o_ref, s_a, s_b = rest
    hd_dim = SWA_HEAD_DIM
    f0 = 0 if front else seq
    meta_k = k_ref[0, f0 + N_FRONT:f0 + BLOCK, :]
    meta_v = v_ref[0, f0 + N_FRONT:f0 + BLOCK, :]
    if front:
        nk = N_META
        kc, vc = meta_k, meta_v
        key = lax.broadcasted_iota(jnp.int32, (nk, tq), 0) + N_FRONT
        qry = lax.broadcasted_iota(jnp.int32, (nk, tq), 1)
        allowed = key <= qry
    else:
        i = pl.program_id(1)
        bw = tq + WINDOW
        nk = bw + N_META
        bs = pl.multiple_of(jnp.maximum(i * tq - WINDOW, 0), BLOCK)
        kc = jnp.concatenate([k_ref[0, pl.ds(bs, bw), :], meta_k], axis=0)
        vc = jnp.concatenate([v_ref[0, pl.ds(bs, bw), :], meta_v], axis=0)
        key = lax.broadcasted_iota(jnp.int32, (nk, tq), 0)
        qry = lax.broadcasted_iota(jnp.int32, (nk, tq), 1)
        dist = (i * tq + qry) - (bs + key)
        allowed = (key >= bw) | ((dist >= 0) & (dist < WINDOW))
    bias = jnp.where(allowed, 0.0, NEG_INF)
    nkp = -(-nk // LANES) * LANES
    vt = _transpose(jnp.concatenate([vc, jnp.zeros((nkp - nk, vc.shape[1]), vc.dtype)], axis=0))
    qt = _transpose(q_ref[0])
    zh = jnp.zeros((hd_dim, tq), BF16)
    sbuf = (s_a, s_b)

    def produce(g, rr):
        hd = g * SWA_GROUP + rr
        qh = qt[hd * hd_dim:(hd + 1) * hd_dim, :]
        rhs = jnp.concatenate([qh, zh] if g == 0 else [zh, qh], axis=0)
        sbuf[g][:, rr * tq:(rr + 1) * tq] = _dot(kc, rhs)

    outs = [None] * SWA_HEADS

    def consume(g, rr):
        hd = g * SWA_GROUP + rr
        sink = sinks_ref[hd] * LOG2E
        s = sbuf[g][:, rr * tq:(rr + 1) * tq] + bias
        m = jnp.maximum(jnp.max(s, axis=0, keepdims=True), sink)
        e = jnp.exp2(s - m)
        den = jnp.sum(e, axis=0, keepdims=True) + jnp.exp2(sink - m)
        eb = jnp.concatenate([e.astype(BF16), jnp.zeros((nkp - nk, tq), BF16)], axis=0)
        o = _dot(vt, eb)
        outs[hd] = o[g * hd_dim:(g + 1) * hd_dim] / den

    for rr in range(SWA_GROUP):
        produce(0, rr)
    for rr in range(SWA_GROUP):
        produce(1, rr)
        consume(0, rr)
    for rr in range(SWA_GROUP):
        consume(1, rr)
    o_ref[0] = jnp.concatenate(outs, axis=0).T.astype(BF16)


def _swa_attention(q, k, v, sinks, *, seq):
    b, l, _ = q.shape
    nq = SWA_Q_COLS
    outs = None
    for front in (False, True):
        tq = BLOCK if front else SWA_TQ
        nk = N_META if front else tq + WINDOW + N_META
        if front:
            grid = (b, 1)
            qmap = lambda bb, i: (bb, seq // BLOCK, 0)
            kvl, kvmap = BLOCK, qmap
        else:
            grid = (b, seq // tq)
            qmap = lambda bb, i: (bb, i, 0)
            kvl, kvmap = l, lambda bb, i: (bb, 0, 0)
        in_specs = [pl.BlockSpec(memory_space=pltpu.SMEM), pl.BlockSpec((1, tq, nq), qmap),
                    pl.BlockSpec((1, kvl, k.shape[2]), kvmap), pl.BlockSpec((1, kvl, v.shape[2]), kvmap)]
        args = [sinks, q, k, v]
        aliases = {}
        if front:
            in_specs.append(pl.BlockSpec(memory_space=pl.ANY))
            aliases = {len(args): 0}
            args.append(outs)
        outs = pl.pallas_call(
            functools.partial(_swa_kernel, tq=tq, seq=seq, front=front),
            grid=grid,
            in_specs=in_specs,
            out_specs=pl.BlockSpec((1, tq, nq), qmap),
            out_shape=jax.ShapeDtypeStruct((b, l, nq), BF16),
            scratch_shapes=[pltpu.VMEM((nk, SWA_GROUP * tq), F32)] * 2,
            input_output_aliases=aliases,
            compiler_params=_cparams(2),
            name="swa_attn" + ("_front" if front else ""),
        )(*args)
    return outs


def _ab_weights(w_in, w_uq, w_ukv):
    win = jnp.pad(w_in, ((0, 0), (0, AB_IN_PAD - w_in.shape[1]))).astype(BF16)
    wq = w_uq.reshape(MLA_Q_RANK, MLA_HEADS, MLA_NOPE + MLA_ROPE)
    wq = jnp.pad(wq, ((0, 0), (0, 0), (0, 2 * LANES - MLA_NOPE - MLA_ROPE))).reshape(MLA_Q_RANK, MLA_HEADS * 2 * LANES)
    wkv = w_ukv.reshape(MLA_KV_RANK, MLA_HEADS, 2, MLA_NOPE).transpose(0, 2, 1, 3).reshape(MLA_KV_RANK, -1)
    return win, wq.astype(BF16), wkv.astype(BF16)


def kernel(x, meta_tokens, ffn1_norm, ffn1_w_gate, ffn1_w_up, ffn1_w_down, mix_norm, ab_w_in, diff_lambda_q1, diff_lambda_k1, diff_lambda_q2, diff_lambda_k2, diff_subln, mla_q_norm, mla_w_uq, mla_kv_norm, mla_w_ukv, ab_w_out, swa_w_qkv, swa_b_qkv, swa_sinks, swa_w_out, swa_b_out, ffn2_norm, ffn2_w_gate, ffn2_w_up, ffn2_w_down, final_norm):
    b, seq, d = x.shape
    depth = ffn1_norm.shape[0]
    l = seq + BLOCK
    n = b * l
    front = jnp.concatenate([jnp.zeros((N_FRONT, d), x.dtype), meta_tokens.astype(x.dtype)], axis=0)

    pos = jnp.concatenate([jnp.arange(seq) + N_META, jnp.maximum(jnp.arange(BLOCK) - N_FRONT, 0)])
    tabs_p = _rope_tables(pos, PART_ROT)
    tabs_m = _rope_tables(pos, MLA_ROPE)
    row2 = lambda a: a.reshape(1, -1)

    ffn1 = tuple(w.astype(BF16) for w in (ffn1_w_gate, ffn1_w_up, ffn1_w_down))
    ffn2 = tuple(w.astype(BF16) for w in (ffn2_w_gate, ffn2_w_up, ffn2_w_down))
    out = None
    for ly in range(depth):
        w1 = (row2(ffn1_norm[ly]), *ffn1)
        h = _ffn_first(x, front, *w1).reshape(n, d) if ly == 0 else _ffn(h, *w1, ly)
        h3 = h.reshape(b, l, d)
        if ly % 2 == 0:
            e = ly // 2
            lambda_init = 0.8 - 0.6 * math.exp(-0.3 * ly)
            win, wuq, wukv = _ab_weights(ab_w_in[e], mla_w_uq[e], mla_w_ukv[e])
            qa, ka, va, qb, kb, vb = _ab_proj(h3, row2(mix_norm[ly]), win, row2(mla_q_norm[e]), wuq,
                                              row2(mla_kv_norm[e]), wukv, tabs_p + tabs_m)
            extra = [row2(diff_lambda_q1[e]), row2(diff_lambda_k1[e]), row2(diff_lambda_q2[e]),
                     row2(diff_lambda_k2[e]), diff_subln[e].reshape(-1, 1)]
            oa = _causal_attention(functools.partial(_diff_kernel, lambda_init=lambda_init), qa, ka, va, extra,
                                   heads=DIFF_HEADS, dk=LANES, reps=2, tq=ATT_TQ, tk=ATT_TK, seq=seq,
                                   name="diff_attn")
            ob = _causal_attention(_mla_kernel, qb, kb, vb, [], heads=MLA_HEADS, dk=2 * LANES, reps=1,
                                   tq=ATT_TQ, tk=ATT_TK, seq=seq, name="mla_attn")
            wo = ab_w_out[e].astype(BF16)
            mix, bias = [(oa, wo[:512]), (ob, wo[512:])], None
        else:
            o = ly // 2
            q, k, v = _swa_proj(h3, row2(mix_norm[ly]), swa_w_qkv[o].astype(BF16), row2(swa_b_qkv[o]), tabs_p)
            att = _swa_attention(q, k, v, swa_sinks[o], seq=seq)
            mix, bias = [(att, swa_w_out[o].astype(BF16))], row2(swa_b_out[o])
        w2 = (row2(ffn2_norm[ly]), *ffn2)
        if ly == depth - 1:
            out = _ffn_final(h3, *w2, ly, row2(final_norm), seq, mix=mix, bias=bias)
        else:
            h = _ffn(h, *w2, ly, mix=[(a.reshape(n, -1), w) for a, w in mix], bias=bias)
    return out
```

```python
import functools
import math

import jax
import jax.numpy as jnp
from jax import lax
from jax.experimental import pallas as pl
from jax.experimental.pallas import tpu as pltpu

F32 = jnp.float32
BF16 = jnp.bfloat16

D_MODEL = 1024
N_META = 16
BLOCK = 128
N_FRONT = BLOCK - N_META
ROPE_THETA = 500000.0
EPS = 1e-6
NEG_INF = -1e30
D_FF = 2816

DIFF_HEADS = 4
DIFF_HEAD_DIM = 64
DIFF_V_DIM = 2 * DIFF_HEAD_DIM
DIFF_EPS = 1e-5
MLA_HEADS = 4
MLA_NOPE = 128
MLA_ROPE = 64
MLA_V = 128
MLA_Q_RANK = 256
MLA_KV_RANK = 256
SWA_HEADS = 16
SWA_KV_HEADS = 2
SWA_GROUP = SWA_HEADS // SWA_KV_HEADS
SWA_HEAD_DIM = 64
WINDOW = 128

PART_ROT = DIFF_HEAD_DIM // 4
LANES = 128
VMEM_LIMIT = 56 * 1024 * 1024

FFN_TM = 640
FFN_FC = 256
PROJ_TM = 640
FFN_FINAL_TM = 512
ATT_TK = 512
ATT_TQ = 2 * ATT_TK
ATT_CW = 256
ATT_GROUP = 2
VT_FILL_UNROLL = 5
LOG2E = 1.4426950408889634
SWA_TQ = 256


def _cparams(n_axes):
    return pltpu.CompilerParams(dimension_semantics=("arbitrary",) * n_axes,
                                vmem_limit_bytes=VMEM_LIMIT)


def _resident(shape):
    nd = len(shape)
    return pl.BlockSpec(shape, lambda *_: (0,) * nd, pipeline_mode=pl.Buffered(1))


def _rms(x, g, eps):
    return x * lax.rsqrt(jnp.mean(x * x, axis=-1, keepdims=True) + eps) * g


def _dot(a, b):
    return jnp.dot(a, b, preferred_element_type=F32)


def _dot_nt(a, b):
    return lax.dot_general(a, b, (((1,), (1,)), ((), ())), preferred_element_type=F32)


def _ffn_kernel(x_ref, *refs, final=False, aliased=False, n_mix=0, mix_bias=False):
    mix, refs = refs[:2 * n_mix], refs[2 * n_mix:]
    if mix_bias:
        mb_ref, refs = refs[0], refs[1:]
    g_ref, wg_ref, wu_ref, wd_ref, *rest = refs
    if final:
        fg_ref, o_ref, act_ref = rest
    elif aliased:
        _, o_ref, act_ref = rest
    else:
        o_ref, act_ref = rest
    x = x_ref[...].reshape(x_ref.shape[-2:])
    for a_ref, w_ref in zip(mix[0::2], mix[1::2]):
        x = x + _dot(a_ref[...].reshape(a_ref.shape[-2:]), w_ref[...])
    if mix_bias:
        x = x + mb_ref[...]
    xn = _rms(x, g_ref[...], EPS).astype(BF16)
    for c in range(D_FF // FFN_FC):
        sl = slice(c * FFN_FC, (c + 1) * FFN_FC)
        g = _dot(xn, wg_ref[:, sl])
        u = _dot(xn, wu_ref[:, sl])
        act_ref[:, sl] = (g * (1.0 / (1.0 + jnp.exp(-g))) * u).astype(BF16)
    y = x + 0.5 * _dot(act_ref[...], wd_ref[...])
    if final:
        y = _rms(y, fg_ref[...], EPS)
    o_ref[...] = y.reshape(o_ref.shape)


def _mix_args(mix, bias, row):
    args, specs = [], []
    for a, w in mix:
        args += [a, w]
        specs += [row(a.shape[-1]), _resident(w.shape)]
    if bias is not None:
        args.append(bias)
        specs.append(_resident(bias.shape))
    return args, specs, dict(n_mix=len(mix), mix_bias=bias is not None)


def _ffn_wspecs(ly):
    mat = lambda r, c: pl.BlockSpec((None, r, c), lambda *_: (ly, 0, 0), pipeline_mode=pl.Buffered(1))
    return [_resident((1, D_MODEL)), mat(D_MODEL, D_FF), mat(D_MODEL, D_FF), mat(D_FF, D_MODEL)]


def _ffn(h, g, wg, wu, wd, ly, mix=(), bias=None):
    n = h.shape[0]
    tm = FFN_TM
    assert n % tm == 0
    row = lambda w: pl.BlockSpec((tm, w), lambda i: (i, 0))
    margs, mspecs, mkw = _mix_args(mix, bias, row)
    return pl.pallas_call(
        functools.partial(_ffn_kernel, **mkw),
        grid=(n // tm,),
        in_specs=[row(D_MODEL)] + mspecs + _ffn_wspecs(ly),
        out_specs=row(D_MODEL),
        out_shape=jax.ShapeDtypeStruct((n, D_MODEL), F32),
        scratch_shapes=[pltpu.VMEM((tm, D_FF), BF16)],
        compiler_params=_cparams(1),
        name="ffn",
    )(h, *margs, g, wg, wu, wd)


def _ffn_first(x, front, g, wg, wu, wd):
    b, seq, d = x.shape
    tm = FFN_FINAL_TM
    assert seq % tm == 0
    weights = _ffn_wspecs(0)
    row = pl.BlockSpec((1, tm, d), lambda bb, i: (bb, i, 0))
    h3 = pl.pallas_call(
        _ffn_kernel,
        grid=(b, seq // tm),
        in_specs=[row] + weights,
        out_specs=row,
        out_shape=jax.ShapeDtypeStruct((b, seq + BLOCK, d), F32),
        scratch_shapes=[pltpu.VMEM((tm, D_FF), BF16)],
        compiler_params=_cparams(2),
        name="ffn_first",
    )(x, g, wg, wu, wd)
    return pl.pallas_call(
        functools.partial(_ffn_kernel, aliased=True),
        grid=(b,),
        in_specs=[pl.BlockSpec((BLOCK, d), lambda bb: (0, 0))] + weights + [pl.BlockSpec(memory_space=pl.ANY)],
        out_specs=pl.BlockSpec((1, BLOCK, d), lambda bb: (bb, seq // BLOCK, 0)),
        out_shape=jax.ShapeDtypeStruct(h3.shape, F32),
        scratch_shapes=[pltpu.VMEM((BLOCK, D_FF), BF16)],
        input_output_aliases={5: 0},
        compiler_params=_cparams(1),
        name="ffn_first_front",
    )(front, g, wg, wu, wd, h3)


def _ffn_final(h3, g, wg, wu, wd, ly, fg, seq, mix=(), bias=None):
    b = h3.shape[0]
    tm = FFN_FINAL_TM
    assert seq % tm == 0
    row = lambda w: pl.BlockSpec((1, tm, w), lambda bb, i: (bb, i, 0))
    margs, mspecs, mkw = _mix_args(mix, bias, row)
    return pl.pallas_call(
        functools.partial(_ffn_kernel, final=True, **mkw),
        grid=(b, seq // tm),
        in_specs=[row(D_MODEL)] + mspecs + _ffn_wspecs(ly) + [_resident((1, D_MODEL))],
        out_specs=row(D_MODEL),
        out_shape=jax.ShapeDtypeStruct((b, seq, D_MODEL), F32),
        scratch_shapes=[pltpu.VMEM((tm, D_FF), BF16)],
        compiler_params=_cparams(2),
        name="ffn_final",
    )(h3, *margs, g, wg, wu, wd, fg)


def _rope(y, c, s, half):
    w = y.shape[1]
    reps = w // LANES
    if reps > 1:
        c = jnp.concatenate([c] * reps, axis=1)
        s = jnp.concatenate([s] * reps, axis=1)
    lane = lax.broadcasted_iota(jnp.int32, y.shape, 1)
    first = (lane & 63) < half
    partner = jnp.where(first, pltpu.roll(y, w - half, 1), pltpu.roll(y, half, 1))
    return y * c + partner * s


def _rope_tables(pos, rot_dim):
    half = rot_dim // 2
    g = jnp.arange(LANES) & 63
    inv = ROPE_THETA ** (-(2 * (g % half)).astype(F32) / rot_dim)
    ang = pos.astype(F32)[:, None] * inv[None, :]
    rotated = (g < rot_dim)[None, :]
    cos = jnp.where(rotated, jnp.cos(ang), 1.0)
    sin = jnp.where(rotated, jnp.where((g < half)[None, :], -jnp.sin(ang), jnp.sin(ang)), 0.0)
    return cos, sin


AB_IN_PAD = 3 * 512 + MLA_Q_RANK + MLA_KV_RANK + LANES


def _ab_proj_kernel(h_ref, g_ref, win_ref, qn_ref, wuq_ref, kvn_ref, wukv_ref, ca_ref, sa_ref, cm_ref, sm_ref,
                    qa_ref, ka_ref, va_ref, qb_ref, kb_ref, vb_ref):
    xn = _rms(h_ref[0], g_ref[...], EPS).astype(BF16)
    ca, sa, cm, sm = ca_ref[...], sa_ref[...], cm_ref[...], sm_ref[...]
    half_a = PART_ROT // 2
    half_m = MLA_ROPE // 2
    qa = _rope(_dot(xn, win_ref[:, 0:512]), ca, sa, half_a)
    qa_ref[0] = (qa * (DIFF_HEAD_DIM ** -0.5 * LOG2E)).astype(BF16)
    ka_ref[0] = _rope(_dot(xn, win_ref[:, 512:1024]), ca, sa, half_a).astype(BF16)
    va_ref[0] = _dot(xn, win_ref[:, 1024:1536]).astype(BF16)
    cq = _dot(xn, win_ref[:, 1536:1792])
    ckv = _dot(xn, win_ref[:, 1792:2048])
    kr = _rope(_dot(xn, win_ref[:, 2048:2176]), cm, sm, half_m).astype(BF16)
    qb = _dot(_rms(cq, qn_ref[...], EPS).astype(BF16), wuq_ref[...])
    scale = (MLA_NOPE + MLA_ROPE) ** -0.5 * LOG2E
    for hh in range(MLA_HEADS):
        o = 2 * LANES * hh
        qb_ref[0, :, o:o + LANES] = (qb[:, o:o + LANES] * scale).astype(BF16)
        qb_ref[0, :, o + LANES:o + 2 * LANES] = (_rope(qb[:, o + LANES:o + 2 * LANES], cm, sm, half_m) * scale).astype(BF16)
    kv = _dot(_rms(ckv, kvn_ref[...], EPS).astype(BF16), wukv_ref[...])
    for hh in range(MLA_HEADS):
        o = 2 * LANES * hh
        kb_ref[0, :, o:o + LANES] = kv[:, LANES * hh:LANES * (hh + 1)].astype(BF16)
        kb_ref[0, :, o + LANES:o + 2 * LANES] = kr
    vb_ref[0] = kv[:, MLA_HEADS * MLA_NOPE:].astype(BF16)


def _ab_proj(h3, g, win, qn, wuq, kvn, wukv, tabs):
    b, l, _ = h3.shape
    tm = PROJ_TM
    assert l % tm == 0
    row = lambda w: pl.BlockSpec((1, tm, w), lambda j, bb: (bb, j, 0))
    tab = pl.BlockSpec((tm, LANES), lambda j, bb: (j, 0))
    outw = (512, 512, 512, 1024, 1024, 512)
    return pl.pallas_call(
        _ab_proj_kernel,
        grid=(l // tm, b),
        in_specs=[row(D_MODEL), _resident((1, D_MODEL)), _resident((D_MODEL, AB_IN_PAD)),
                  _resident((1, MLA_Q_RANK)), _resident((MLA_Q_RANK, 1024)),
                  _resident((1, MLA_KV_RANK)), _resident((MLA_KV_RANK, 1024)), tab, tab, tab, tab],
        out_specs=[row(w) for w in outw],
        out_shape=[jax.ShapeDtypeStruct((b, l, w), BF16) for w in outw],
        compiler_params=_cparams(2),
        name="ab_proj",
    )(h3, g, win, qn, wuq, kvn, wukv, *tabs)


def _transpose(x):
    return x.T


def _flash_step(k, vt, qt, m_ref, l_ref, acc_ref, mask, between=None):
    cw = min(ATT_CW, qt.shape[1])
    probs = {}
    for c0 in range(0, qt.shape[1], cw):
        cs = slice(c0, c0 + cw)
        s = _dot(k, qt[:, cs])
        s = jnp.where(mask(c0, s.shape), s, NEG_INF)
        m_new = jnp.max(s, axis=0, keepdims=True)
        p = jnp.exp2(s - m_new)
        l_ref[:, cs] = jnp.sum(p, axis=0, keepdims=True)
        m_ref[:, cs] = m_new
        probs[c0] = p.astype(BF16)
    if between is not None:
        between()
    for c0, p in probs.items():
        acc_ref[:, c0:c0 + cw] = _dot(vt, p)


def _produce(k, qt, s_ref, x_ref, chunks):
    for c0 in chunks:
        cs = slice(c0, c0 + ATT_CW)
        s = _dot(k, qt[:, cs])
        s_ref[:, cs] = s
        x_ref[:, cs] = jnp.max(s, axis=0, keepdims=True)


def _consume(s_ref, x_ref, vt, m_ref, l_ref, acc_ref, chunks, diag=None):
    tk = s_ref.shape[0]
    for c0 in chunks:
        cs = slice(c0, c0 + ATT_CW)
        if diag is None:
            seen = tk
        else:
            key0, tq, tri = diag
            seen = min((c0 & (tq - 1)) - key0, tk)
            if seen + ATT_CW <= 0:
                continue
        if seen == tk:
            s = s_ref[:, cs]
            smax = x_ref[:, cs]
        else:
            s = s_ref[seen:seen + ATT_CW, cs] + tri
            if seen > 0:
                s = jnp.concatenate([s_ref[0:seen, cs], s], axis=0)
            smax = jnp.max(s, axis=0, keepdims=True)
        m_prev = m_ref[:, cs]
        m_new = jnp.maximum(m_prev, smax)
        alpha = jnp.exp2(m_prev - m_new)
        p = jnp.exp2(s - m_new)
        l_ref[:, cs] = alpha * l_ref[:, cs] + jnp.sum(p, axis=0, keepdims=True)
        acc_ref[:, cs] = alpha * acc_ref[:, cs] + _dot(vt[:, :s.shape[0]], p.astype(BF16))
        m_ref[:, cs] = m_new


def _causal_attend(qt, k_ref, v_ref, scr, *, tq, tk, seq, front):
    vt_ref, m_ref, l_ref, acc_ref, s0, s1, x0, x1 = scr
    state = (m_ref, l_ref, acc_ref)
    f0 = 0 if front else seq
    kf = k_ref[0, f0:f0 + BLOCK, :]

    def valid(c0, shape):
        return lax.broadcasted_iota(jnp.int32, shape, 0) >= N_FRONT

    def causal(key0):
        def mask(c0, shape):
            key = lax.broadcasted_iota(jnp.int32, shape, 0) + key0
            qry = (lax.broadcasted_iota(jnp.int32, shape, 1) + c0) & (tq - 1)
            return key <= qry
        return mask

    if front:
        both = lambda c0, shape: valid(c0, shape) & causal(0)(c0, shape)
        _flash_step(kf, _transpose(v_ref[0, f0:f0 + BLOCK, :]), qt, *state, both)
        return
    assert tq == 2 * tk
    i = pl.program_id(2)
    nblk = v_ref.shape[1] // BLOCK
    sub = tk // BLOCK
    chunks = list(range(0, qt.shape[1], ATT_CW))
    late = [c0 for c0 in chunks if (c0 & (tq - 1)) + ATT_CW > tk]

    @pl.when(i == 0)
    def _fill():
        def fill(c, carry):
            st = pl.multiple_of(c * BLOCK, BLOCK)
            vt_ref[c] = _transpose(v_ref[0, pl.ds(st, BLOCK), :])
            return carry
        lax.fori_loop(0, nblk, fill, 0, unroll=VT_FILL_UNROLL)

    def k_block(j):
        return k_ref[0, pl.ds(pl.multiple_of(j * tk, tk), tk), :]

    def vt_block(j):
        return jnp.concatenate([vt_ref[j * sub + u] for u in range(sub)], axis=1)

    _flash_step(kf, vt_ref[seq // BLOCK], qt, *state, valid,
                between=lambda: _produce(k_block(0), qt, s0, x0, chunks))

    def step(k_next, s_next, x_next, s_cur, x_cur, vt_cur, produced=chunks, diag=None):
        for g0 in range(0, len(chunks), ATT_GROUP):
            grp = chunks[g0:g0 + ATT_GROUP]
            _produce(k_next, qt, s_next, x_next, [c0 for c0 in grp if c0 in produced])
            _consume(s_cur, x_cur, vt_cur, *state, grp, diag=diag)

    def body(p, carry):
        step(k_block(2 * p + 1), s1, x1, s0, x0, vt_block(2 * p))
        step(k_block(2 * p + 2), s0, x0, s1, x1, vt_block(2 * p + 1))
        return carry

    lax.fori_loop(0, i, body, 0)
    tri_shape = (ATT_CW, ATT_CW)
    tri = jnp.where(lax.broadcasted_iota(jnp.int32, tri_shape, 0) <= lax.broadcasted_iota(jnp.int32, tri_shape, 1),
                    0.0, NEG_INF)
    step(k_block(2 * i + 1), s1, x1, s0, x0, vt_block(2 * i), produced=late, diag=(0, tq, tri))
    _consume(s1, x1, vt_block(2 * i + 1), *state, chunks, diag=(tk, tq, tri))


def _diff_kernel(q_ref, k_ref, v_ref, lq1_ref, lk1_ref, lq2_ref, lk2_ref, sub_ref, *rest,
                 tq, tk, seq, front, lambda_init):
    if front:
        _, o_ref, *scr = rest
    else:
        o_ref, *scr = rest
    m_ref, l_ref, acc_ref = scr[1:4]
    qt = _transpose(q_ref[0])
    d = lax.broadcasted_iota(jnp.int32, qt.shape, 0)
    zero = jnp.zeros_like(qt)
    qs = jnp.concatenate([jnp.where(d < DIFF_HEAD_DIM, qt, zero), jnp.where(d >= DIFF_HEAD_DIM, qt, zero)], axis=1)
    _causal_attend(qs, k_ref, v_ref, scr, tq=tq, tk=tk, seq=seq, front=front)
    o = acc_ref[...] / l_ref[...]
    lam = (jnp.exp(jnp.sum(lq1_ref[...] * lk1_ref[...], keepdims=True))
           - jnp.exp(jnp.sum(lq2_ref[...] * lk2_ref[...], keepdims=True)) + lambda_init)
    w = o[:, :tq] - lam * o[:, tq:]
    w = w * lax.rsqrt(jnp.mean(w * w, axis=0, keepdims=True) + DIFF_EPS) * sub_ref[...]
    o_ref[0] = (w * (1.0 - lambda_init)).T.astype(BF16)


def _mla_kernel(q_ref, k_ref, v_ref, *rest, tq, tk, seq, front):
    if front:
        _, o_ref, *scr = rest
    else:
        o_ref, *scr = rest
    m_ref, l_ref, acc_ref = scr[1:4]
    _causal_attend(_transpose(q_ref[0]), k_ref, v_ref, scr, tq=tq, tk=tk, seq=seq, front=front)
    o_ref[0] = (acc_ref[...] / l_ref[...]).T.astype(BF16)


def _causal_attention(kernel, q, k, v, extra, *, heads, dk, reps, tq, tk, seq, name):
    b, l, _ = q.shape
    dv = LANES
    outs = None
    for front in (False, True):
        t = BLOCK if front else tq
        if front:
            grid = (b, heads, 1)
            qmap = lambda bb, hh, i: (bb, seq // BLOCK, hh)
        else:
            grid = (b, heads, seq // t)
            qmap = lambda bb, hh, i: (bb, i, hh)
        if front:
            kvl, kvmap = BLOCK, qmap
        else:
            kvl, kvmap = l, lambda bb, hh, i: (bb, 0, hh)
        in_specs = [pl.BlockSpec((1, t, dk), qmap), pl.BlockSpec((1, kvl, dk), kvmap),
                    pl.BlockSpec((1, kvl, dv), kvmap)]
        in_specs += [_resident(e.shape) for e in extra]
        args = [q, k, v, *extra]
        aliases = {}
        if front:
            in_specs.append(pl.BlockSpec(memory_space=pl.ANY))
            aliases = {len(args): 0}
            args.append(outs)
        outs = pl.pallas_call(
            functools.partial(kernel, tq=t, tk=min(t, tk), seq=seq, front=front),
            grid=grid,
            in_specs=in_specs,
            out_specs=pl.BlockSpec((1, t, dv), qmap),
            out_shape=jax.ShapeDtypeStruct((b, l, heads * dv), BF16),
            scratch_shapes=[pltpu.VMEM((1 if front else l // BLOCK, dv, BLOCK), BF16),
                            pltpu.VMEM((1, reps * t), F32), pltpu.VMEM((1, reps * t), F32),
                            pltpu.VMEM((dv, reps * t), F32)]
                           + [pltpu.VMEM((8 if front else tk, reps * t), F32)] * 2
                           + [pltpu.VMEM((1, reps * t), F32)] * 2,
            input_output_aliases=aliases,
            compiler_params=_cparams(3),
            name=name + ("_front" if front else ""),
        )(*args)
    return outs


SWA_Q_COLS = SWA_HEADS * SWA_HEAD_DIM
SWA_KV_COLS = SWA_KV_HEADS * SWA_HEAD_DIM
SWA_IN_COLS = SWA_Q_COLS + 2 * SWA_KV_COLS


def _swa_proj_kernel(h_ref, g_ref, w_ref, b_ref, ca_ref, sa_ref, q_ref, k_ref, v_ref):
    xn = _rms(h_ref[0], g_ref[...], EPS).astype(BF16)
    ca, sa = ca_ref[...], sa_ref[...]
    half = PART_ROT // 2
    nq = SWA_Q_COLS
    nk = SWA_KV_COLS
    q = _rope(_dot(xn, w_ref[:, 0:nq]) + b_ref[:, 0:nq], ca, sa, half)
    q_ref[0] = (q * (SWA_HEAD_DIM ** -0.5 * LOG2E)).astype(BF16)
    k_ref[0] = _rope(_dot(xn, w_ref[:, nq:nq + nk]) + b_ref[:, nq:nq + nk], ca, sa, half).astype(BF16)
    v_ref[0] = (_dot(xn, w_ref[:, nq + nk:]) + b_ref[:, nq + nk:]).astype(BF16)


def _swa_proj(h3, g, w, bias, tabs):
    b, l, _ = h3.shape
    tm = PROJ_TM
    row = lambda wd: pl.BlockSpec((1, tm, wd), lambda j, bb: (bb, j, 0))
    tab = pl.BlockSpec((tm, LANES), lambda j, bb: (j, 0))
    outw = (SWA_Q_COLS, SWA_KV_COLS, SWA_KV_COLS)
    return pl.pallas_call(
        _swa_proj_kernel,
        grid=(l // tm, b),
        in_specs=[row(D_MODEL), _resident((1, D_MODEL)), _resident((D_MODEL, SWA_IN_COLS)),
                  _resident((1, SWA_IN_COLS)), tab, tab],
        out_specs=[row(w_) for w_ in outw],
        out_shape=[jax.ShapeDtypeStruct((b, l, w_), BF16) for w_ in outw],
        compiler_params=_cparams(2),
        name="swa_proj",
    )(h3, g, w, bias, *tabs)


def _swa_kernel(sinks_ref, q_ref, k_ref, v_ref, *rest, tq, seq, front):
    if front:
        _, o_ref, s_a, s_b = rest
    else:
        o_ref, s_a, s_b = rest
    hd_dim = SWA_HEAD_DIM
    f0 = 0 if front else seq
    meta_k = k_ref[0, f0 + N_FRONT:f0 + BLOCK, :]
    meta_v = v_ref[0, f0 + N_FRONT:f0 + BLOCK, :]
    if front:
        nk = N_META
        kc, vc = meta_k, meta_v
        key = lax.broadcasted_iota(jnp.int32, (nk, tq), 0) + N_FRONT
        qry = lax.broadcasted_iota(jnp.int32, (nk, tq), 1)
        allowed = key <= qry
    else:
        i = pl.program_id(1)
        bw = tq + WINDOW
        nk = bw + N_META
        bs = pl.multiple_of(jnp.maximum(i * tq - WINDOW, 0), BLOCK)
        kc = jnp.concatenate([k_ref[0, pl.ds(bs, bw), :], meta_k], axis=0)
        vc = jnp.concatenate([v_ref[0, pl.ds(bs, bw), :], meta_v], axis=0)
        key = lax.broadcasted_iota(jnp.int32, (nk, tq), 0)
        qry = lax.broadcasted_iota(jnp.int32, (nk, tq), 1)
        dist = (i * tq + qry) - (bs + key)
        allowed = (key >= bw) | ((dist >= 0) & (dist < WINDOW))
    bias = jnp.where(allowed, 0.0, NEG_INF)
    nkp = -(-nk // LANES) * LANES
    vt = _transpose(jnp.concatenate([vc, jnp.zeros((nkp - nk, vc.shape[1]), vc.dtype)], axis=0))
    qt = _transpose(q_ref[0])
    zh = jnp.zeros((hd_dim, tq), BF16)
    sbuf = (s_a, s_b)

    def produce(g, rr):
        hd = g * SWA_GROUP + rr
        qh = qt[hd * hd_dim:(hd + 1) * hd_dim, :]
        rhs = jnp.concatenate([qh, zh] if g == 0 else [zh, qh], axis=0)
        sbuf[g][:, rr * tq:(rr + 1) * tq] = _dot(kc, rhs)

    outs = [None] * SWA_HEADS

    def consume(g, rr):
        hd = g * SWA_GROUP + rr
        sink = sinks_ref[hd] * LOG2E
        s = sbuf[g][:, rr * tq:(rr + 1) * tq] + bias
        m = jnp.maximum(jnp.max(s, axis=0, keepdims=True), sink)
        e = jnp.exp2(s - m)
        den = jnp.sum(e, axis=0, keepdims=True) + jnp.exp2(sink - m)
        eb = jnp.concatenate([e.astype(BF16), jnp.zeros((nkp - nk, tq), BF16)], axis=0)
        o = _dot(vt, eb)
        outs[hd] = o[g * hd_dim:(g + 1) * hd_dim] / den

    for rr in range(SWA_GROUP):
        produce(0, rr)
    for rr in range(SWA_GROUP):
        produce(1, rr)
        consume(0, rr)
    for rr in range(SWA_GROUP):
        consume(1, rr)
    o_ref[0] = jnp.concatenate(outs, axis=0).T.astype(BF16)


def _swa_attention(q, k, v, sinks, *, seq):
    b, l, _ = q.shape
    nq = SWA_Q_COLS
    outs = None
    for front in (False, True):
        tq = BLOCK if front else SWA_TQ
        nk = N_META if front else tq + WINDOW + N_META
        if front:
            grid = (b, 1)
            qmap = lambda bb, i: (bb, seq // BLOCK, 0)
            kvl, kvmap = BLOCK, qmap
        else:
            grid = (b, seq // tq)
            qmap = lambda bb, i: (bb, i, 0)
            kvl, kvmap = l, lambda bb, i: (bb, 0, 0)
        in_specs = [pl.BlockSpec(memory_space=pltpu.SMEM), pl.BlockSpec((1, tq, nq), qmap),
                    pl.BlockSpec((1, kvl, k.shape[2]), kvmap), pl.BlockSpec((1, kvl, v.shape[2]), kvmap)]
        args = [sinks, q, k, v]
        aliases = {}
        if front:
            in_specs.append(pl.BlockSpec(memory_space=pl.ANY))
            aliases = {len(args): 0}
            args.append(outs)
        outs = pl.pallas_call(
            functools.partial(_swa_kernel, tq=tq, seq=seq, front=front),
            grid=grid,
            in_specs=in_specs,
            out_specs=pl.BlockSpec((1, tq, nq), qmap),
            out_shape=jax.ShapeDtypeStruct((b, l, nq), BF16),
            scratch_shapes=[pltpu.VMEM((nk, SWA_GROUP * tq), F32)] * 2,
            input_output_aliases=aliases,
            compiler_params=_cparams(2),
            name="swa_attn" + ("_front" if front else ""),
        )(*args)
    return outs


def _ab_weights(w_in, w_uq, w_ukv):
    win = jnp.pad(w_in, ((0, 0), (0, AB_IN_PAD - w_in.shape[1]))).astype(BF16)
    wq = w_uq.reshape(MLA_Q_RANK, MLA_HEADS, MLA_NOPE + MLA_ROPE)
    wq = jnp.pad(wq, ((0, 0), (0, 0), (0, 2 * LANES - MLA_NOPE - MLA_ROPE))).reshape(MLA_Q_RANK, MLA_HEADS * 2 * LANES)
    wkv = w_ukv.reshape(MLA_KV_RANK, MLA_HEADS, 2, MLA_NOPE).transpose(0, 2, 1, 3).reshape(MLA_KV_RANK, -1)
    return win, wq.astype(BF16), wkv.astype(BF16)


def kernel(x, meta_tokens, ffn1_norm, ffn1_w_gate, ffn1_w_up, ffn1_w_down, mix_norm, ab_w_in, diff_lambda_q1, diff_lambda_k1, diff_lambda_q2, diff_lambda_k2, diff_subln, mla_q_norm, mla_w_uq, mla_kv_norm, mla_w_ukv, ab_w_out, swa_w_qkv, swa_b_qkv, swa_sinks, swa_w_out, swa_b_out, ffn2_norm, ffn2_w_gate, ffn2_w_up, ffn2_w_down, final_norm):
    b, seq, d = x.shape
    depth = ffn1_norm.shape[0]
    l = seq + BLOCK
    n = b * l
    front = jnp.concatenate([jnp.zeros((N_FRONT, d), x.dtype), meta_tokens.astype(x.dtype)], axis=0)

    pos = jnp.concatenate([jnp.arange(seq) + N_META, jnp.maximum(jnp.arange(BLOCK) - N_FRONT, 0)])
    tabs_p = _rope_tables(pos, PART_ROT)
    tabs_m = _rope_tables(pos, MLA_ROPE)
    row2 = lambda a: a.reshape(1, -1)

    ffn1 = tuple(w.astype(BF16) for w in (ffn1_w_gate, ffn1_w_up, ffn1_w_down))
    ffn2 = tuple(w.astype(BF16) for w in (ffn2_w_gate, ffn2_w_up, ffn2_w_down))
    out = None
    for ly in range(depth):
        w1 = (row2(ffn1_norm[ly]), *ffn1)
        h = _ffn_first(x, front, *w1).reshape(n, d) if ly == 0 else _ffn(h, *w1, ly)
        h3 = h.reshape(b, l, d)
        if ly % 2 == 0:
            e = ly // 2
            lambda_init = 0.8 - 0.6 * math.exp(-0.3 * ly)
            win, wuq, wukv = _ab_weights(ab_w_in[e], mla_w_uq[e], mla_w_ukv[e])
            qa, ka, va, qb, kb, vb = _ab_proj(h3, row2(mix_norm[ly]), win, row2(mla_q_norm[e]), wuq,
                                              row2(mla_kv_norm[e]), wukv, tabs_p + tabs_m)
            extra = [row2(diff_lambda_q1[e]), row2(diff_lambda_k1[e]), row2(diff_lambda_q2[e]),
                     row2(diff_lambda_k2[e]), diff_subln[e].reshape(-1, 1)]
            oa = _causal_attention(functools.partial(_diff_kernel, lambda_init=lambda_init), qa, ka, va, extra,
                                   heads=DIFF_HEADS, dk=LANES, reps=2, tq=ATT_TQ, tk=ATT_TK, seq=seq,
                                   name="diff_attn")
            ob = _causal_attention(_mla_kernel, qb, kb, vb, [], heads=MLA_HEADS, dk=2 * LANES, reps=1,
                                   tq=ATT_TQ, tk=ATT_TK, seq=seq, name="mla_attn")
            wo = ab_w_out[e].astype(BF16)
            mix, bias = [(oa, wo[:512]), (ob, wo[512:])], None
        else:
            o = ly // 2
            q, k, v = _swa_proj(h3, row2(mix_norm[ly]), swa_w_qkv[o].astype(BF16), row2(swa_b_qkv[o]), tabs_p)
            att = _swa_attention(q, k, v, swa_sinks[o], seq=seq)
            mix, bias = [(att, swa_w_out[o].astype(BF16))], row2(swa_b_out[o])
        w2 = (row2(ffn2_norm[ly]), *ffn2)
        if ly == depth - 1:
            out = _ffn_final(h3, *w2, ly, row2(final_norm), seq, mix=mix, bias=bias)
        else:
            h = _ffn(h, *w2, ly, mix=[(a.reshape(n, -1), w) for a, w in mix], bias=bias)
    return out
```

```python
import functools
import math

import numpy as np
import jax
import jax.numpy as jnp
from jax import lax
from jax.experimental import pallas as pl
from jax.experimental.pallas import tpu as pltpu

F32 = jnp.float32
BF16 = jnp.bfloat16

D_MODEL = 1024
N_META = 16
BLOCK = 128
N_FRONT = BLOCK - N_META
ROPE_THETA = 500000.0
EPS = 1e-6
NEG_INF = -1e30
D_FF = 2816

DIFF_HEADS = 4
DIFF_HEAD_DIM = 64
DIFF_V_DIM = 2 * DIFF_HEAD_DIM
DIFF_EPS = 1e-5
MLA_HEADS = 4
MLA_NOPE = 128
MLA_ROPE = 64
MLA_V = 128
MLA_Q_RANK = 256
MLA_KV_RANK = 256
SWA_HEADS = 16
SWA_KV_HEADS = 2
SWA_GROUP = SWA_HEADS // SWA_KV_HEADS
SWA_HEAD_DIM = 64
WINDOW = 128

PART_ROT = DIFF_HEAD_DIM // 4
LANES = 128
VMEM_LIMIT = 56 * 1024 * 1024

FFN_TM = 640
FFN_FC = 256
PROJ_TM = 640
FFN_FINAL_TM = 512
ATT_TK = 512
ATT_TQ = 2 * ATT_TK
ATT_CW = 256
ATT_GROUP = 2
VT_FILL_UNROLL = 5
LOG2E = 1.4426950408889634
SWA_TQ = 256


def _cparams(n_axes):
    return pltpu.CompilerParams(dimension_semantics=("arbitrary",) * n_axes,
                                vmem_limit_bytes=VMEM_LIMIT)


def _resident(shape):
    nd = len(shape)
    return pl.BlockSpec(shape, lambda *_: (0,) * nd, pipeline_mode=pl.Buffered(1))


def _rms(x, g, eps):
    return x * lax.rsqrt(jnp.mean(x * x, axis=-1, keepdims=True) + eps) * g


def _dot(a, b):
    return jnp.dot(a, b, preferred_element_type=F32)


def _dot_nt(a, b):
    return lax.dot_general(a, b, (((1,), (1,)), ((), ())), preferred_element_type=F32)


def _ffn_kernel(x_ref, *refs, final=False, aliased=False, n_mix=0, mix_bias=False):
    mix, refs = refs[:2 * n_mix], refs[2 * n_mix:]
    if mix_bias:
        mb_ref, refs = refs[0], refs[1:]
    g_ref, wg_ref, wu_ref, wd_ref, *rest = refs
    if final:
        fg_ref, o_ref, act_ref = rest
    elif aliased:
        _, o_ref, act_ref = rest
    else:
        o_ref, act_ref = rest
    x = x_ref[...].reshape(x_ref.shape[-2:])
    for a_ref, w_ref in zip(mix[0::2], mix[1::2]):
        x = x + _dot(a_ref[...].reshape(a_ref.shape[-2:]), w_ref[...])
    if mix_bias:
        x = x + mb_ref[...]
    xn = _rms(x, g_ref[...], EPS).astype(BF16)
    for c in range(D_FF // FFN_FC):
        sl = slice(c * FFN_FC, (c + 1) * FFN_FC)
        g = _dot(xn, wg_ref[:, sl])
        u = _dot(xn, wu_ref[:, sl])
        act_ref[:, sl] = (g * (1.0 / (1.0 + jnp.exp(-g))) * u).astype(BF16)
    y = x + 0.5 * _dot(act_ref[...], wd_ref[...])
    if final:
        y = _rms(y, fg_ref[...], EPS)
    o_ref[...] = y.reshape(o_ref.shape)


def _mix_args(mix, bias, row):
    args, specs = [], []
    for a, w in mix:
        args += [a, w]
        specs += [row(a.shape[-1]), _resident(w.shape)]
    if bias is not None:
        args.append(bias)
        specs.append(_resident(bias.shape))
    return args, specs, dict(n_mix=len(mix), mix_bias=bias is not None)


def _ffn_wspecs(ly):
    mat = lambda r, c: pl.BlockSpec((None, r, c), lambda *_: (ly, 0, 0), pipeline_mode=pl.Buffered(1))
    return [_resident((1, D_MODEL)), mat(D_MODEL, D_FF), mat(D_MODEL, D_FF), mat(D_FF, D_MODEL)]


def _ffn(h, g, wg, wu, wd, ly, mix=(), bias=None):
    n = h.shape[0]
    tm = FFN_TM
    assert n % tm == 0
    row = lambda w: pl.BlockSpec((tm, w), lambda i: (i, 0))
    margs, mspecs, mkw = _mix_args(mix, bias, row)
    return pl.pallas_call(
        functools.partial(_ffn_kernel, **mkw),
        grid=(n // tm,),
        in_specs=[row(D_MODEL)] + mspecs + _ffn_wspecs(ly),
        out_specs=row(D_MODEL),
        out_shape=jax.ShapeDtypeStruct((n, D_MODEL), F32),
        scratch_shapes=[pltpu.VMEM((tm, D_FF), BF16)],
        compiler_params=_cparams(1),
        name="ffn",
    )(h, *margs, g, wg, wu, wd)


def _ffn_first(x, front, g, wg, wu, wd):
    b, seq, d = x.shape
    tm = FFN_FINAL_TM
    assert seq % tm == 0
    weights = _ffn_wspecs(0)
    row = pl.BlockSpec((1, tm, d), lambda bb, i: (bb, i, 0))
    h3 = pl.pallas_call(
        _ffn_kernel,
        grid=(b, seq // tm),
        in_specs=[row] + weights,
        out_specs=row,
        out_shape=jax.ShapeDtypeStruct((b, seq + BLOCK, d), F32),
        scratch_shapes=[pltpu.VMEM((tm, D_FF), BF16)],
        compiler_params=_cparams(2),
        name="ffn_first",
    )(x, g, wg, wu, wd)
    return pl.pallas_call(
        functools.partial(_ffn_kernel, aliased=True),
        grid=(b,),
        in_specs=[pl.BlockSpec((BLOCK, d), lambda bb: (0, 0))] + weights + [pl.BlockSpec(memory_space=pl.ANY)],
        out_specs=pl.BlockSpec((1, BLOCK, d), lambda bb: (bb, seq // BLOCK, 0)),
        out_shape=jax.ShapeDtypeStruct(h3.shape, F32),
        scratch_shapes=[pltpu.VMEM((BLOCK, D_FF), BF16)],
        input_output_aliases={5: 0},
        compiler_params=_cparams(1),
        name="ffn_first_front",
    )(front, g, wg, wu, wd, h3)


def _ffn_final(h3, g, wg, wu, wd, ly, fg, seq, mix=(), bias=None):
    b = h3.shape[0]
    tm = FFN_FINAL_TM
    assert seq % tm == 0
    row = lambda w: pl.BlockSpec((1, tm, w), lambda bb, i: (bb, i, 0))
    margs, mspecs, mkw = _mix_args(mix, bias, row)
    return pl.pallas_call(
        functools.partial(_ffn_kernel, final=True, **mkw),
        grid=(b, seq // tm),
        in_specs=[row(D_MODEL)] + mspecs + _ffn_wspecs(ly) + [_resident((1, D_MODEL))],
        out_specs=row(D_MODEL),
        out_shape=jax.ShapeDtypeStruct((b, seq, D_MODEL), F32),
        scratch_shapes=[pltpu.VMEM((tm, D_FF), BF16)],
        compiler_params=_cparams(2),
        name="ffn_final",
    )(h3, *margs, g, wg, wu, wd, fg)


def _rope(y, c, s, half):
    w = y.shape[1]
    reps = w // LANES
    if reps > 1:
        c = jnp.concatenate([c] * reps, axis=1)
        s = jnp.concatenate([s] * reps, axis=1)
    lane = lax.broadcasted_iota(jnp.int32, y.shape, 1)
    first = (lane & 63) < half
    partner = jnp.where(first, pltpu.roll(y, w - half, 1), pltpu.roll(y, half, 1))
    return y * c + partner * s


def _rope_tables(pos, rot_dim):
    half = rot_dim // 2
    g = np.arange(LANES) & 63
    inv = np.power(np.float32(ROPE_THETA), -(2 * (g % half)).astype(np.float32) / np.float32(rot_dim))
    ang = pos.astype(np.float32)[:, None] * inv[None, :].astype(np.float32)
    rotated = (g < rot_dim)[None, :]
    cos = np.where(rotated, np.cos(ang), np.float32(1.0)).astype(np.float32)
    sin = np.where(rotated, np.where((g < half)[None, :], -np.sin(ang), np.sin(ang)), np.float32(0.0))
    return jnp.asarray(cos), jnp.asarray(sin.astype(np.float32))


AB_IN_PAD = 3 * 512 + MLA_Q_RANK + MLA_KV_RANK + LANES


def _ab_proj_kernel(h_ref, g_ref, win_ref, qn_ref, wuq_ref, kvn_ref, wukv_ref, ca_ref, sa_ref, cm_ref, sm_ref,
                    qa_ref, ka_ref, va_ref, qb_ref, kb_ref, vb_ref):
    xn = _rms(h_ref[0], g_ref[...], EPS).astype(BF16)
    ca, sa, cm, sm = ca_ref[...], sa_ref[...], cm_ref[...], sm_ref[...]
    half_a = PART_ROT // 2
    half_m = MLA_ROPE // 2
    qa = _rope(_dot(xn, win_ref[:, 0:512]), ca, sa, half_a)
    qa_ref[0] = (qa * (DIFF_HEAD_DIM ** -0.5 * LOG2E)).astype(BF16)
    ka_ref[0] = _rope(_dot(xn, win_ref[:, 512:1024]), ca, sa, half_a).astype(BF16)
    va_ref[0] = _dot(xn, win_ref[:, 1024:1536]).astype(BF16)
    cq = _dot(xn, win_ref[:, 1536:1792])
    ckv = _dot(xn, win_ref[:, 1792:2048])
    kr = _rope(_dot(xn, win_ref[:, 2048:2176]), cm, sm, half_m).astype(BF16)
    qb = _dot(_rms(cq, qn_ref[...], EPS).astype(BF16), wuq_ref[...])
    scale = (MLA_NOPE + MLA_ROPE) ** -0.5 * LOG2E
    for hh in range(MLA_HEADS):
        o = 2 * LANES * hh
        qb_ref[0, :, o:o + LANES] = (qb[:, o:o + LANES] * scale).astype(BF16)
        qb_ref[0, :, o + LANES:o + 2 * LANES] = (_rope(qb[:, o + LANES:o + 2 * LANES], cm, sm, half_m) * scale).astype(BF16)
    kv = _dot(_rms(ckv, kvn_ref[...], EPS).astype(BF16), wukv_ref[...])
    for hh in range(MLA_HEADS):
        o = 2 * LANES * hh
        kb_ref[0, :, o:o + LANES] = kv[:, LANES * hh:LANES * (hh + 1)].astype(BF16)
        kb_ref[0, :, o + LANES:o + 2 * LANES] = kr
    vb_ref[0] = kv[:, MLA_HEADS * MLA_NOPE:].astype(BF16)


def _ab_proj(h3, g, win, qn, wuq, kvn, wukv, tabs):
    b, l, _ = h3.shape
    tm = PROJ_TM
    assert l % tm == 0
    row = lambda w: pl.BlockSpec((1, tm, w), lambda j, bb: (bb, j, 0))
    tab = pl.BlockSpec((tm, LANES), lambda j, bb: (j, 0))
    outw = (512, 512, 512, 1024, 1024, 512)
    return pl.pallas_call(
        _ab_proj_kernel,
        grid=(l // tm, b),
        in_specs=[row(D_MODEL), _resident((1, D_MODEL)), _resident((D_MODEL, AB_IN_PAD)),
                  _resident((1, MLA_Q_RANK)), _resident((MLA_Q_RANK, 1024)),
                  _resident((1, MLA_KV_RANK)), _resident((MLA_KV_RANK, 1024)), tab, tab, tab, tab],
        out_specs=[row(w) for w in outw],
        out_shape=[jax.ShapeDtypeStruct((b, l, w), BF16) for w in outw],
        compiler_params=_cparams(2),
        name="ab_proj",
    )(h3, g, win, qn, wuq, kvn, wukv, *tabs)


def _transpose(x):
    return x.T


def _flash_step(k, vt, qt, m_ref, l_ref, acc_ref, mask, between=None):
    cw = min(ATT_CW, qt.shape[1])
    probs = {}
    for c0 in range(0, qt.shape[1], cw):
        cs = slice(c0, c0 + cw)
        s = _dot(k, qt[:, cs])
        s = jnp.where(mask(c0, s.shape), s, NEG_INF)
        m_new = jnp.max(s, axis=0, keepdims=True)
        p = jnp.exp2(s - m_new)
        l_ref[:, cs] = jnp.sum(p, axis=0, keepdims=True)
        m_ref[:, cs] = m_new
        probs[c0] = p.astype(BF16)
    if between is not None:
        between()
    for c0, p in probs.items():
        acc_ref[:, c0:c0 + cw] = _dot(vt, p)


def _produce(k, qt, s_ref, x_ref, chunks):
    for c0 in chunks:
        cs = slice(c0, c0 + ATT_CW)
        s = _dot(k, qt[:, cs])
        s_ref[:, cs] = s
        x_ref[:, cs] = jnp.max(s, axis=0, keepdims=True)


def _consume(s_ref, x_ref, vt, m_ref, l_ref, acc_ref, chunks, diag=None):
    tk = s_ref.shape[0]
    for c0 in chunks:
        cs = slice(c0, c0 + ATT_CW)
        if diag is None:
            seen = tk
        else:
            key0, tq, tri = diag
            seen = min((c0 & (tq - 1)) - key0, tk)
            if seen + ATT_CW <= 0:
                continue
        if seen == tk:
            s = s_ref[:, cs]
            smax = x_ref[:, cs]
        else:
            s = s_ref[seen:seen + ATT_CW, cs] + tri
            if seen > 0:
                s = jnp.concatenate([s_ref[0:seen, cs], s], axis=0)
            smax = jnp.max(s, axis=0, keepdims=True)
        m_prev = m_ref[:, cs]
        m_new = jnp.maximum(m_prev, smax)
        alpha = jnp.exp2(m_prev - m_new)
        p = jnp.exp2(s - m_new)
        l_ref[:, cs] = alpha * l_ref[:, cs] + jnp.sum(p, axis=0, keepdims=True)
        acc_ref[:, cs] = alpha * acc_ref[:, cs] + _dot(vt[:, :s.shape[0]], p.astype(BF16))
        m_ref[:, cs] = m_new


def _causal_attend(make_qt, k_ref, v_ref, scr, *, tq, tk, seq, front):
    vt_ref, m_ref, l_ref, acc_ref, s0, s1, x0, x1 = scr
    state = (m_ref, l_ref, acc_ref)
    f0 = 0 if front else seq
    kf = k_ref[0, f0:f0 + BLOCK, :]

    def valid(c0, shape):
        return lax.broadcasted_iota(jnp.int32, shape, 0) >= N_FRONT

    def causal(key0):
        def mask(c0, shape):
            key = lax.broadcasted_iota(jnp.int32, shape, 0) + key0
            qry = (lax.broadcasted_iota(jnp.int32, shape, 1) + c0) & (tq - 1)
            return key <= qry
        return mask

    if front:
        qt = make_qt()
        both = lambda c0, shape: valid(c0, shape) & causal(0)(c0, shape)
        _flash_step(kf, _transpose(v_ref[0, f0:f0 + BLOCK, :]), qt, *state, both)
        return
    assert tq == 2 * tk
    i = pl.program_id(2)
    nblk = v_ref.shape[1] // BLOCK
    sub = tk // BLOCK
    chunks = list(range(0, m_ref.shape[1], ATT_CW))
    late = [c0 for c0 in chunks if (c0 & (tq - 1)) + ATT_CW > tk]

    @pl.when(i == 0)
    def _fill():
        def fill(c, carry):
            st = pl.multiple_of(c * BLOCK, BLOCK)
            vt_ref[c] = _transpose(v_ref[0, pl.ds(st, BLOCK), :])
            return carry
        lax.fori_loop(0, nblk, fill, 0, unroll=VT_FILL_UNROLL)

    qt = make_qt()

    def k_block(j):
        return k_ref[0, pl.ds(pl.multiple_of(j * tk, tk), tk), :]

    def vt_block(j):
        return jnp.concatenate([vt_ref[j * sub + u] for u in range(sub)], axis=1)

    def step(k_next, s_next, x_next, s_cur, x_cur, vt_cur, produced=chunks, diag=None):
        for g0 in range(0, len(chunks), ATT_GROUP):
            grp = chunks[g0:g0 + ATT_GROUP]
            _produce(k_next, qt, s_next, x_next, [c0 for c0 in grp if c0 in produced])
            _consume(s_cur, x_cur, vt_cur, *state, grp, diag=diag)

    _flash_step(kf, vt_ref[seq // BLOCK], qt, *state, valid,
                between=lambda: _produce(k_block(0), qt, s0, x0, chunks))

    def body(p, carry):
        step(k_block(2 * p + 1), s1, x1, s0, x0, vt_block(2 * p))
        step(k_block(2 * p + 2), s0, x0, s1, x1, vt_block(2 * p + 1))
        return carry

    lax.fori_loop(0, i, body, 0)
    tri_shape = (ATT_CW, ATT_CW)
    tri = jnp.where(lax.broadcasted_iota(jnp.int32, tri_shape, 0) <= lax.broadcasted_iota(jnp.int32, tri_shape, 1),
                    0.0, NEG_INF)
    step(k_block(2 * i + 1), s1, x1, s0, x0, vt_block(2 * i), produced=late, diag=(0, tq, tri))
    _consume(s1, x1, vt_block(2 * i + 1), *state, chunks, diag=(tk, tq, tri))


def _diff_kernel(q_ref, k_ref, v_ref, lq1_ref, lk1_ref, lq2_ref, lk2_ref, sub_ref, *rest,
                 tq, tk, seq, front, lambda_init):
    if front:
        _, o_ref, *scr = rest
    else:
        o_ref, *scr = rest
    m_ref, l_ref, acc_ref = scr[1:4]
    def stacked_queries():
        qt = _transpose(q_ref[0])
        d = lax.broadcasted_iota(jnp.int32, qt.shape, 0)
        zero = jnp.zeros_like(qt)
        return jnp.concatenate([jnp.where(d < DIFF_HEAD_DIM, qt, zero), jnp.where(d >= DIFF_HEAD_DIM, qt, zero)],
                               axis=1)

    _causal_attend(stacked_queries, k_ref, v_ref, scr, tq=tq, tk=tk, seq=seq, front=front)
    o = acc_ref[...] / l_ref[...]
    lam = (jnp.exp(jnp.sum(lq1_ref[...] * lk1_ref[...], keepdims=True))
           - jnp.exp(jnp.sum(lq2_ref[...] * lk2_ref[...], keepdims=True)) + lambda_init)
    w = o[:, :tq] - lam * o[:, tq:]
    w = w * lax.rsqrt(jnp.mean(w * w, axis=0, keepdims=True) + DIFF_EPS) * sub_ref[...]
    o_ref[0] = (w * (1.0 - lambda_init)).T.astype(BF16)


def _mla_kernel(q_ref, k_ref, v_ref, *rest, tq, tk, seq, front):
    if front:
        _, o_ref, *scr = rest
    else:
        o_ref, *scr = rest
    m_ref, l_ref, acc_ref = scr[1:4]
    _causal_attend(lambda: _transpose(q_ref[0]), k_ref, v_ref, scr, tq=tq, tk=tk, seq=seq, front=front)
    o_ref[0] = (acc_ref[...] / l_ref[...]).T.astype(BF16)


def _causal_attention(kernel, q, k, v, extra, *, heads, dk, reps, tq, tk, seq, name):
    b, l, _ = q.shape
    dv = LANES
    outs = None
    for front in (False, True):
        t = BLOCK if front else tq
        if front:
            grid = (b, heads, 1)
            qmap = lambda bb, hh, i: (bb, seq // BLOCK, hh)
        else:
            grid = (b, heads, seq // t)
            qmap = lambda bb, hh, i: (bb, i, hh)
        if front:
            kvl, kvmap = BLOCK, qmap
        else:
            kvl, kvmap = l, lambda bb, hh, i: (bb, 0, hh)
        in_specs = [pl.BlockSpec((1, t, dk), qmap), pl.BlockSpec((1, kvl, dk), kvmap),
                    pl.BlockSpec((1, kvl, dv), kvmap)]
        in_specs += [_resident(e.shape) for e in extra]
        args = [q, k, v, *extra]
        aliases = {}
        if front:
            in_specs.append(pl.BlockSpec(memory_space=pl.ANY))
            aliases = {len(args): 0}
            args.append(outs)
        outs = pl.pallas_call(
            functools.partial(kernel, tq=t, tk=min(t, tk), seq=seq, front=front),
            grid=grid,
            in_specs=in_specs,
            out_specs=pl.BlockSpec((1, t, dv), qmap),
            out_shape=jax.ShapeDtypeStruct((b, l, heads * dv), BF16),
            scratch_shapes=[pltpu.VMEM((1 if front else l // BLOCK, dv, BLOCK), BF16),
                            pltpu.VMEM((1, reps * t), F32), pltpu.VMEM((1, reps * t), F32),
                            pltpu.VMEM((dv, reps * t), F32)]
                           + [pltpu.VMEM((8 if front else tk, reps * t), F32)] * 2
                           + [pltpu.VMEM((1, reps * t), F32)] * 2,
            input_output_aliases=aliases,
            compiler_params=_cparams(3),
            name=name + ("_front" if front else ""),
        )(*args)
    return outs


SWA_Q_COLS = SWA_HEADS * SWA_HEAD_DIM
SWA_KV_COLS = SWA_KV_HEADS * SWA_HEAD_DIM
SWA_IN_COLS = SWA_Q_COLS + 2 * SWA_KV_COLS


def _swa_proj_kernel(h_ref, g_ref, w_ref, b_ref, ca_ref, sa_ref, q_ref, k_ref, v_ref):
    xn = _rms(h_ref[0], g_ref[...], EPS).astype(BF16)
    ca, sa = ca_ref[...], sa_ref[...]
    half = PART_ROT // 2
    nq = SWA_Q_COLS
    nk = SWA_KV_COLS
    q = _rope(_dot(xn, w_ref[:, 0:nq]) + b_ref[:, 0:nq], ca, sa, half)
    q_ref[0] = (q * (SWA_HEAD_DIM ** -0.5 * LOG2E)).astype(BF16)
    k_ref[0] = _rope(_dot(xn, w_ref[:, nq:nq + nk]) + b_ref[:, nq:nq + nk], ca, sa, half).astype(BF16)
    v_ref[0] = (_dot(xn, w_ref[:, nq + nk:]) + b_ref[:, nq + nk:]).astype(BF16)


def _swa_proj(h3, g, w, bias, tabs):
    b, l, _ = h3.shape
    tm = PROJ_TM
    row = lambda wd: pl.BlockSpec((1, tm, wd), lambda j, bb: (bb, j, 0))
    tab = pl.BlockSpec((tm, LANES), lambda j, bb: (j, 0))
    outw = (SWA_Q_COLS, SWA_KV_COLS, SWA_KV_COLS)
    return pl.pallas_call(
        _swa_proj_kernel,
        grid=(l // tm, b),
        in_specs=[row(D_MODEL), _resident((1, D_MODEL)), _resident((D_MODEL, SWA_IN_COLS)),
                  _resident((1, SWA_IN_COLS)), tab, tab],
        out_specs=[row(w_) for w_ in outw],
        out_shape=[jax.ShapeDtypeStruct((b, l, w_), BF16) for w_ in outw],
        compiler_params=_cparams(2),
        name="swa_proj",
    )(h3, g, w, bias, *tabs)


def _swa_kernel(sinks_ref, q_ref, k_ref, v_ref, *rest, tq, seq, front):
    if front:
        _, o_ref, s_a, s_b = rest
    else:
        o_ref, s_a, s_b = rest
    hd_dim = SWA_HEAD_DIM
    f0 = 0 if front else seq
    meta_k = k_ref[0, f0 + N_FRONT:f0 + BLOCK, :]
    meta_v = v_ref[0, f0 + N_FRONT:f0 + BLOCK, :]
    if front:
        nk = N_META
        kc, vc = meta_k, meta_v
        key = lax.broadcasted_iota(jnp.int32, (nk, tq), 0) + N_FRONT
        qry = lax.broadcasted_iota(jnp.int32, (nk, tq), 1)
        allowed = key <= qry
    else:
        i = pl.program_id(1)
        bw = tq + WINDOW
        nk = bw + N_META
        bs = pl.multiple_of(jnp.maximum(i * tq - WINDOW, 0), BLOCK)
        kc = jnp.concatenate([k_ref[0, pl.ds(bs, bw), :], meta_k], axis=0)
        vc = jnp.concatenate([v_ref[0, pl.ds(bs, bw), :], meta_v], axis=0)
        key = lax.broadcasted_iota(jnp.int32, (nk, tq), 0)
        qry = lax.broadcasted_iota(jnp.int32, (nk, tq), 1)
        dist = (i * tq + qry) - (bs + key)
        allowed = (key >= bw) | ((dist >= 0) & (dist < WINDOW))
    bias = jnp.where(allowed, 0.0, NEG_INF)
    nkp = -(-nk // LANES) * LANES
    vt = _transpose(jnp.concatenate([vc, jnp.zeros((nkp - nk, vc.shape[1]), vc.dtype)], axis=0))
    qt = _transpose(q_ref[0])
    zh = jnp.zeros((hd_dim, tq), BF16)
    sbuf = (s_a, s_b)

    def produce(g, rr):
        hd = g * SWA_GROUP + rr
        qh = qt[hd * hd_dim:(hd + 1) * hd_dim, :]
        rhs = jnp.concatenate([qh, zh] if g == 0 else [zh, qh], axis=0)
        sbuf[g][:, rr * tq:(rr + 1) * tq] = _dot(kc, rhs)

    outs = [None] * SWA_HEADS

    def consume(g, rr):
        hd = g * SWA_GROUP + rr
        sink = sinks_ref[hd] * LOG2E
        s = sbuf[g][:, rr * tq:(rr + 1) * tq] + bias
        m = jnp.maximum(jnp.max(s, axis=0, keepdims=True), sink)
        e = jnp.exp2(s - m)
        den = jnp.sum(e, axis=0, keepdims=True) + jnp.exp2(sink - m)
        eb = jnp.concatenate([e.astype(BF16), jnp.zeros((nkp - nk, tq), BF16)], axis=0)
        o = _dot(vt, eb)
        outs[hd] = o[g * hd_dim:(g + 1) * hd_dim] / den

    for rr in range(SWA_GROUP):
        produce(0, rr)
    for rr in range(SWA_GROUP):
        produce(1, rr)
        consume(0, rr)
    for rr in range(SWA_GROUP):
        consume(1, rr)
    o_ref[0] = jnp.concatenate(outs, axis=0).T.astype(BF16)


def _swa_attention(q, k, v, sinks, *, seq):
    b, l, _ = q.shape
    nq = SWA_Q_COLS
    outs = None
    for front in (False, True):
        tq = BLOCK if front else SWA_TQ
        nk = N_META if front else tq + WINDOW + N_META
        if front:
            grid = (b, 1)
            qmap = lambda bb, i: (bb, seq // BLOCK, 0)
            kvl, kvmap = BLOCK, qmap
        else:
            grid = (b, seq // tq)
            qmap = lambda bb, i: (bb, i, 0)
            kvl, kvmap = l, lambda bb, i: (bb, 0, 0)
        in_specs = [pl.BlockSpec(memory_space=pltpu.SMEM), pl.BlockSpec((1, tq, nq), qmap),
                    pl.BlockSpec((1, kvl, k.shape[2]), kvmap), pl.BlockSpec((1, kvl, v.shape[2]), kvmap)]
        args = [sinks, q, k, v]
        aliases = {}
        if front:
            in_specs.append(pl.BlockSpec(memory_space=pl.ANY))
            aliases = {len(args): 0}
            args.append(outs)
        outs = pl.pallas_call(
            functools.partial(_swa_kernel, tq=tq, seq=seq, front=front),
            grid=grid,
            in_specs=in_specs,
            out_specs=pl.BlockSpec((1, tq, nq), qmap),
            out_shape=jax.ShapeDtypeStruct((b, l, nq), BF16),
            scratch_shapes=[pltpu.VMEM((nk, SWA_GROUP * tq), F32)] * 2,
            input_output_aliases=aliases,
            compiler_params=_cparams(2),
            name="swa_attn" + ("_front" if front else ""),
        )(*args)
    return outs


def _ab_weights(w_in, w_uq, w_ukv):
    win = jnp.pad(w_in, ((0, 0), (0, AB_IN_PAD - w_in.shape[1]))).astype(BF16)
    wq = w_uq.reshape(MLA_Q_RANK, MLA_HEADS, MLA_NOPE + MLA_ROPE)
    wq = jnp.pad(wq, ((0, 0), (0, 0), (0, 2 * LANES - MLA_NOPE - MLA_ROPE))).reshape(MLA_Q_RANK, MLA_HEADS * 2 * LANES)
    wkv = w_ukv.reshape(MLA_KV_RANK, MLA_HEADS, 2, MLA_NOPE).transpose(0, 2, 1, 3).reshape(MLA_KV_RANK, -1)
    return win, wq.astype(BF16), wkv.astype(BF16)


def kernel(x, meta_tokens, ffn1_norm, ffn1_w_gate, ffn1_w_up, ffn1_w_down, mix_norm, ab_w_in, diff_lambda_q1, diff_lambda_k1, diff_lambda_q2, diff_lambda_k2, diff_subln, mla_q_norm, mla_w_uq, mla_kv_norm, mla_w_ukv, ab_w_out, swa_w_qkv, swa_b_qkv, swa_sinks, swa_w_out, swa_b_out, ffn2_norm, ffn2_w_gate, ffn2_w_up, ffn2_w_down, final_norm):
    b, seq, d = x.shape
    depth = ffn1_norm.shape[0]
    l = seq + BLOCK
    n = b * l
    front = jnp.concatenate([jnp.zeros((N_FRONT, d), x.dtype), meta_tokens.astype(x.dtype)], axis=0)

    pos = np.concatenate([np.arange(seq) + N_META, np.maximum(np.arange(BLOCK) - N_FRONT, 0)])
    tabs_p = _rope_tables(pos, PART_ROT)
    tabs_m = _rope_tables(pos, MLA_ROPE)
    row2 = lambda a: a.reshape(1, -1)

    ffn1 = tuple(w.astype(BF16) for w in (ffn1_w_gate, ffn1_w_up, ffn1_w_down))
    ffn2 = tuple(w.astype(BF16) for w in (ffn2_w_gate, ffn2_w_up, ffn2_w_down))
    out = None
    for ly in range(depth):
        w1 = (row2(ffn1_norm[ly]), *ffn1)
        h = _ffn_first(x, front, *w1).reshape(n, d) if ly == 0 else _ffn(h, *w1, ly)
        h3 = h.reshape(b, l, d)
        if ly % 2 == 0:
            e = ly // 2
            lambda_init = 0.8 - 0.6 * math.exp(-0.3 * ly)
            win, wuq, wukv = _ab_weights(ab_w_in[e], mla_w_uq[e], mla_w_ukv[e])
            qa, ka, va, qb, kb, vb = _ab_proj(h3, row2(mix_norm[ly]), win, row2(mla_q_norm[e]), wuq,
                                              row2(mla_kv_norm[e]), wukv, tabs_p + tabs_m)
            extra = [row2(diff_lambda_q1[e]), row2(diff_lambda_k1[e]), row2(diff_lambda_q2[e]),
                     row2(diff_lambda_k2[e]), diff_subln[e].reshape(-1, 1)]
            oa = _causal_attention(functools.partial(_diff_kernel, lambda_init=lambda_init), qa, ka, va, extra,
                                   heads=DIFF_HEADS, dk=LANES, reps=2, tq=ATT_TQ, tk=ATT_TK, seq=seq,
                                   name="diff_attn")
            ob = _causal_attention(_mla_kernel, qb, kb, vb, [], heads=MLA_HEADS, dk=2 * LANES, reps=1,
                                   tq=ATT_TQ, tk=ATT_TK, seq=seq, name="mla_attn")
            wo = ab_w_out[e].astype(BF16)
            mix, bias = [(oa, wo[:512]), (ob, wo[512:])], None
        else:
            o = ly // 2
            q, k, v = _swa_proj(h3, row2(mix_norm[ly]), swa_w_qkv[o].astype(BF16), row2(swa_b_qkv[o]), tabs_p)
            att = _swa_attention(q, k, v, swa_sinks[o], seq=seq)
            mix, bias = [(att, swa_w_out[o].astype(BF16))], row2(swa_b_out[o])
        w2 = (row2(ffn2_norm[ly]), *ffn2)
        if ly == depth - 1:
            out = _ffn_final(h3, *w2, ly, row2(final_norm), seq, mix=mix, bias=bias)
        else:
            h = _ffn(h, *w2, ly, mix=[(a.reshape(n, -1), w) for a, w in mix], bias=bias)
    return out
```

```python
import functools
import math

import numpy as np
import jax
import jax.numpy as jnp
from jax import lax
from jax.experimental import pallas as pl
from jax.experimental.pallas import tpu as pltpu

F32 = jnp.float32
BF16 = jnp.bfloat16

D_MODEL = 1024
N_META = 16
BLOCK = 128
N_FRONT = BLOCK - N_META
ROPE_THETA = 500000.0
EPS = 1e-6
NEG_INF = -1e30
D_FF = 2816

DIFF_HEADS = 4
DIFF_HEAD_DIM = 64
DIFF_V_DIM = 2 * DIFF_HEAD_DIM
DIFF_EPS = 1e-5
MLA_HEADS = 4
MLA_NOPE = 128
MLA_ROPE = 64
MLA_V = 128
MLA_Q_RANK = 256
MLA_KV_RANK = 256
SWA_HEADS = 16
SWA_KV_HEADS = 2
SWA_GROUP = SWA_HEADS // SWA_KV_HEADS
SWA_HEAD_DIM = 64
WINDOW = 128

PART_ROT = DIFF_HEAD_DIM // 4
LANES = 128
VMEM_LIMIT = 56 * 1024 * 1024

FFN_TM = 640
FFN_FC = 256
PROJ_TM = 640
FFN_FINAL_TM = 512
ATT_TK = 512
ATT_TQ = 2 * ATT_TK
ATT_CW = 256
ATT_GROUP = 2
VT_FILL_UNROLL = 5
LOG2E = 1.4426950408889634
SWA_TQ = 1024
SWA_TS = 128


def _cparams(n_axes):
    return pltpu.CompilerParams(dimension_semantics=("arbitrary",) * n_axes,
                                vmem_limit_bytes=VMEM_LIMIT)


def _resident(shape):
    nd = len(shape)
    return pl.BlockSpec(shape, lambda *_: (0,) * nd, pipeline_mode=pl.Buffered(1))


def _rms(x, g, eps):
    return x * lax.rsqrt(jnp.mean(x * x, axis=-1, keepdims=True) + eps) * g


def _dot(a, b):
    return jnp.dot(a, b, preferred_element_type=F32)


def _dot_nt(a, b):
    return lax.dot_general(a, b, (((1,), (1,)), ((), ())), preferred_element_type=F32)


def _ffn_kernel(x_ref, *refs, final=False, aliased=False, n_mix=0, mix_bias=False):
    mix, refs = refs[:2 * n_mix], refs[2 * n_mix:]
    if mix_bias:
        mb_ref, refs = refs[0], refs[1:]
    g_ref, wg_ref, wu_ref, wd_ref, *rest = refs
    if final:
        fg_ref, o_ref, act_ref = rest
    elif aliased:
        _, o_ref, act_ref = rest
    else:
        o_ref, act_ref = rest
    x = x_ref[...].reshape(x_ref.shape[-2:])
    for a_ref, w_ref in zip(mix[0::2], mix[1::2]):
        x = x + _dot(a_ref[...].reshape(a_ref.shape[-2:]), w_ref[...])
    if mix_bias:
        x = x + mb_ref[...]
    xn = _rms(x, g_ref[...], EPS).astype(BF16)
    for c in range(D_FF // FFN_FC):
        sl = slice(c * FFN_FC, (c + 1) * FFN_FC)
        g = _dot(xn, wg_ref[:, sl].astype(BF16))
        u = _dot(xn, wu_ref[:, sl].astype(BF16))
        act_ref[:, sl] = (g * (1.0 / (1.0 + jnp.exp(-g))) * u).astype(BF16)
    y = x + 0.5 * _dot(act_ref[...], wd_ref[...])
    if final:
        y = _rms(y, fg_ref[...], EPS)
    o_ref[...] = y.reshape(o_ref.shape)


def _mix_args(mix, bias, row):
    args, specs = [], []
    for a, w in mix:
        args += [a, w]
        specs += [row(a.shape[-1]), _resident(w.shape)]
    if bias is not None:
        args.append(bias)
        specs.append(_resident(bias.shape))
    return args, specs, dict(n_mix=len(mix), mix_bias=bias is not None)


def _ffn_wspecs(ly):
    mat = lambda r, c: pl.BlockSpec((None, r, c), lambda *_: (ly, 0, 0), pipeline_mode=pl.Buffered(1))
    return [_resident((1, D_MODEL)), mat(D_MODEL, D_FF), mat(D_MODEL, D_FF), mat(D_FF, D_MODEL)]


def _ffn(h, g, wg, wu, wd, ly, mix=(), bias=None):
    n = h.shape[0]
    tm = FFN_TM
    assert n % tm == 0
    row = lambda w: pl.BlockSpec((tm, w), lambda i: (i, 0))
    margs, mspecs, mkw = _mix_args(mix, bias, row)
    return pl.pallas_call(
        functools.partial(_ffn_kernel, **mkw),
        grid=(n // tm,),
        in_specs=[row(D_MODEL)] + mspecs + _ffn_wspecs(ly),
        out_specs=row(D_MODEL),
        out_shape=jax.ShapeDtypeStruct((n, D_MODEL), F32),
        scratch_shapes=[pltpu.VMEM((tm, D_FF), BF16)],
        compiler_params=_cparams(1),
        name="ffn",
    )(h, *margs, g, wg, wu, wd)


def _ffn_first(x, front, g, wg, wu, wd):
    b, seq, d = x.shape
    tm = FFN_FINAL_TM
    assert seq % tm == 0
    weights = _ffn_wspecs(0)
    row = pl.BlockSpec((1, tm, d), lambda bb, i: (bb, i, 0))
    h3 = pl.pallas_call(
        _ffn_kernel,
        grid=(b, seq // tm),
        in_specs=[row] + weights,
        out_specs=row,
        out_shape=jax.ShapeDtypeStruct((b, seq + BLOCK, d), F32),
        scratch_shapes=[pltpu.VMEM((tm, D_FF), BF16)],
        compiler_params=_cparams(2),
        name="ffn_first",
    )(x, g, wg, wu, wd)
    return pl.pallas_call(
        functools.partial(_ffn_kernel, aliased=True),
        grid=(b,),
        in_specs=[pl.BlockSpec((BLOCK, d), lambda bb: (0, 0))] + weights + [pl.BlockSpec(memory_space=pl.ANY)],
        out_specs=pl.BlockSpec((1, BLOCK, d), lambda bb: (bb, seq // BLOCK, 0)),
        out_shape=jax.ShapeDtypeStruct(h3.shape, F32),
        scratch_shapes=[pltpu.VMEM((BLOCK, D_FF), BF16)],
        input_output_aliases={5: 0},
        compiler_params=_cparams(1),
        name="ffn_first_front",
    )(front, g, wg, wu, wd, h3)


def _ffn_final(h3, g, wg, wu, wd, ly, fg, seq, mix=(), bias=None):
    b = h3.shape[0]
    tm = FFN_FINAL_TM
    assert seq % tm == 0
    row = lambda w: pl.BlockSpec((1, tm, w), lambda bb, i: (bb, i, 0))
    margs, mspecs, mkw = _mix_args(mix, bias, row)
    return pl.pallas_call(
        functools.partial(_ffn_kernel, final=True, **mkw),
        grid=(b, seq // tm),
        in_specs=[row(D_MODEL)] + mspecs + _ffn_wspecs(ly) + [_resident((1, D_MODEL))],
        out_specs=row(D_MODEL),
        out_shape=jax.ShapeDtypeStruct((b, seq, D_MODEL), F32),
        scratch_shapes=[pltpu.VMEM((tm, D_FF), BF16)],
        compiler_params=_cparams(2),
        name="ffn_final",
    )(h3, *margs, g, wg, wu, wd, fg)


def _rope(y, c, s, half):
    w = y.shape[1]
    reps = w // LANES
    if reps > 1:
        c = jnp.concatenate([c] * reps, axis=1)
        s = jnp.concatenate([s] * reps, axis=1)
    lane = lax.broadcasted_iota(jnp.int32, y.shape, 1)
    first = (lane & 63) < half
    partner = jnp.where(first, pltpu.roll(y, w - half, 1), pltpu.roll(y, half, 1))
    return y * c + partner * s


def _rope_tables(pos, rot_dim):
    half = rot_dim // 2
    g = np.arange(LANES) & 63
    inv = np.power(np.float32(ROPE_THETA), -(2 * (g % half)).astype(np.float32) / np.float32(rot_dim))
    ang = pos.astype(np.float32)[:, None] * inv[None, :].astype(np.float32)
    rotated = (g < rot_dim)[None, :]
    cos = np.where(rotated, np.cos(ang), np.float32(1.0)).astype(np.float32)
    sin = np.where(rotated, np.where((g < half)[None, :], -np.sin(ang), np.sin(ang)), np.float32(0.0))
    return jnp.asarray(cos), jnp.asarray(sin.astype(np.float32))


AB_IN_PAD = 3 * 512 + MLA_Q_RANK + MLA_KV_RANK + LANES


def _ab_proj_kernel(h_ref, g_ref, win_ref, qn_ref, wuq_ref, kvn_ref, wukv_ref, ca_ref, sa_ref, cm_ref, sm_ref,
                    qa_ref, ka_ref, va_ref, qb_ref, kb_ref, vb_ref):
    xn = _rms(h_ref[0], g_ref[...], EPS).astype(BF16)
    ca, sa, cm, sm = ca_ref[...], sa_ref[...], cm_ref[...], sm_ref[...]
    half_a = PART_ROT // 2
    half_m = MLA_ROPE // 2
    qa = _rope(_dot(xn, win_ref[:, 0:512]), ca, sa, half_a)
    qa_ref[0] = (qa * (DIFF_HEAD_DIM ** -0.5 * LOG2E)).astype(BF16)
    ka_ref[0] = _rope(_dot(xn, win_ref[:, 512:1024]), ca, sa, half_a).astype(BF16)
    va_ref[0] = _dot(xn, win_ref[:, 1024:1536]).astype(BF16)
    cq = _dot(xn, win_ref[:, 1536:1792])
    ckv = _dot(xn, win_ref[:, 1792:2048])
    kr = _rope(_dot(xn, win_ref[:, 2048:2176]), cm, sm, half_m).astype(BF16)
    qb = _dot(_rms(cq, qn_ref[...], EPS).astype(BF16), wuq_ref[...])
    scale = (MLA_NOPE + MLA_ROPE) ** -0.5 * LOG2E
    for hh in range(MLA_HEADS):
        o = 2 * LANES * hh
        qb_ref[0, :, o:o + LANES] = (qb[:, o:o + LANES] * scale).astype(BF16)
        qb_ref[0, :, o + LANES:o + 2 * LANES] = (_rope(qb[:, o + LANES:o + 2 * LANES], cm, sm, half_m) * scale).astype(BF16)
    kv = _dot(_rms(ckv, kvn_ref[...], EPS).astype(BF16), wukv_ref[...])
    for hh in range(MLA_HEADS):
        o = 2 * LANES * hh
        kb_ref[0, :, o:o + LANES] = kv[:, LANES * hh:LANES * (hh + 1)].astype(BF16)
        kb_ref[0, :, o + LANES:o + 2 * LANES] = kr
    vb_ref[0] = kv[:, MLA_HEADS * MLA_NOPE:].astype(BF16)


def _ab_proj(h3, g, win, qn, wuq, kvn, wukv, tabs):
    b, l, _ = h3.shape
    tm = PROJ_TM
    assert l % tm == 0
    row = lambda w: pl.BlockSpec((1, tm, w), lambda j, bb: (bb, j, 0))
    tab = pl.BlockSpec((tm, LANES), lambda j, bb: (j, 0))
    outw = (512, 512, 512, 1024, 1024, 512)
    return pl.pallas_call(
        _ab_proj_kernel,
        grid=(l // tm, b),
        in_specs=[row(D_MODEL), _resident((1, D_MODEL)), _resident((D_MODEL, AB_IN_PAD)),
                  _resident((1, MLA_Q_RANK)), _resident((MLA_Q_RANK, 1024)),
                  _resident((1, MLA_KV_RANK)), _resident((MLA_KV_RANK, 1024)), tab, tab, tab, tab],
        out_specs=[row(w) for w in outw],
        out_shape=[jax.ShapeDtypeStruct((b, l, w), BF16) for w in outw],
        compiler_params=_cparams(2),
        name="ab_proj",
    )(h3, g, win, qn, wuq, kvn, wukv, *tabs)


def _transpose(x):
    return x.T


def _flash_step(k, vt, qt, m_ref, l_ref, acc_ref, mask, between=None):
    cw = min(ATT_CW, qt.shape[1])
    probs = {}
    for c0 in range(0, qt.shape[1], cw):
        cs = slice(c0, c0 + cw)
        s = _dot(k, qt[:, cs])
        s = jnp.where(mask(c0, s.shape), s, NEG_INF)
        m_new = jnp.max(s, axis=0, keepdims=True)
        p = jnp.exp2(s - m_new)
        l_ref[:, cs] = jnp.sum(p, axis=0, keepdims=True)
        m_ref[:, cs] = m_new
        probs[c0] = p.astype(BF16)
    if between is not None:
        between()
    for c0, p in probs.items():
        acc_ref[:, c0:c0 + cw] = _dot(vt, p)


def _produce(k, qt, s_ref, x_ref, chunks):
    for c0 in chunks:
        cs = slice(c0, c0 + ATT_CW)
        s = _dot(k, qt[:, cs])
        s_ref[:, cs] = s
        x_ref[:, cs] = jnp.max(s, axis=0, keepdims=True)


def _consume(s_ref, x_ref, vt, m_ref, l_ref, acc_ref, chunks, diag=None):
    tk = s_ref.shape[0]
    for c0 in chunks:
        cs = slice(c0, c0 + ATT_CW)
        if diag is None:
            seen = tk
        else:
            key0, tq, tri = diag
            seen = min((c0 & (tq - 1)) - key0, tk)
            if seen + ATT_CW <= 0:
                continue
        if seen == tk:
            s = s_ref[:, cs]
            smax = x_ref[:, cs]
        else:
            s = s_ref[seen:seen + ATT_CW, cs] + tri
            if seen > 0:
                s = jnp.concatenate([s_ref[0:seen, cs], s], axis=0)
            smax = jnp.max(s, axis=0, keepdims=True)
        m_prev = m_ref[:, cs]
        m_new = jnp.maximum(m_prev, smax)
        alpha = jnp.exp2(m_prev - m_new)
        p = jnp.exp2(s - m_new)
        l_ref[:, cs] = alpha * l_ref[:, cs] + jnp.sum(p, axis=0, keepdims=True)
        acc_ref[:, cs] = alpha * acc_ref[:, cs] + _dot(vt[:, :s.shape[0]], p.astype(BF16))
        m_ref[:, cs] = m_new


def _causal_attend(make_qt, k_ref, v_ref, scr, *, tq, tk, seq, front):
    vt_ref, m_ref, l_ref, acc_ref, s0, s1, x0, x1 = scr
    state = (m_ref, l_ref, acc_ref)
    f0 = 0 if front else seq
    kf = k_ref[0, f0:f0 + BLOCK, :]

    def valid(c0, shape):
        return lax.broadcasted_iota(jnp.int32, shape, 0) >= N_FRONT

    def causal(key0):
        def mask(c0, shape):
            key = lax.broadcasted_iota(jnp.int32, shape, 0) + key0
            qry = (lax.broadcasted_iota(jnp.int32, shape, 1) + c0) & (tq - 1)
            return key <= qry
        return mask

    if front:
        qt = make_qt()
        both = lambda c0, shape: valid(c0, shape) & causal(0)(c0, shape)
        _flash_step(kf, _transpose(v_ref[0, f0:f0 + BLOCK, :]), qt, *state, both)
        return
    assert tq == 2 * tk
    i = pl.program_id(2)
    nblk = v_ref.shape[1] // BLOCK
    sub = tk // BLOCK
    chunks = list(range(0, m_ref.shape[1], ATT_CW))
    late = [c0 for c0 in chunks if (c0 & (tq - 1)) + ATT_CW > tk]

    @pl.when(i == 0)
    def _fill():
        def fill(c, carry):
            st = pl.multiple_of(c * BLOCK, BLOCK)
            vt_ref[c] = _transpose(v_ref[0, pl.ds(st, BLOCK), :])
            return carry
        lax.fori_loop(0, nblk, fill, 0, unroll=VT_FILL_UNROLL)

    qt = make_qt()

    def k_block(j):
        return k_ref[0, pl.ds(pl.multiple_of(j * tk, tk), tk), :]

    def vt_block(j):
        return jnp.concatenate([vt_ref[j * sub + u] for u in range(sub)], axis=1)

    def step(k_next, s_next, x_next, s_cur, x_cur, vt_cur, produced=chunks, diag=None):
        for g0 in range(0, len(chunks), ATT_GROUP):
            grp = chunks[g0:g0 + ATT_GROUP]
            _produce(k_next, qt, s_next, x_next, [c0 for c0 in grp if c0 in produced])
            _consume(s_cur, x_cur, vt_cur, *state, grp, diag=diag)

    _flash_step(kf, vt_ref[seq // BLOCK], qt, *state, valid,
                between=lambda: _produce(k_block(0), qt, s0, x0, chunks))

    def body(p, carry):
        step(k_block(2 * p + 1), s1, x1, s0, x0, vt_block(2 * p))
        step(k_block(2 * p + 2), s0, x0, s1, x1, vt_block(2 * p + 1))
        return carry

    lax.fori_loop(0, i, body, 0)
    tri_shape = (ATT_CW, ATT_CW)
    tri = jnp.where(lax.broadcasted_iota(jnp.int32, tri_shape, 0) <= lax.broadcasted_iota(jnp.int32, tri_shape, 1),
                    0.0, NEG_INF)
    step(k_block(2 * i + 1), s1, x1, s0, x0, vt_block(2 * i), produced=late, diag=(0, tq, tri))
    _consume(s1, x1, vt_block(2 * i + 1), *state, chunks, diag=(tk, tq, tri))


def _diff_kernel(q_ref, k_ref, v_ref, lq1_ref, lk1_ref, lq2_ref, lk2_ref, sub_ref, *rest,
                 tq, tk, seq, front, lambda_init):
    if front:
        _, o_ref, *scr = rest
    else:
        o_ref, *scr = rest
    m_ref, l_ref, acc_ref = scr[1:4]
    def stacked_queries():
        qt = _transpose(q_ref[0])
        d = lax.broadcasted_iota(jnp.int32, qt.shape, 0)
        zero = jnp.zeros_like(qt)
        return jnp.concatenate([jnp.where(d < DIFF_HEAD_DIM, qt, zero), jnp.where(d >= DIFF_HEAD_DIM, qt, zero)],
                               axis=1)

    _causal_attend(stacked_queries, k_ref, v_ref, scr, tq=tq, tk=tk, seq=seq, front=front)
    o = acc_ref[...] / l_ref[...]
    lam = (jnp.exp(jnp.sum(lq1_ref[...] * lk1_ref[...], keepdims=True))
           - jnp.exp(jnp.sum(lq2_ref[...] * lk2_ref[...], keepdims=True)) + lambda_init)
    w = o[:, :tq] - lam * o[:, tq:]
    w = w * lax.rsqrt(jnp.mean(w * w, axis=0, keepdims=True) + DIFF_EPS) * sub_ref[...]
    o_ref[0] = (w * (1.0 - lambda_init)).T.astype(BF16)


def _mla_kernel(q_ref, k_ref, v_ref, *rest, tq, tk, seq, front):
    if front:
        _, o_ref, *scr = rest
    else:
        o_ref, *scr = rest
    m_ref, l_ref, acc_ref = scr[1:4]
    _causal_attend(lambda: _transpose(q_ref[0]), k_ref, v_ref, scr, tq=tq, tk=tk, seq=seq, front=front)
    o_ref[0] = (acc_ref[...] / l_ref[...]).T.astype(BF16)


def _causal_attention(kernel, q, k, v, extra, *, heads, dk, reps, tq, tk, seq, name):
    b, l, _ = q.shape
    dv = LANES
    outs = None
    for front in (False, True):
        t = BLOCK if front else tq
        if front:
            grid = (b, heads, 1)
            qmap = lambda bb, hh, i: (bb, seq // BLOCK, hh)
        else:
            grid = (b, heads, seq // t)
            qmap = lambda bb, hh, i: (bb, i, hh)
        if front:
            kvl, kvmap = BLOCK, qmap
        else:
            kvl, kvmap = l, lambda bb, hh, i: (bb, 0, hh)
        in_specs = [pl.BlockSpec((1, t, dk), qmap), pl.BlockSpec((1, kvl, dk), kvmap),
                    pl.BlockSpec((1, kvl, dv), kvmap)]
        in_specs += [_resident(e.shape) for e in extra]
        args = [q, k, v, *extra]
        aliases = {}
        if front:
            in_specs.append(pl.BlockSpec(memory_space=pl.ANY))
            aliases = {len(args): 0}
            args.append(outs)
        outs = pl.pallas_call(
            functools.partial(kernel, tq=t, tk=min(t, tk), seq=seq, front=front),
            grid=grid,
            in_specs=in_specs,
            out_specs=pl.BlockSpec((1, t, dv), qmap),
            out_shape=jax.ShapeDtypeStruct((b, l, heads * dv), BF16),
            scratch_shapes=[pltpu.VMEM((1 if front else l // BLOCK, dv, BLOCK), BF16),
                            pltpu.VMEM((1, reps * t), F32), pltpu.VMEM((1, reps * t), F32),
                            pltpu.VMEM((dv, reps * t), F32)]
                           + [pltpu.VMEM((8 if front else tk, reps * t), F32)] * 2
                           + [pltpu.VMEM((1, reps * t), F32)] * 2,
            input_output_aliases=aliases,
            compiler_params=_cparams(3),
            name=name + ("_front" if front else ""),
        )(*args)
    return outs


SWA_Q_COLS = SWA_HEADS * SWA_HEAD_DIM
SWA_KV_COLS = SWA_KV_HEADS * SWA_HEAD_DIM
SWA_IN_COLS = SWA_Q_COLS + 2 * SWA_KV_COLS


def _swa_proj_kernel(h_ref, g_ref, w_ref, b_ref, ca_ref, sa_ref, q_ref, k_ref, v_ref):
    xn = _rms(h_ref[0], g_ref[...], EPS).astype(BF16)
    ca, sa = ca_ref[...], sa_ref[...]
    half = PART_ROT // 2
    nq = SWA_Q_COLS
    nk = SWA_KV_COLS
    q = _rope(_dot(xn, w_ref[:, 0:nq]) + b_ref[:, 0:nq], ca, sa, half)
    q_ref[0] = (q * (SWA_HEAD_DIM ** -0.5 * LOG2E)).astype(BF16)
    k_ref[0] = _rope(_dot(xn, w_ref[:, nq:nq + nk]) + b_ref[:, nq:nq + nk], ca, sa, half).astype(BF16)
    v_ref[0] = (_dot(xn, w_ref[:, nq + nk:]) + b_ref[:, nq + nk:]).astype(BF16)


def _swa_proj(h3, g, w, bias, tabs):
    b, l, _ = h3.shape
    tm = PROJ_TM
    row = lambda wd: pl.BlockSpec((1, tm, wd), lambda j, bb: (bb, j, 0))
    tab = pl.BlockSpec((tm, LANES), lambda j, bb: (j, 0))
    outw = (SWA_Q_COLS, SWA_KV_COLS, SWA_KV_COLS)
    return pl.pallas_call(
        _swa_proj_kernel,
        grid=(l // tm, b),
        in_specs=[row(D_MODEL), _resident((1, D_MODEL)), _resident((D_MODEL, SWA_IN_COLS)),
                  _resident((1, SWA_IN_COLS)), tab, tab],
        out_specs=[row(w_) for w_ in outw],
        out_shape=[jax.ShapeDtypeStruct((b, l, w_), BF16) for w_ in outw],
        compiler_params=_cparams(2),
        name="swa_proj",
    )(h3, g, w, bias, *tabs)


def _swa_kernel(sinks_ref, q_ref, k_ref, v_ref, *rest, tq, seq, front):
    if front:
        _, o_ref, s_a, s_b = rest
    else:
        o_ref, s_a, s_b = rest
    hd_dim = SWA_HEAD_DIM
    ts = min(SWA_TS, tq)
    pairs = SWA_GROUP // 2
    cw = 2 * ts
    f0 = 0 if front else seq
    meta_k = k_ref[0, f0 + N_FRONT:f0 + BLOCK, :]
    meta_v = v_ref[0, f0 + N_FRONT:f0 + BLOCK, :]
    band = 0 if front else ts + WINDOW
    nk = band + N_META
    nkp = -(-nk // LANES) * LANES
    qt = _transpose(q_ref[0])
    zh = jnp.zeros((hd_dim, cw), BF16)
    key = lax.broadcasted_iota(jnp.int32, (nk, ts), 0)
    qry = lax.broadcasted_iota(jnp.int32, (nk, ts), 1)
    first_half = lax.broadcasted_iota(jnp.int32, (1, cw), 1) < ts

    def keys_of(u):
        if front:
            kc, vc = meta_k, meta_v
            allowed = key + N_FRONT <= qry
        else:
            start = pl.program_id(1) * tq + u * ts
            bs = pl.multiple_of(jnp.maximum(start - WINDOW, 0), BLOCK)
            kc = jnp.concatenate([k_ref[0, pl.ds(bs, band), :], meta_k], axis=0)
            vc = jnp.concatenate([v_ref[0, pl.ds(bs, band), :], meta_v], axis=0)
            dist = (start + qry) - (bs + key)
            allowed = (key >= band) | ((dist >= 0) & (dist < WINDOW))
        bias = jnp.where(allowed, 0.0, NEG_INF)
        vt = _transpose(jnp.concatenate([vc, jnp.zeros((nkp - nk, vc.shape[1]), vc.dtype)], axis=0))
        return kc, vt, jnp.concatenate([bias, bias], axis=1)

    sets = [(u, g) for u in range(tq // ts) for g in range(SWA_KV_HEADS)]
    operands = {}
    sbuf = (s_a, s_b)

    def produce(n, pp):
        u, g = sets[n]
        if u not in operands:
            operands[u] = keys_of(u)
        hd = g * SWA_GROUP + 2 * pp
        qh = jnp.concatenate([qt[(hd + a) * hd_dim:(hd + a + 1) * hd_dim, u * ts:(u + 1) * ts] for a in (0, 1)],
                             axis=1)
        rhs = jnp.concatenate([qh, zh] if g == 0 else [zh, qh], axis=0)
        sbuf[n % 2][:, pp * cw:(pp + 1) * cw] = _dot(operands[u][0], rhs)

    outs = {}

    def consume(n, pp):
        u, g = sets[n]
        _, vt, bias = operands[u]
        hd = g * SWA_GROUP + 2 * pp
        sink = jnp.where(first_half, sinks_ref[hd] * LOG2E, sinks_ref[hd + 1] * LOG2E)
        s = sbuf[n % 2][:, pp * cw:(pp + 1) * cw] + bias
        m = jnp.maximum(jnp.max(s, axis=0, keepdims=True), sink)
        e = jnp.exp2(s - m)
        den = jnp.sum(e, axis=0, keepdims=True) + jnp.exp2(sink - m)
        eb = jnp.concatenate([e.astype(BF16), jnp.zeros((nkp - nk, cw), BF16)], axis=0)
        o = _dot(vt, eb)[g * hd_dim:(g + 1) * hd_dim] / den
        outs[hd, u] = o[:, :ts]
        outs[hd + 1, u] = o[:, ts:]

    for pp in range(pairs):
        produce(0, pp)
    for n in range(1, len(sets)):
        for pp in range(pairs):
            produce(n, pp)
            consume(n - 1, pp)
    for pp in range(pairs):
        consume(len(sets) - 1, pp)
    rows = [jnp.concatenate([outs[hd, u] for u in range(tq // ts)], axis=1) for hd in range(SWA_HEADS)]
    o_ref[0] = jnp.concatenate(rows, axis=0).T.astype(BF16)


def _swa_attention(q, k, v, sinks, *, seq):
    b, l, _ = q.shape
    nq = SWA_Q_COLS
    outs = None
    for front in (False, True):
        tq = BLOCK if front else SWA_TQ
        ts = min(SWA_TS, tq)
        nk = N_META if front else ts + WINDOW + N_META
        if front:
            grid = (b, 1)
            qmap = lambda bb, i: (bb, seq // BLOCK, 0)
            kvl, kvmap = BLOCK, qmap
        else:
            grid = (b, seq // tq)
            qmap = lambda bb, i: (bb, i, 0)
            kvl, kvmap = l, lambda bb, i: (bb, 0, 0)
        in_specs = [pl.BlockSpec(memory_space=pltpu.SMEM), pl.BlockSpec((1, tq, nq), qmap),
                    pl.BlockSpec((1, kvl, k.shape[2]), kvmap), pl.BlockSpec((1, kvl, v.shape[2]), kvmap)]
        args = [sinks, q, k, v]
        aliases = {}
        if front:
            in_specs.append(pl.BlockSpec(memory_space=pl.ANY))
            aliases = {len(args): 0}
            args.append(outs)
        outs = pl.pallas_call(
            functools.partial(_swa_kernel, tq=tq, seq=seq, front=front),
            grid=grid,
            in_specs=in_specs,
            out_specs=pl.BlockSpec((1, tq, nq), qmap),
            out_shape=jax.ShapeDtypeStruct((b, l, nq), BF16),
            scratch_shapes=[pltpu.VMEM((nk, SWA_GROUP * ts), F32)] * 2,
            input_output_aliases=aliases,
            compiler_params=_cparams(2),
            name="swa_attn" + ("_front" if front else ""),
        )(*args)
    return outs


def _ab_weights(w_in, w_uq, w_ukv):
    win = jnp.pad(w_in, ((0, 0), (0, AB_IN_PAD - w_in.shape[1]))).astype(BF16)
    wq = w_uq.reshape(MLA_Q_RANK, MLA_HEADS, MLA_NOPE + MLA_ROPE)
    wq = jnp.pad(wq, ((0, 0), (0, 0), (0, 2 * LANES - MLA_NOPE - MLA_ROPE))).reshape(MLA_Q_RANK, MLA_HEADS * 2 * LANES)
    wkv = w_ukv.reshape(MLA_KV_RANK, MLA_HEADS, 2, MLA_NOPE).transpose(0, 2, 1, 3).reshape(MLA_KV_RANK, -1)
    return win, wq.astype(BF16), wkv.astype(BF16)


def kernel(x, meta_tokens, ffn1_norm, ffn1_w_gate, ffn1_w_up, ffn1_w_down, mix_norm, ab_w_in, diff_lambda_q1, diff_lambda_k1, diff_lambda_q2, diff_lambda_k2, diff_subln, mla_q_norm, mla_w_uq, mla_kv_norm, mla_w_ukv, ab_w_out, swa_w_qkv, swa_b_qkv, swa_sinks, swa_w_out, swa_b_out, ffn2_norm, ffn2_w_gate, ffn2_w_up, ffn2_w_down, final_norm):
    b, seq, d = x.shape
    depth = ffn1_norm.shape[0]
    l = seq + BLOCK
    n = b * l
    front = jnp.concatenate([jnp.zeros((N_FRONT, d), x.dtype), meta_tokens.astype(x.dtype)], axis=0)

    pos = np.concatenate([np.arange(seq) + N_META, np.maximum(np.arange(BLOCK) - N_FRONT, 0)])
    tabs_p = _rope_tables(pos, PART_ROT)
    tabs_m = _rope_tables(pos, MLA_ROPE)
    row2 = lambda a: a.reshape(1, -1)

    ffn1 = (ffn1_w_gate, ffn1_w_up, ffn1_w_down.astype(BF16))
    ffn2 = (ffn2_w_gate, ffn2_w_up, ffn2_w_down.astype(BF16))
    out = None
    for ly in range(depth):
        w1 = (row2(ffn1_norm[ly]), *ffn1)
        h = _ffn_first(x, front, *w1).reshape(n, d) if ly == 0 else _ffn(h, *w1, ly)
        h3 = h.reshape(b, l, d)
        if ly % 2 == 0:
            e = ly // 2
            lambda_init = 0.8 - 0.6 * math.exp(-0.3 * ly)
            win, wuq, wukv = _ab_weights(ab_w_in[e], mla_w_uq[e], mla_w_ukv[e])
            qa, ka, va, qb, kb, vb = _ab_proj(h3, row2(mix_norm[ly]), win, row2(mla_q_norm[e]), wuq,
                                              row2(mla_kv_norm[e]), wukv, tabs_p + tabs_m)
            extra = [row2(diff_lambda_q1[e]), row2(diff_lambda_k1[e]), row2(diff_lambda_q2[e]),
                     row2(diff_lambda_k2[e]), diff_subln[e].reshape(-1, 1)]
            oa = _causal_attention(functools.partial(_diff_kernel, lambda_init=lambda_init), qa, ka, va, extra,
                                   heads=DIFF_HEADS, dk=LANES, reps=2, tq=ATT_TQ, tk=ATT_TK, seq=seq,
                                   name="diff_attn")
            ob = _causal_attention(_mla_kernel, qb, kb, vb, [], heads=MLA_HEADS, dk=2 * LANES, reps=1,
                                   tq=ATT_TQ, tk=ATT_TK, seq=seq, name="mla_attn")
            wo = ab_w_out[e].astype(BF16)
            mix, bias = [(oa, wo[:512]), (ob, wo[512:])], None
        else:
            o = ly // 2
            q, k, v = _swa_proj(h3, row2(mix_norm[ly]), swa_w_qkv[o].astype(BF16), row2(swa_b_qkv[o]), tabs_p)
            att = _swa_attention(q, k, v, swa_sinks[o], seq=seq)
            mix, bias = [(att, swa_w_out[o].astype(BF16))], row2(swa_b_out[o])
        w2 = (row2(ffn2_norm[ly]), *ffn2)
        if ly == depth - 1:
            out = _ffn_final(h3, *w2, ly, row2(final_norm), seq, mix=mix, bias=bias)
        else:
            h = _ffn(h, *w2, ly, mix=[(a.reshape(n, -1), w) for a, w in mix], bias=bias)
    return out
```

```python
import functools
import math

import numpy as np
import jax
import jax.numpy as jnp
from jax import lax
from jax.experimental import pallas as pl
from jax.experimental.pallas import tpu as pltpu

F32 = jnp.float32
BF16 = jnp.bfloat16

D_MODEL = 1024
N_META = 16
BLOCK = 128
N_FRONT = BLOCK - N_META
ROPE_THETA = 500000.0
EPS = 1e-6
NEG_INF = -1e30
D_FF = 2816

DIFF_HEADS = 4
DIFF_HEAD_DIM = 64
DIFF_V_DIM = 2 * DIFF_HEAD_DIM
DIFF_EPS = 1e-5
MLA_HEADS = 4
MLA_NOPE = 128
MLA_ROPE = 64
MLA_V = 128
MLA_Q_RANK = 256
MLA_KV_RANK = 256
SWA_HEADS = 16
SWA_KV_HEADS = 2
SWA_GROUP = SWA_HEADS // SWA_KV_HEADS
SWA_HEAD_DIM = 64
WINDOW = 128

PART_ROT = DIFF_HEAD_DIM // 4
LANES = 128
VMEM_LIMIT = 56 * 1024 * 1024

FFN_TM = 640
FFN_FC = 256
PROJ_TM = 640
FFN_FINAL_TM = 512
ATT_TK = 512
ATT_TQ = 2 * ATT_TK
ATT_CW = 256
MLA_HEADS_PER_STEP = 2
ATT_GROUP = 2
VT_FILL_UNROLL = 5
LOG2E = 1.4426950408889634
SWA_TQ = 1024
SWA_TS = 128


def _cparams(n_axes):
    return pltpu.CompilerParams(dimension_semantics=("arbitrary",) * n_axes,
                                vmem_limit_bytes=VMEM_LIMIT)


def _resident(shape):
    nd = len(shape)
    return pl.BlockSpec(shape, lambda *_: (0,) * nd, pipeline_mode=pl.Buffered(1))


def _rms(x, g, eps):
    return x * lax.rsqrt(jnp.mean(x * x, axis=-1, keepdims=True) + eps) * g


def _dot(a, b):
    return jnp.dot(a, b, preferred_element_type=F32)


def _dot_nt(a, b):
    return lax.dot_general(a, b, (((1,), (1,)), ((), ())), preferred_element_type=F32)


def _ffn_kernel(x_ref, *refs, final=False, aliased=False, n_mix=0, mix_bias=False):
    mix, refs = refs[:2 * n_mix], refs[2 * n_mix:]
    if mix_bias:
        mb_ref, refs = refs[0], refs[1:]
    g_ref, wg_ref, wu_ref, wd_ref, *rest = refs
    if final:
        fg_ref, o_ref, act_ref = rest
    elif aliased:
        _, o_ref, act_ref = rest
    else:
        o_ref, act_ref = rest
    x = x_ref[...].reshape(x_ref.shape[-2:])
    for a_ref, w_ref in zip(mix[0::2], mix[1::2]):
        x = x + _dot(a_ref[...].reshape(a_ref.shape[-2:]), w_ref[...])
    if mix_bias:
        x = x + mb_ref[...]
    xn = _rms(x, g_ref[...], EPS).astype(BF16)
    for c in range(D_FF // FFN_FC):
        sl = slice(c * FFN_FC, (c + 1) * FFN_FC)
        g = _dot(xn, wg_ref[:, sl].astype(BF16))
        u = _dot(xn, wu_ref[:, sl].astype(BF16))
        act_ref[:, sl] = (g * (1.0 / (1.0 + jnp.exp(-g))) * u).astype(BF16)
    y = x + 0.5 * _dot(act_ref[...], wd_ref[...])
    if final:
        y = _rms(y, fg_ref[...], EPS)
    o_ref[...] = y.reshape(o_ref.shape)


def _mix_args(mix, bias, row):
    args, specs = [], []
    for a, w in mix:
        args += [a, w]
        specs += [row(a.shape[-1]), _resident(w.shape)]
    if bias is not None:
        args.append(bias)
        specs.append(_resident(bias.shape))
    return args, specs, dict(n_mix=len(mix), mix_bias=bias is not None)


def _ffn_wspecs(ly):
    mat = lambda r, c: pl.BlockSpec((None, r, c), lambda *_: (ly, 0, 0), pipeline_mode=pl.Buffered(1))
    return [_resident((1, D_MODEL)), mat(D_MODEL, D_FF), mat(D_MODEL, D_FF), mat(D_FF, D_MODEL)]


def _ffn(h, g, wg, wu, wd, ly, mix=(), bias=None):
    n = h.shape[0]
    tm = FFN_TM
    assert n % tm == 0
    row = lambda w: pl.BlockSpec((tm, w), lambda i: (i, 0))
    margs, mspecs, mkw = _mix_args(mix, bias, row)
    return pl.pallas_call(
        functools.partial(_ffn_kernel, **mkw),
        grid=(n // tm,),
        in_specs=[row(D_MODEL)] + mspecs + _ffn_wspecs(ly),
        out_specs=row(D_MODEL),
        out_shape=jax.ShapeDtypeStruct((n, D_MODEL), F32),
        scratch_shapes=[pltpu.VMEM((tm, D_FF), BF16)],
        compiler_params=_cparams(1),
        name="ffn",
    )(h, *margs, g, wg, wu, wd)


def _ffn_first(x, front, g, wg, wu, wd):
    b, seq, d = x.shape
    tm = FFN_FINAL_TM
    assert seq % tm == 0
    weights = _ffn_wspecs(0)
    row = pl.BlockSpec((1, tm, d), lambda bb, i: (bb, i, 0))
    h3 = pl.pallas_call(
        _ffn_kernel,
        grid=(b, seq // tm),
        in_specs=[row] + weights,
        out_specs=row,
        out_shape=jax.ShapeDtypeStruct((b, seq + BLOCK, d), F32),
        scratch_shapes=[pltpu.VMEM((tm, D_FF), BF16)],
        compiler_params=_cparams(2),
        name="ffn_first",
    )(x, g, wg, wu, wd)
    return pl.pallas_call(
        functools.partial(_ffn_kernel, aliased=True),
        grid=(b,),
        in_specs=[pl.BlockSpec((BLOCK, d), lambda bb: (0, 0))] + weights + [pl.BlockSpec(memory_space=pl.ANY)],
        out_specs=pl.BlockSpec((1, BLOCK, d), lambda bb: (bb, seq // BLOCK, 0)),
        out_shape=jax.ShapeDtypeStruct(h3.shape, F32),
        scratch_shapes=[pltpu.VMEM((BLOCK, D_FF), BF16)],
        input_output_aliases={5: 0},
        compiler_params=_cparams(1),
        name="ffn_first_front",
    )(front, g, wg, wu, wd, h3)


def _ffn_final(h3, g, wg, wu, wd, ly, fg, seq, mix=(), bias=None):
    b = h3.shape[0]
    tm = FFN_FINAL_TM
    assert seq % tm == 0
    row = lambda w: pl.BlockSpec((1, tm, w), lambda bb, i: (bb, i, 0))
    margs, mspecs, mkw = _mix_args(mix, bias, row)
    return pl.pallas_call(
        functools.partial(_ffn_kernel, final=True, **mkw),
        grid=(b, seq // tm),
        in_specs=[row(D_MODEL)] + mspecs + _ffn_wspecs(ly) + [_resident((1, D_MODEL))],
        out_specs=row(D_MODEL),
        out_shape=jax.ShapeDtypeStruct((b, seq, D_MODEL), F32),
        scratch_shapes=[pltpu.VMEM((tm, D_FF), BF16)],
        compiler_params=_cparams(2),
        name="ffn_final",
    )(h3, *margs, g, wg, wu, wd, fg)


def _rope(y, c, s, half):
    w = y.shape[1]
    reps = w // LANES
    if reps > 1:
        c = jnp.concatenate([c] * reps, axis=1)
        s = jnp.concatenate([s] * reps, axis=1)
    lane = lax.broadcasted_iota(jnp.int32, y.shape, 1)
    first = (lane & 63) < half
    partner = jnp.where(first, pltpu.roll(y, w - half, 1), pltpu.roll(y, half, 1))
    return y * c + partner * s


def _rope_tables(pos, rot_dim):
    half = rot_dim // 2
    g = np.arange(LANES) & 63
    inv = np.power(np.float32(ROPE_THETA), -(2 * (g % half)).astype(np.float32) / np.float32(rot_dim))
    ang = pos.astype(np.float32)[:, None] * inv[None, :].astype(np.float32)
    rotated = (g < rot_dim)[None, :]
    cos = np.where(rotated, np.cos(ang), np.float32(1.0)).astype(np.float32)
    sin = np.where(rotated, np.where((g < half)[None, :], -np.sin(ang), np.sin(ang)), np.float32(0.0))
    return jnp.asarray(cos), jnp.asarray(sin.astype(np.float32))


AB_IN_PAD = 3 * 512 + MLA_Q_RANK + MLA_KV_RANK + LANES


def _ab_proj_kernel(h_ref, g_ref, win_ref, qn_ref, wuq_ref, kvn_ref, wukv_ref, ca_ref, sa_ref, cm_ref, sm_ref,
                    qa_ref, ka_ref, va_ref, qb_ref, kb_ref, vb_ref):
    xn = _rms(h_ref[0], g_ref[...], EPS).astype(BF16)
    ca, sa, cm, sm = ca_ref[...], sa_ref[...], cm_ref[...], sm_ref[...]
    half_a = PART_ROT // 2
    half_m = MLA_ROPE // 2
    qa = _rope(_dot(xn, win_ref[:, 0:512]), ca, sa, half_a)
    qa_ref[0] = (qa * (DIFF_HEAD_DIM ** -0.5 * LOG2E)).astype(BF16)
    ka_ref[0] = _rope(_dot(xn, win_ref[:, 512:1024]), ca, sa, half_a).astype(BF16)
    va_ref[0] = _dot(xn, win_ref[:, 1024:1536]).astype(BF16)
    cq = _dot(xn, win_ref[:, 1536:1792])
    ckv = _dot(xn, win_ref[:, 1792:2048])
    kr = _rope(_dot(xn, win_ref[:, 2048:2176]), cm, sm, half_m).astype(BF16)
    qb = _dot(_rms(cq, qn_ref[...], EPS).astype(BF16), wuq_ref[...])
    scale = (MLA_NOPE + MLA_ROPE) ** -0.5 * LOG2E
    for hh in range(MLA_HEADS):
        o = 2 * LANES * hh
        qb_ref[0, :, o:o + LANES] = (qb[:, o:o + LANES] * scale).astype(BF16)
        qb_ref[0, :, o + LANES:o + 2 * LANES] = (_rope(qb[:, o + LANES:o + 2 * LANES], cm, sm, half_m) * scale).astype(BF16)
    kv = _dot(_rms(ckv, kvn_ref[...], EPS).astype(BF16), wukv_ref[...])
    for hh in range(MLA_HEADS):
        o = 2 * LANES * hh
        kb_ref[0, :, o:o + LANES] = kv[:, LANES * hh:LANES * (hh + 1)].astype(BF16)
        kb_ref[0, :, o + LANES:o + 2 * LANES] = kr
    vb_ref[0] = kv[:, MLA_HEADS * MLA_NOPE:].astype(BF16)


def _ab_proj(h3, g, win, qn, wuq, kvn, wukv, tabs):
    b, l, _ = h3.shape
    tm = PROJ_TM
    assert l % tm == 0
    row = lambda w: pl.BlockSpec((1, tm, w), lambda j, bb: (bb, j, 0))
    tab = pl.BlockSpec((tm, LANES), lambda j, bb: (j, 0))
    outw = (512, 512, 512, 1024, 1024, 512)
    return pl.pallas_call(
        _ab_proj_kernel,
        grid=(l // tm, b),
        in_specs=[row(D_MODEL), _resident((1, D_MODEL)), _resident((D_MODEL, AB_IN_PAD)),
                  _resident((1, MLA_Q_RANK)), _resident((MLA_Q_RANK, 1024)),
                  _resident((1, MLA_KV_RANK)), _resident((MLA_KV_RANK, 1024)), tab, tab, tab, tab],
        out_specs=[row(w) for w in outw],
        out_shape=[jax.ShapeDtypeStruct((b, l, w), BF16) for w in outw],
        compiler_params=_cparams(2),
        name="ab_proj",
    )(h3, g, win, qn, wuq, kvn, wukv, *tabs)


def _transpose(x):
    return x.T


def _head_slices(c0, heads):
    cph, dk, dv = heads
    h = c0 // cph
    return slice(h * dk, (h + 1) * dk), slice(h * dv, (h + 1) * dv)


def _flash_step(k, vt, qt, m_ref, l_ref, acc_ref, mask, heads, between=None):
    cw = min(ATT_CW, heads[0])
    probs = {}
    for c0 in range(0, qt.shape[1], cw):
        cs = slice(c0, c0 + cw)
        ks, _ = _head_slices(c0, heads)
        s = _dot(k[:, ks], qt[:, cs])
        s = jnp.where(mask(c0, s.shape), s, NEG_INF)
        m_new = jnp.max(s, axis=0, keepdims=True)
        p = jnp.exp2(s - m_new)
        l_ref[:, cs] = jnp.sum(p, axis=0, keepdims=True)
        m_ref[:, cs] = m_new
        probs[c0] = p.astype(BF16)
    if between is not None:
        between()
    for c0, p in probs.items():
        acc_ref[:, c0:c0 + cw] = _dot(vt[_head_slices(c0, heads)[1]], p)


def _produce(k, qt, s_ref, x_ref, chunks, heads):
    for c0 in chunks:
        cs = slice(c0, c0 + ATT_CW)
        s = _dot(k[:, _head_slices(c0, heads)[0]], qt[:, cs])
        s_ref[:, cs] = s
        x_ref[:, cs] = jnp.max(s, axis=0, keepdims=True)


def _consume(s_ref, x_ref, vt, m_ref, l_ref, acc_ref, chunks, heads, diag=None):
    tk = s_ref.shape[0]
    for c0 in chunks:
        cs = slice(c0, c0 + ATT_CW)
        if diag is None:
            seen = tk
        else:
            key0, tq, tri = diag
            seen = min((c0 & (tq - 1)) - key0, tk)
            if seen + ATT_CW <= 0:
                continue
        if seen == tk:
            s = s_ref[:, cs]
            smax = x_ref[:, cs]
        else:
            s = s_ref[seen:seen + ATT_CW, cs] + tri
            if seen > 0:
                s = jnp.concatenate([s_ref[0:seen, cs], s], axis=0)
            smax = jnp.max(s, axis=0, keepdims=True)
        m_prev = m_ref[:, cs]
        m_new = jnp.maximum(m_prev, smax)
        alpha = jnp.exp2(m_prev - m_new)
        p = jnp.exp2(s - m_new)
        l_ref[:, cs] = alpha * l_ref[:, cs] + jnp.sum(p, axis=0, keepdims=True)
        acc_ref[:, cs] = alpha * acc_ref[:, cs] + _dot(vt[_head_slices(c0, heads)[1], :s.shape[0]], p.astype(BF16))
        m_ref[:, cs] = m_new


def _causal_attend(make_qt, k_ref, v_ref, scr, *, tq, tk, seq, front, nh):
    vt_ref, m_ref, l_ref, acc_ref, s0, s1, x0, x1 = scr
    state = (m_ref, l_ref, acc_ref)
    heads = (m_ref.shape[1] // nh, k_ref.shape[2] // nh, v_ref.shape[2] // nh)
    f0 = 0 if front else seq
    kf = k_ref[0, f0:f0 + BLOCK, :]

    def valid(c0, shape):
        return lax.broadcasted_iota(jnp.int32, shape, 0) >= N_FRONT

    def causal(key0):
        def mask(c0, shape):
            key = lax.broadcasted_iota(jnp.int32, shape, 0) + key0
            qry = (lax.broadcasted_iota(jnp.int32, shape, 1) + c0) & (tq - 1)
            return key <= qry
        return mask

    if front:
        qt = make_qt()
        both = lambda c0, shape: valid(c0, shape) & causal(0)(c0, shape)
        _flash_step(kf, _transpose(v_ref[0, f0:f0 + BLOCK, :]), qt, *state, both, heads)
        return
    assert tq == 2 * tk
    i = pl.program_id(2)
    nblk = v_ref.shape[1] // BLOCK
    sub = tk // BLOCK
    chunks = list(range(0, m_ref.shape[1], ATT_CW))
    late = [c0 for c0 in chunks if (c0 & (tq - 1)) + ATT_CW > tk]

    @pl.when(i == 0)
    def _fill():
        def fill(c, carry):
            st = pl.multiple_of(c * BLOCK, BLOCK)
            vt_ref[c] = _transpose(v_ref[0, pl.ds(st, BLOCK), :])
            return carry
        lax.fori_loop(0, nblk, fill, 0, unroll=VT_FILL_UNROLL)

    qt = make_qt()

    def k_block(j):
        return k_ref[0, pl.ds(pl.multiple_of(j * tk, tk), tk), :]

    def vt_block(j):
        return jnp.concatenate([vt_ref[j * sub + u] for u in range(sub)], axis=1)

    def step(k_next, s_next, x_next, s_cur, x_cur, vt_cur, produced=chunks, diag=None):
        for g0 in range(0, len(chunks), ATT_GROUP):
            grp = chunks[g0:g0 + ATT_GROUP]
            _produce(k_next, qt, s_next, x_next, [c0 for c0 in grp if c0 in produced], heads)
            _consume(s_cur, x_cur, vt_cur, *state, grp, heads, diag=diag)

    _flash_step(kf, vt_ref[seq // BLOCK], qt, *state, valid, heads,
                between=lambda: _produce(k_block(0), qt, s0, x0, chunks, heads))

    def body(p, carry):
        step(k_block(2 * p + 1), s1, x1, s0, x0, vt_block(2 * p))
        step(k_block(2 * p + 2), s0, x0, s1, x1, vt_block(2 * p + 1))
        return carry

    lax.fori_loop(0, i, body, 0)
    tri_shape = (ATT_CW, ATT_CW)
    tri = jnp.where(lax.broadcasted_iota(jnp.int32, tri_shape, 0) <= lax.broadcasted_iota(jnp.int32, tri_shape, 1),
                    0.0, NEG_INF)
    step(k_block(2 * i + 1), s1, x1, s0, x0, vt_block(2 * i), produced=late, diag=(0, tq, tri))
    _consume(s1, x1, vt_block(2 * i + 1), *state, chunks, heads, diag=(tk, tq, tri))


def _diff_kernel(q_ref, k_ref, v_ref, lq1_ref, lk1_ref, lq2_ref, lk2_ref, sub_ref, *rest,
                 tq, tk, seq, front, lambda_init):
    if front:
        _, o_ref, *scr = rest
    else:
        o_ref, *scr = rest
    m_ref, l_ref, acc_ref = scr[1:4]
    def stacked_queries():
        qt = _transpose(q_ref[0])
        d = lax.broadcasted_iota(jnp.int32, qt.shape, 0)
        zero = jnp.zeros_like(qt)
        return jnp.concatenate([jnp.where(d < DIFF_HEAD_DIM, qt, zero), jnp.where(d >= DIFF_HEAD_DIM, qt, zero)],
                               axis=1)

    _causal_attend(stacked_queries, k_ref, v_ref, scr, tq=tq, tk=tk, seq=seq, front=front, nh=1)
    o = acc_ref[...] / l_ref[...]
    lam = (jnp.exp(jnp.sum(lq1_ref[...] * lk1_ref[...], keepdims=True))
           - jnp.exp(jnp.sum(lq2_ref[...] * lk2_ref[...], keepdims=True)) + lambda_init)
    w = o[:, :tq] - lam * o[:, tq:]
    w = w * lax.rsqrt(jnp.mean(w * w, axis=0, keepdims=True) + DIFF_EPS) * sub_ref[...]
    o_ref[0] = (w * (1.0 - lambda_init)).T.astype(BF16)


def _mla_kernel(q_ref, k_ref, v_ref, *rest, tq, tk, seq, front, nh):
    if front:
        _, o_ref, *scr = rest
    else:
        o_ref, *scr = rest
    m_ref, l_ref, acc_ref = scr[1:4]
    dk = q_ref.shape[2] // nh

    def queries():
        t = _transpose(q_ref[0])
        return jnp.concatenate([t[h * dk:(h + 1) * dk] for h in range(nh)], axis=1)

    _causal_attend(queries, k_ref, v_ref, scr, tq=tq, tk=tk, seq=seq, front=front, nh=nh)
    o = acc_ref[...] / l_ref[...]
    for h in range(nh):
        o_ref[0, :, h * LANES:(h + 1) * LANES] = o[:, h * tq:(h + 1) * tq].T.astype(BF16)


def _causal_attention(kernel, q, k, v, extra, *, heads, dk, reps, tq, tk, seq, name, nh=1):
    b, l, _ = q.shape
    dv = LANES * nh
    dk = dk * nh
    reps = reps * nh
    heads = heads // nh
    outs = None
    for front in (False, True):
        t = BLOCK if front else tq
        if front:
            grid = (b, heads, 1)
            qmap = lambda bb, hh, i: (bb, seq // BLOCK, hh)
        else:
            grid = (b, heads, seq // t)
            qmap = lambda bb, hh, i: (bb, i, hh)
        if front:
            kvl, kvmap = BLOCK, qmap
        else:
            kvl, kvmap = l, lambda bb, hh, i: (bb, 0, hh)
        in_specs = [pl.BlockSpec((1, t, dk), qmap), pl.BlockSpec((1, kvl, dk), kvmap),
                    pl.BlockSpec((1, kvl, dv), kvmap)]
        in_specs += [_resident(e.shape) for e in extra]
        args = [q, k, v, *extra]
        aliases = {}
        if front:
            in_specs.append(pl.BlockSpec(memory_space=pl.ANY))
            aliases = {len(args): 0}
            args.append(outs)
        outs = pl.pallas_call(
            functools.partial(kernel, tq=t, tk=min(t, tk), seq=seq, front=front),
            grid=grid,
            in_specs=in_specs,
            out_specs=pl.BlockSpec((1, t, dv), qmap),
            out_shape=jax.ShapeDtypeStruct((b, l, heads * dv), BF16),
            scratch_shapes=[pltpu.VMEM((1 if front else l // BLOCK, dv, BLOCK), BF16),
                            pltpu.VMEM((1, reps * t), F32), pltpu.VMEM((1, reps * t), F32),
                            pltpu.VMEM((LANES, reps * t), F32)]
                           + [pltpu.VMEM((8 if front else tk, reps * t), F32)] * 2
                           + [pltpu.VMEM((1, reps * t), F32)] * 2,
            input_output_aliases=aliases,
            compiler_params=_cparams(3),
            name=name + ("_front" if front else ""),
        )(*args)
    return outs


SWA_Q_COLS = SWA_HEADS * SWA_HEAD_DIM
SWA_KV_COLS = SWA_KV_HEADS * SWA_HEAD_DIM
SWA_IN_COLS = SWA_Q_COLS + 2 * SWA_KV_COLS


def _swa_proj_kernel(h_ref, g_ref, w_ref, b_ref, ca_ref, sa_ref, q_ref, k_ref, v_ref):
    xn = _rms(h_ref[0], g_ref[...], EPS).astype(BF16)
    ca, sa = ca_ref[...], sa_ref[...]
    half = PART_ROT // 2
    nq = SWA_Q_COLS
    nk = SWA_KV_COLS
    q = _rope(_dot(xn, w_ref[:, 0:nq]) + b_ref[:, 0:nq], ca, sa, half)
    q_ref[0] = (q * (SWA_HEAD_DIM ** -0.5 * LOG2E)).astype(BF16)
    k_ref[0] = _rope(_dot(xn, w_ref[:, nq:nq + nk]) + b_ref[:, nq:nq + nk], ca, sa, half).astype(BF16)
    v_ref[0] = (_dot(xn, w_ref[:, nq + nk:]) + b_ref[:, nq + nk:]).astype(BF16)


def _swa_proj(h3, g, w, bias, tabs):
    b, l, _ = h3.shape
    tm = PROJ_TM
    row = lambda wd: pl.BlockSpec((1, tm, wd), lambda j, bb: (bb, j, 0))
    tab = pl.BlockSpec((tm, LANES), lambda j, bb: (j, 0))
    outw = (SWA_Q_COLS, SWA_KV_COLS, SWA_KV_COLS)
    return pl.pallas_call(
        _swa_proj_kernel,
        grid=(l // tm, b),
        in_specs=[row(D_MODEL), _resident((1, D_MODEL)), _resident((D_MODEL, SWA_IN_COLS)),
                  _resident((1, SWA_IN_COLS)), tab, tab],
        out_specs=[row(w_) for w_ in outw],
        out_shape=[jax.ShapeDtypeStruct((b, l, w_), BF16) for w_ in outw],
        compiler_params=_cparams(2),
        name="swa_proj",
    )(h3, g, w, bias, *tabs)


def _swa_kernel(sinks_ref, q_ref, k_ref, v_ref, *rest, tq, seq, front):
    if front:
        _, o_ref, s_a, s_b = rest
    else:
        o_ref, s_a, s_b = rest
    hd_dim = SWA_HEAD_DIM
    ts = min(SWA_TS, tq)
    pairs = SWA_GROUP // 2
    cw = 2 * ts
    f0 = 0 if front else seq
    meta_k = k_ref[0, f0 + N_FRONT:f0 + BLOCK, :]
    meta_v = v_ref[0, f0 + N_FRONT:f0 + BLOCK, :]
    band = 0 if front else ts + WINDOW
    nk = band + N_META
    nkp = -(-nk // LANES) * LANES
    qt = _transpose(q_ref[0])
    zh = jnp.zeros((hd_dim, cw), BF16)
    key = lax.broadcasted_iota(jnp.int32, (nk, ts), 0)
    qry = lax.broadcasted_iota(jnp.int32, (nk, ts), 1)
    first_half = lax.broadcasted_iota(jnp.int32, (1, cw), 1) < ts

    def keys_of(u):
        if front:
            kc, vc = meta_k, meta_v
            allowed = key + N_FRONT <= qry
        else:
            start = pl.program_id(1) * tq + u * ts
            bs = pl.multiple_of(jnp.maximum(start - WINDOW, 0), BLOCK)
            kc = jnp.concatenate([k_ref[0, pl.ds(bs, band), :], meta_k], axis=0)
            vc = jnp.concatenate([v_ref[0, pl.ds(bs, band), :], meta_v], axis=0)
            dist = (start + qry) - (bs + key)
            allowed = (key >= band) | ((dist >= 0) & (dist < WINDOW))
        bias = jnp.where(allowed, 0.0, NEG_INF)
        vt = _transpose(jnp.concatenate([vc, jnp.zeros((nkp - nk, vc.shape[1]), vc.dtype)], axis=0))
        return kc, vt, jnp.concatenate([bias, bias], axis=1)

    sets = [(u, g) for u in range(tq // ts) for g in range(SWA_KV_HEADS)]
    operands = {}
    sbuf = (s_a, s_b)

    def produce(n, pp):
        u, g = sets[n]
        if u not in operands:
            operands[u] = keys_of(u)
        hd = g * SWA_GROUP + 2 * pp
        qh = jnp.concatenate([qt[(hd + a) * hd_dim:(hd + a + 1) * hd_dim, u * ts:(u + 1) * ts] for a in (0, 1)],
                             axis=1)
        rhs = jnp.concatenate([qh, zh] if g == 0 else [zh, qh], axis=0)
        sbuf[n % 2][:, pp * cw:(pp + 1) * cw] = _dot(operands[u][0], rhs)

    outs = {}

    def consume(n, pp):
        u, g = sets[n]
        _, vt, bias = operands[u]
        hd = g * SWA_GROUP + 2 * pp
        sink = jnp.where(first_half, sinks_ref[hd] * LOG2E, sinks_ref[hd + 1] * LOG2E)
        s = sbuf[n % 2][:, pp * cw:(pp + 1) * cw] + bias
        m = jnp.maximum(jnp.max(s, axis=0, keepdims=True), sink)
        e = jnp.exp2(s - m)
        den = jnp.sum(e, axis=0, keepdims=True) + jnp.exp2(sink - m)
        eb = jnp.concatenate([e.astype(BF16), jnp.zeros((nkp - nk, cw), BF16)], axis=0)
        o = _dot(vt, eb)[g * hd_dim:(g + 1) * hd_dim] / den
        outs[hd, u] = o[:, :ts]
        outs[hd + 1, u] = o[:, ts:]

    for pp in range(pairs):
        produce(0, pp)
    for n in range(1, len(sets)):
        for pp in range(pairs):
            produce(n, pp)
            consume(n - 1, pp)
    for pp in range(pairs):
        consume(len(sets) - 1, pp)
    rows = [jnp.concatenate([outs[hd, u] for u in range(tq // ts)], axis=1) for hd in range(SWA_HEADS)]
    o_ref[0] = jnp.concatenate(rows, axis=0).T.astype(BF16)


def _swa_attention(q, k, v, sinks, *, seq):
    b, l, _ = q.shape
    nq = SWA_Q_COLS
    outs = None
    for front in (False, True):
        tq = BLOCK if front else SWA_TQ
        ts = min(SWA_TS, tq)
        nk = N_META if front else ts + WINDOW + N_META
        if front:
            grid = (b, 1)
            qmap = lambda bb, i: (bb, seq // BLOCK, 0)
            kvl, kvmap = BLOCK, qmap
        else:
            grid = (b, seq // tq)
            qmap = lambda bb, i: (bb, i, 0)
            kvl, kvmap = l, lambda bb, i: (bb, 0, 0)
        in_specs = [pl.BlockSpec(memory_space=pltpu.SMEM), pl.BlockSpec((1, tq, nq), qmap),
                    pl.BlockSpec((1, kvl, k.shape[2]), kvmap), pl.BlockSpec((1, kvl, v.shape[2]), kvmap)]
        args = [sinks, q, k, v]
        aliases = {}
        if front:
            in_specs.append(pl.BlockSpec(memory_space=pl.ANY))
            aliases = {len(args): 0}
            args.append(outs)
        outs = pl.pallas_call(
            functools.partial(_swa_kernel, tq=tq, seq=seq, front=front),
            grid=grid,
            in_specs=in_specs,
            out_specs=pl.BlockSpec((1, tq, nq), qmap),
            out_shape=jax.ShapeDtypeStruct((b, l, nq), BF16),
            scratch_shapes=[pltpu.VMEM((nk, SWA_GROUP * ts), F32)] * 2,
            input_output_aliases=aliases,
            compiler_params=_cparams(2),
            name="swa_attn" + ("_front" if front else ""),
        )(*args)
    return outs


def _ab_weights(w_in, w_uq, w_ukv):
    win = jnp.pad(w_in, ((0, 0), (0, AB_IN_PAD - w_in.shape[1]))).astype(BF16)
    wq = w_uq.reshape(MLA_Q_RANK, MLA_HEADS, MLA_NOPE + MLA_ROPE)
    wq = jnp.pad(wq, ((0, 0), (0, 0), (0, 2 * LANES - MLA_NOPE - MLA_ROPE))).reshape(MLA_Q_RANK, MLA_HEADS * 2 * LANES)
    wkv = w_ukv.reshape(MLA_KV_RANK, MLA_HEADS, 2, MLA_NOPE).transpose(0, 2, 1, 3).reshape(MLA_KV_RANK, -1)
    return win, wq.astype(BF16), wkv.astype(BF16)


def kernel(x, meta_tokens, ffn1_norm, ffn1_w_gate, ffn1_w_up, ffn1_w_down, mix_norm, ab_w_in, diff_lambda_q1, diff_lambda_k1, diff_lambda_q2, diff_lambda_k2, diff_subln, mla_q_norm, mla_w_uq, mla_kv_norm, mla_w_ukv, ab_w_out, swa_w_qkv, swa_b_qkv, swa_sinks, swa_w_out, swa_b_out, ffn2_norm, ffn2_w_gate, ffn2_w_up, ffn2_w_down, final_norm):
    b, seq, d = x.shape
    depth = ffn1_norm.shape[0]
    l = seq + BLOCK
    n = b * l
    front = jnp.concatenate([jnp.zeros((N_FRONT, d), x.dtype), meta_tokens.astype(x.dtype)], axis=0)

    pos = np.concatenate([np.arange(seq) + N_META, np.maximum(np.arange(BLOCK) - N_FRONT, 0)])
    tabs_p = _rope_tables(pos, PART_ROT)
    tabs_m = _rope_tables(pos, MLA_ROPE)
    row2 = lambda a: a.reshape(1, -1)

    ffn1 = (ffn1_w_gate, ffn1_w_up, ffn1_w_down.astype(BF16))
    ffn2 = (ffn2_w_gate, ffn2_w_up, ffn2_w_down.astype(BF16))
    out = None
    for ly in range(depth):
        w1 = (row2(ffn1_norm[ly]), *ffn1)
        h = _ffn_first(x, front, *w1).reshape(n, d) if ly == 0 else _ffn(h, *w1, ly)
        h3 = h.reshape(b, l, d)
        if ly % 2 == 0:
            e = ly // 2
            lambda_init = 0.8 - 0.6 * math.exp(-0.3 * ly)
            win, wuq, wukv = _ab_weights(ab_w_in[e], mla_w_uq[e], mla_w_ukv[e])
            qa, ka, va, qb, kb, vb = _ab_proj(h3, row2(mix_norm[ly]), win, row2(mla_q_norm[e]), wuq,
                                              row2(mla_kv_norm[e]), wukv, tabs_p + tabs_m)
            extra = [row2(diff_lambda_q1[e]), row2(diff_lambda_k1[e]), row2(diff_lambda_q2[e]),
                     row2(diff_lambda_k2[e]), diff_subln[e].reshape(-1, 1)]
            oa = _causal_attention(functools.partial(_diff_kernel, lambda_init=lambda_init), qa, ka, va, extra,
                                   heads=DIFF_HEADS, dk=LANES, reps=2, tq=ATT_TQ, tk=ATT_TK, seq=seq,
                                   name="diff_attn")
            ob = _causal_attention(functools.partial(_mla_kernel, nh=MLA_HEADS_PER_STEP), qb, kb, vb, [],
                                   heads=MLA_HEADS, dk=2 * LANES, reps=1, tq=ATT_TQ, tk=ATT_TK, seq=seq,
                                   name="mla_attn", nh=MLA_HEADS_PER_STEP)
            wo = ab_w_out[e].astype(BF16)
            mix, bias = [(oa, wo[:512]), (ob, wo[512:])], None
        else:
            o = ly // 2
            q, k, v = _swa_proj(h3, row2(mix_norm[ly]), swa_w_qkv[o].astype(BF16), row2(swa_b_qkv[o]), tabs_p)
            att = _swa_attention(q, k, v, swa_sinks[o], seq=seq)
            mix, bias = [(att, swa_w_out[o].astype(BF16))], row2(swa_b_out[o])
        w2 = (row2(ffn2_norm[ly]), *ffn2)
        if ly == depth - 1:
            out = _ffn_final(h3, *w2, ly, row2(final_norm), seq, mix=mix, bias=bias)
        else:
            h = _ffn(h, *w2, ly, mix=[(a.reshape(n, -1), w) for a, w in mix], bias=bias)
    return out
```

```python
import functools
import math

import numpy as np
import jax
import jax.numpy as jnp
from jax import lax
from jax.experimental import pallas as pl
from jax.experimental.pallas import tpu as pltpu

F32 = jnp.float32
BF16 = jnp.bfloat16

D_MODEL = 1024
N_META = 16
BLOCK = 128
N_FRONT = BLOCK - N_META
ROPE_THETA = 500000.0
EPS = 1e-6
NEG_INF = -1e30
D_FF = 2816

DIFF_HEADS = 4
DIFF_HEAD_DIM = 64
DIFF_V_DIM = 2 * DIFF_HEAD_DIM
DIFF_EPS = 1e-5
MLA_HEADS = 4
MLA_NOPE = 128
MLA_ROPE = 64
MLA_V = 128
MLA_Q_RANK = 256
MLA_KV_RANK = 256
SWA_HEADS = 16
SWA_KV_HEADS = 2
SWA_GROUP = SWA_HEADS // SWA_KV_HEADS
SWA_HEAD_DIM = 64
WINDOW = 128

PART_ROT = DIFF_HEAD_DIM // 4
LANES = 128
VMEM_LIMIT = 56 * 1024 * 1024

FFN_TM = 640
FFN_FC = 256
PROJ_TM = 640
FFN_FINAL_TM = 512
ATT_TK = 512
ATT_TQ = 2 * ATT_TK
ATT_CW = 256
DIFF_HEADS_PER_STEP = 2
MLA_HEADS_PER_STEP = 2
ATT_GROUP = 2
VT_FILL_UNROLL = 5
LOG2E = 1.4426950408889634
SWA_TQ = 1024
SWA_TS = 128


def _cparams(n_axes):
    return pltpu.CompilerParams(dimension_semantics=("arbitrary",) * n_axes,
                                vmem_limit_bytes=VMEM_LIMIT)


def _resident(shape):
    nd = len(shape)
    return pl.BlockSpec(shape, lambda *_: (0,) * nd, pipeline_mode=pl.Buffered(1))


def _rms(x, g, eps):
    return x * lax.rsqrt(jnp.mean(x * x, axis=-1, keepdims=True) + eps) * g


def _dot(a, b):
    return jnp.dot(a, b, preferred_element_type=F32)


def _dot_nt(a, b):
    return lax.dot_general(a, b, (((1,), (1,)), ((), ())), preferred_element_type=F32)


def _ffn_kernel(x_ref, *refs, final=False, aliased=False, n_mix=0, mix_bias=False):
    mix, refs = refs[:2 * n_mix], refs[2 * n_mix:]
    if mix_bias:
        mb_ref, refs = refs[0], refs[1:]
    g_ref, wg_ref, wu_ref, wd_ref, *rest = refs
    if final:
        fg_ref, o_ref, act_ref = rest
    elif aliased:
        _, o_ref, act_ref = rest
    else:
        o_ref, act_ref = rest
    x = x_ref[...].reshape(x_ref.shape[-2:])
    for a_ref, w_ref in zip(mix[0::2], mix[1::2]):
        x = x + _dot(a_ref[...].reshape(a_ref.shape[-2:]), w_ref[...])
    if mix_bias:
        x = x + mb_ref[...]
    xn = _rms(x, g_ref[...], EPS).astype(BF16)
    for c in range(D_FF // FFN_FC):
        sl = slice(c * FFN_FC, (c + 1) * FFN_FC)
        g = _dot(xn, wg_ref[:, sl].astype(BF16))
        u = _dot(xn, wu_ref[:, sl].astype(BF16))
        act_ref[:, sl] = (g * (1.0 / (1.0 + jnp.exp(-g))) * u).astype(BF16)
    y = x + 0.5 * _dot(act_ref[...], wd_ref[...])
    if final:
        y = _rms(y, fg_ref[...], EPS)
    o_ref[...] = y.reshape(o_ref.shape)


def _mix_args(mix, bias, row):
    args, specs = [], []
    for a, w in mix:
        args += [a, w]
        specs += [row(a.shape[-1]), _resident(w.shape)]
    if bias is not None:
        args.append(bias)
        specs.append(_resident(bias.shape))
    return args, specs, dict(n_mix=len(mix), mix_bias=bias is not None)


def _ffn_wspecs(ly):
    mat = lambda r, c: pl.BlockSpec((None, r, c), lambda *_: (ly, 0, 0), pipeline_mode=pl.Buffered(1))
    return [_resident((1, D_MODEL)), mat(D_MODEL, D_FF), mat(D_MODEL, D_FF), mat(D_FF, D_MODEL)]


def _ffn(h, g, wg, wu, wd, ly, mix=(), bias=None):
    n = h.shape[0]
    tm = FFN_TM
    assert n % tm == 0
    row = lambda w: pl.BlockSpec((tm, w), lambda i: (i, 0))
    margs, mspecs, mkw = _mix_args(mix, bias, row)
    return pl.pallas_call(
        functools.partial(_ffn_kernel, **mkw),
        grid=(n // tm,),
        in_specs=[row(D_MODEL)] + mspecs + _ffn_wspecs(ly),
        out_specs=row(D_MODEL),
        out_shape=jax.ShapeDtypeStruct((n, D_MODEL), F32),
        scratch_shapes=[pltpu.VMEM((tm, D_FF), BF16)],
        compiler_params=_cparams(1),
        name="ffn",
    )(h, *margs, g, wg, wu, wd)


def _ffn_first(x, front, g, wg, wu, wd):
    b, seq, d = x.shape
    tm = FFN_FINAL_TM
    assert seq % tm == 0
    weights = _ffn_wspecs(0)
    row = pl.BlockSpec((1, tm, d), lambda bb, i: (bb, i, 0))
    h3 = pl.pallas_call(
        _ffn_kernel,
        grid=(b, seq // tm),
        in_specs=[row] + weights,
        out_specs=row,
        out_shape=jax.ShapeDtypeStruct((b, seq + BLOCK, d), F32),
        scratch_shapes=[pltpu.VMEM((tm, D_FF), BF16)],
        compiler_params=_cparams(2),
        name="ffn_first",
    )(x, g, wg, wu, wd)
    return pl.pallas_call(
        functools.partial(_ffn_kernel, aliased=True),
        grid=(b,),
        in_specs=[pl.BlockSpec((BLOCK, d), lambda bb: (0, 0))] + weights + [pl.BlockSpec(memory_space=pl.ANY)],
        out_specs=pl.BlockSpec((1, BLOCK, d), lambda bb: (bb, seq // BLOCK, 0)),
        out_shape=jax.ShapeDtypeStruct(h3.shape, F32),
        scratch_shapes=[pltpu.VMEM((BLOCK, D_FF), BF16)],
        input_output_aliases={5: 0},
        compiler_params=_cparams(1),
        name="ffn_first_front",
    )(front, g, wg, wu, wd, h3)


def _ffn_final(h3, g, wg, wu, wd, ly, fg, seq, mix=(), bias=None):
    b = h3.shape[0]
    tm = FFN_FINAL_TM
    assert seq % tm == 0
    row = lambda w: pl.BlockSpec((1, tm, w), lambda bb, i: (bb, i, 0))
    margs, mspecs, mkw = _mix_args(mix, bias, row)
    return pl.pallas_call(
        functools.partial(_ffn_kernel, final=True, **mkw),
        grid=(b, seq // tm),
        in_specs=[row(D_MODEL)] + mspecs + _ffn_wspecs(ly) + [_resident((1, D_MODEL))],
        out_specs=row(D_MODEL),
        out_shape=jax.ShapeDtypeStruct((b, seq, D_MODEL), F32),
        scratch_shapes=[pltpu.VMEM((tm, D_FF), BF16)],
        compiler_params=_cparams(2),
        name="ffn_final",
    )(h3, *margs, g, wg, wu, wd, fg)


def _rope(y, c, s, half):
    w = y.shape[1]
    reps = w // LANES
    if reps > 1:
        c = jnp.concatenate([c] * reps, axis=1)
        s = jnp.concatenate([s] * reps, axis=1)
    lane = lax.broadcasted_iota(jnp.int32, y.shape, 1)
    first = (lane & 63) < half
    partner = jnp.where(first, pltpu.roll(y, w - half, 1), pltpu.roll(y, half, 1))
    return y * c + partner * s


def _rope_tables(pos, rot_dim):
    half = rot_dim // 2
    g = np.arange(LANES) & 63
    inv = np.power(np.float32(ROPE_THETA), -(2 * (g % half)).astype(np.float32) / np.float32(rot_dim))
    ang = pos.astype(np.float32)[:, None] * inv[None, :].astype(np.float32)
    rotated = (g < rot_dim)[None, :]
    cos = np.where(rotated, np.cos(ang), np.float32(1.0)).astype(np.float32)
    sin = np.where(rotated, np.where((g < half)[None, :], -np.sin(ang), np.sin(ang)), np.float32(0.0))
    return jnp.asarray(cos), jnp.asarray(sin.astype(np.float32))


AB_IN_PAD = 3 * 512 + MLA_Q_RANK + MLA_KV_RANK + LANES


def _ab_proj_kernel(h_ref, g_ref, win_ref, qn_ref, wuq_ref, kvn_ref, wukv_ref, ca_ref, sa_ref, cm_ref, sm_ref,
                    qa_ref, ka_ref, va_ref, qb_ref, kb_ref, vb_ref):
    xn = _rms(h_ref[0], g_ref[...], EPS).astype(BF16)
    ca, sa, cm, sm = ca_ref[...], sa_ref[...], cm_ref[...], sm_ref[...]
    half_a = PART_ROT // 2
    half_m = MLA_ROPE // 2
    qa = _rope(_dot(xn, win_ref[:, 0:512]), ca, sa, half_a)
    qa_ref[0] = (qa * (DIFF_HEAD_DIM ** -0.5 * LOG2E)).astype(BF16)
    ka_ref[0] = _rope(_dot(xn, win_ref[:, 512:1024]), ca, sa, half_a).astype(BF16)
    va_ref[0] = _dot(xn, win_ref[:, 1024:1536]).astype(BF16)
    cq = _dot(xn, win_ref[:, 1536:1792])
    ckv = _dot(xn, win_ref[:, 1792:2048])
    kr = _rope(_dot(xn, win_ref[:, 2048:2176]), cm, sm, half_m).astype(BF16)
    qb = _dot(_rms(cq, qn_ref[...], EPS).astype(BF16), wuq_ref[...])
    scale = (MLA_NOPE + MLA_ROPE) ** -0.5 * LOG2E
    for hh in range(MLA_HEADS):
        o = 2 * LANES * hh
        qb_ref[0, :, o:o + LANES] = (qb[:, o:o + LANES] * scale).astype(BF16)
        qb_ref[0, :, o + LANES:o + 2 * LANES] = (_rope(qb[:, o + LANES:o + 2 * LANES], cm, sm, half_m) * scale).astype(BF16)
    kv = _dot(_rms(ckv, kvn_ref[...], EPS).astype(BF16), wukv_ref[...])
    for hh in range(MLA_HEADS):
        o = 2 * LANES * hh
        kb_ref[0, :, o:o + LANES] = kv[:, LANES * hh:LANES * (hh + 1)].astype(BF16)
        kb_ref[0, :, o + LANES:o + 2 * LANES] = kr
    vb_ref[0] = kv[:, MLA_HEADS * MLA_NOPE:].astype(BF16)


def _ab_proj(h3, g, win, qn, wuq, kvn, wukv, tabs):
    b, l, _ = h3.shape
    tm = PROJ_TM
    assert l % tm == 0
    row = lambda w: pl.BlockSpec((1, tm, w), lambda j, bb: (bb, j, 0))
    tab = pl.BlockSpec((tm, LANES), lambda j, bb: (j, 0))
    outw = (512, 512, 512, 1024, 1024, 512)
    return pl.pallas_call(
        _ab_proj_kernel,
        grid=(l // tm, b),
        in_specs=[row(D_MODEL), _resident((1, D_MODEL)), _resident((D_MODEL, AB_IN_PAD)),
                  _resident((1, MLA_Q_RANK)), _resident((MLA_Q_RANK, 1024)),
                  _resident((1, MLA_KV_RANK)), _resident((MLA_KV_RANK, 1024)), tab, tab, tab, tab],
        out_specs=[row(w) for w in outw],
        out_shape=[jax.ShapeDtypeStruct((b, l, w), BF16) for w in outw],
        compiler_params=_cparams(2),
        name="ab_proj",
    )(h3, g, win, qn, wuq, kvn, wukv, *tabs)


def _transpose(x):
    return x.T


def _head_slices(c0, heads):
    cph, dk, dv = heads
    h = c0 // cph
    return slice(h * dk, (h + 1) * dk), slice(h * dv, (h + 1) * dv)


def _flash_step(k, vt, qt, m_ref, l_ref, acc_ref, mask, heads, between=None):
    cw = min(ATT_CW, heads[0])
    probs = {}
    for c0 in range(0, qt.shape[1], cw):
        cs = slice(c0, c0 + cw)
        ks, _ = _head_slices(c0, heads)
        s = _dot(k[:, ks], qt[:, cs])
        s = jnp.where(mask(c0, s.shape), s, NEG_INF)
        m_new = jnp.max(s, axis=0, keepdims=True)
        p = jnp.exp2(s - m_new)
        l_ref[:, cs] = jnp.sum(p, axis=0, keepdims=True)
        m_ref[:, cs] = m_new
        probs[c0] = p.astype(BF16)
    if between is not None:
        between()
    for c0, p in probs.items():
        acc_ref[:, c0:c0 + cw] = _dot(vt[_head_slices(c0, heads)[1]], p)


def _produce(k, qt, s_ref, x_ref, chunks, heads):
    for c0 in chunks:
        cs = slice(c0, c0 + ATT_CW)
        s = _dot(k[:, _head_slices(c0, heads)[0]], qt[:, cs])
        s_ref[:, cs] = s
        x_ref[:, cs] = jnp.max(s, axis=0, keepdims=True)


def _consume(s_ref, x_ref, vt, m_ref, l_ref, acc_ref, chunks, heads, diag=None):
    tk = s_ref.shape[0]
    for c0 in chunks:
        cs = slice(c0, c0 + ATT_CW)
        if diag is None:
            seen = tk
        else:
            key0, tq, tri = diag
            seen = min((c0 & (tq - 1)) - key0, tk)
            if seen + ATT_CW <= 0:
                continue
        if seen == tk:
            s = s_ref[:, cs]
            smax = x_ref[:, cs]
        else:
            s = s_ref[seen:seen + ATT_CW, cs] + tri
            if seen > 0:
                s = jnp.concatenate([s_ref[0:seen, cs], s], axis=0)
            smax = jnp.max(s, axis=0, keepdims=True)
        m_prev = m_ref[:, cs]
        m_new = jnp.maximum(m_prev, smax)
        alpha = jnp.exp2(m_prev - m_new)
        p = jnp.exp2(s - m_new)
        l_ref[:, cs] = alpha * l_ref[:, cs] + jnp.sum(p, axis=0, keepdims=True)
        acc_ref[:, cs] = alpha * acc_ref[:, cs] + _dot(vt[_head_slices(c0, heads)[1], :s.shape[0]], p.astype(BF16))
        m_ref[:, cs] = m_new


def _causal_attend(make_qt, k_ref, v_ref, scr, *, tq, tk, seq, front, nh):
    vt_ref, m_ref, l_ref, acc_ref, s0, s1, x0, x1 = scr
    state = (m_ref, l_ref, acc_ref)
    heads = (m_ref.shape[1] // nh, k_ref.shape[2] // nh, v_ref.shape[2] // nh)
    f0 = 0 if front else seq
    kf = k_ref[0, f0:f0 + BLOCK, :]

    def valid(c0, shape):
        return lax.broadcasted_iota(jnp.int32, shape, 0) >= N_FRONT

    def causal(key0):
        def mask(c0, shape):
            key = lax.broadcasted_iota(jnp.int32, shape, 0) + key0
            qry = (lax.broadcasted_iota(jnp.int32, shape, 1) + c0) & (tq - 1)
            return key <= qry
        return mask

    if front:
        qt = make_qt()
        both = lambda c0, shape: valid(c0, shape) & causal(0)(c0, shape)
        _flash_step(kf, _transpose(v_ref[0, f0:f0 + BLOCK, :]), qt, *state, both, heads)
        return
    assert tq == 2 * tk
    i = pl.program_id(2)
    nblk = v_ref.shape[1] // BLOCK
    sub = tk // BLOCK
    chunks = list(range(0, m_ref.shape[1], ATT_CW))
    late = [c0 for c0 in chunks if (c0 & (tq - 1)) + ATT_CW > tk]

    @pl.when(i == 0)
    def _fill():
        def fill(c, carry):
            st = pl.multiple_of(c * BLOCK, BLOCK)
            vt_ref[c] = _transpose(v_ref[0, pl.ds(st, BLOCK), :])
            return carry
        lax.fori_loop(0, nblk, fill, 0, unroll=VT_FILL_UNROLL)

    qt = make_qt()

    def k_block(j):
        return k_ref[0, pl.ds(pl.multiple_of(j * tk, tk), tk), :]

    def vt_block(j):
        return jnp.concatenate([vt_ref[j * sub + u] for u in range(sub)], axis=1)

    def step(k_next, s_next, x_next, s_cur, x_cur, vt_cur, produced=chunks, diag=None):
        for g0 in range(0, len(chunks), ATT_GROUP):
            grp = chunks[g0:g0 + ATT_GROUP]
            _produce(k_next, qt, s_next, x_next, [c0 for c0 in grp if c0 in produced], heads)
            _consume(s_cur, x_cur, vt_cur, *state, grp, heads, diag=diag)

    _flash_step(kf, vt_ref[seq // BLOCK], qt, *state, valid, heads,
                between=lambda: _produce(k_block(0), qt, s0, x0, chunks, heads))

    def body(p, carry):
        step(k_block(2 * p + 1), s1, x1, s0, x0, vt_block(2 * p))
        step(k_block(2 * p + 2), s0, x0, s1, x1, vt_block(2 * p + 1))
        return carry

    lax.fori_loop(0, i, body, 0)
    tri_shape = (ATT_CW, ATT_CW)
    tri = jnp.where(lax.broadcasted_iota(jnp.int32, tri_shape, 0) <= lax.broadcasted_iota(jnp.int32, tri_shape, 1),
                    0.0, NEG_INF)
    step(k_block(2 * i + 1), s1, x1, s0, x0, vt_block(2 * i), produced=late, diag=(0, tq, tri))
    _consume(s1, x1, vt_block(2 * i + 1), *state, chunks, heads, diag=(tk, tq, tri))


def _diff_kernel(q_ref, k_ref, v_ref, lq1_ref, lk1_ref, lq2_ref, lk2_ref, sub_ref, *rest,
                 tq, tk, seq, front, lambda_init, nh):
    if front:
        _, o_ref, *scr = rest
    else:
        o_ref, *scr = rest
    m_ref, l_ref, acc_ref = scr[1:4]

    def stacked_queries():
        t = _transpose(q_ref[0])
        d = lax.broadcasted_iota(jnp.int32, (LANES, tq), 0)
        zero = jnp.zeros((LANES, tq), t.dtype)
        cols = []
        for h in range(nh):
            th = t[h * LANES:(h + 1) * LANES]
            cols += [jnp.where(d < DIFF_HEAD_DIM, th, zero), jnp.where(d >= DIFF_HEAD_DIM, th, zero)]
        return jnp.concatenate(cols, axis=1)

    _causal_attend(stacked_queries, k_ref, v_ref, scr, tq=tq, tk=tk, seq=seq, front=front, nh=nh)
    o = acc_ref[...] / l_ref[...]
    lam = (jnp.exp(jnp.sum(lq1_ref[...] * lk1_ref[...], keepdims=True))
           - jnp.exp(jnp.sum(lq2_ref[...] * lk2_ref[...], keepdims=True)) + lambda_init)
    for h in range(nh):
        w = o[:, 2 * h * tq:(2 * h + 1) * tq] - lam * o[:, (2 * h + 1) * tq:(2 * h + 2) * tq]
        w = w * lax.rsqrt(jnp.mean(w * w, axis=0, keepdims=True) + DIFF_EPS) * sub_ref[...]
        o_ref[0, :, h * LANES:(h + 1) * LANES] = (w * (1.0 - lambda_init)).T.astype(BF16)


def _mla_kernel(q_ref, k_ref, v_ref, *rest, tq, tk, seq, front, nh):
    if front:
        _, o_ref, *scr = rest
    else:
        o_ref, *scr = rest
    m_ref, l_ref, acc_ref = scr[1:4]
    dk = q_ref.shape[2] // nh

    def queries():
        t = _transpose(q_ref[0])
        return jnp.concatenate([t[h * dk:(h + 1) * dk] for h in range(nh)], axis=1)

    _causal_attend(queries, k_ref, v_ref, scr, tq=tq, tk=tk, seq=seq, front=front, nh=nh)
    o = acc_ref[...] / l_ref[...]
    for h in range(nh):
        o_ref[0, :, h * LANES:(h + 1) * LANES] = o[:, h * tq:(h + 1) * tq].T.astype(BF16)


def _causal_attention(kernel, q, k, v, extra, *, heads, dk, reps, tq, tk, seq, name, nh=1):
    b, l, _ = q.shape
    dv = LANES * nh
    dk = dk * nh
    reps = reps * nh
    heads = heads // nh
    outs = None
    for front in (False, True):
        t = BLOCK if front else tq
        if front:
            grid = (b, heads, 1)
            qmap = lambda bb, hh, i: (bb, seq // BLOCK, hh)
        else:
            grid = (b, heads, seq // t)
            qmap = lambda bb, hh, i: (bb, i, hh)
        if front:
            kvl, kvmap = BLOCK, qmap
        else:
            kvl, kvmap = l, lambda bb, hh, i: (bb, 0, hh)
        in_specs = [pl.BlockSpec((1, t, dk), qmap), pl.BlockSpec((1, kvl, dk), kvmap),
                    pl.BlockSpec((1, kvl, dv), kvmap)]
        in_specs += [_resident(e.shape) for e in extra]
        args = [q, k, v, *extra]
        aliases = {}
        if front:
            in_specs.append(pl.BlockSpec(memory_space=pl.ANY))
            aliases = {len(args): 0}
            args.append(outs)
        outs = pl.pallas_call(
            functools.partial(kernel, tq=t, tk=min(t, tk), seq=seq, front=front),
            grid=grid,
            in_specs=in_specs,
            out_specs=pl.BlockSpec((1, t, dv), qmap),
            out_shape=jax.ShapeDtypeStruct((b, l, heads * dv), BF16),
            scratch_shapes=[pltpu.VMEM((1 if front else l // BLOCK, dv, BLOCK), BF16),
                            pltpu.VMEM((1, reps * t), F32), pltpu.VMEM((1, reps * t), F32),
                            pltpu.VMEM((LANES, reps * t), F32)]
                           + [pltpu.VMEM((8 if front else tk, reps * t), F32)] * 2
                           + [pltpu.VMEM((1, reps * t), F32)] * 2,
            input_output_aliases=aliases,
            compiler_params=_cparams(3),
            name=name + ("_front" if front else ""),
        )(*args)
    return outs


SWA_Q_COLS = SWA_HEADS * SWA_HEAD_DIM
SWA_KV_COLS = SWA_KV_HEADS * SWA_HEAD_DIM
SWA_IN_COLS = SWA_Q_COLS + 2 * SWA_KV_COLS


def _swa_proj_kernel(h_ref, g_ref, w_ref, b_ref, ca_ref, sa_ref, q_ref, k_ref, v_ref):
    xn = _rms(h_ref[0], g_ref[...], EPS).astype(BF16)
    ca, sa = ca_ref[...], sa_ref[...]
    half = PART_ROT // 2
    nq = SWA_Q_COLS
    nk = SWA_KV_COLS
    q = _rope(_dot(xn, w_ref[:, 0:nq]) + b_ref[:, 0:nq], ca, sa, half)
    q_ref[0] = (q * (SWA_HEAD_DIM ** -0.5 * LOG2E)).astype(BF16)
    k_ref[0] = _rope(_dot(xn, w_ref[:, nq:nq + nk]) + b_ref[:, nq:nq + nk], ca, sa, half).astype(BF16)
    v_ref[0] = (_dot(xn, w_ref[:, nq + nk:]) + b_ref[:, nq + nk:]).astype(BF16)


def _swa_proj(h3, g, w, bias, tabs):
    b, l, _ = h3.shape
    tm = PROJ_TM
    row = lambda wd: pl.BlockSpec((1, tm, wd), lambda j, bb: (bb, j, 0))
    tab = pl.BlockSpec((tm, LANES), lambda j, bb: (j, 0))
    outw = (SWA_Q_COLS, SWA_KV_COLS, SWA_KV_COLS)
    return pl.pallas_call(
        _swa_proj_kernel,
        grid=(l // tm, b),
        in_specs=[row(D_MODEL), _resident((1, D_MODEL)), _resident((D_MODEL, SWA_IN_COLS)),
                  _resident((1, SWA_IN_COLS)), tab, tab],
        out_specs=[row(w_) for w_ in outw],
        out_shape=[jax.ShapeDtypeStruct((b, l, w_), BF16) for w_ in outw],
        compiler_params=_cparams(2),
        name="swa_proj",
    )(h3, g, w, bias, *tabs)


def _swa_kernel(sinks_ref, q_ref, k_ref, v_ref, *rest, tq, seq, front):
    if front:
        _, o_ref, s_a, s_b = rest
    else:
        o_ref, s_a, s_b = rest
    hd_dim = SWA_HEAD_DIM
    ts = min(SWA_TS, tq)
    pairs = SWA_GROUP // 2
    cw = 2 * ts
    f0 = 0 if front else seq
    meta_k = k_ref[0, f0 + N_FRONT:f0 + BLOCK, :]
    meta_v = v_ref[0, f0 + N_FRONT:f0 + BLOCK, :]
    band = 0 if front else ts + WINDOW
    nk = band + N_META
    nkp = -(-nk // LANES) * LANES
    qt = _transpose(q_ref[0])
    zh = jnp.zeros((hd_dim, cw), BF16)
    key = lax.broadcasted_iota(jnp.int32, (nk, ts), 0)
    qry = lax.broadcasted_iota(jnp.int32, (nk, ts), 1)
    first_half = lax.broadcasted_iota(jnp.int32, (1, cw), 1) < ts

    def keys_of(u):
        if front:
            kc, vc = meta_k, meta_v
            allowed = key + N_FRONT <= qry
        else:
            start = pl.program_id(1) * tq + u * ts
            bs = pl.multiple_of(jnp.maximum(start - WINDOW, 0), BLOCK)
            kc = jnp.concatenate([k_ref[0, pl.ds(bs, band), :], meta_k], axis=0)
            vc = jnp.concatenate([v_ref[0, pl.ds(bs, band), :], meta_v], axis=0)
            dist = (start + qry) - (bs + key)
            allowed = (key >= band) | ((dist >= 0) & (dist < WINDOW))
        bias = jnp.where(allowed, 0.0, NEG_INF)
        vt = _transpose(jnp.concatenate([vc, jnp.zeros((nkp - nk, vc.shape[1]), vc.dtype)], axis=0))
        return kc, vt, jnp.concatenate([bias, bias], axis=1)

    sets = [(u, g) for u in range(tq // ts) for g in range(SWA_KV_HEADS)]
    operands = {}
    sbuf = (s_a, s_b)

    def produce(n, pp):
        u, g = sets[n]
        if u not in operands:
            operands[u] = keys_of(u)
        hd = g * SWA_GROUP + 2 * pp
        qh = jnp.concatenate([qt[(hd + a) * hd_dim:(hd + a + 1) * hd_dim, u * ts:(u + 1) * ts] for a in (0, 1)],
                             axis=1)
        rhs = jnp.concatenate([qh, zh] if g == 0 else [zh, qh], axis=0)
        sbuf[n % 2][:, pp * cw:(pp + 1) * cw] = _dot(operands[u][0], rhs)

    outs = {}

    def consume(n, pp):
        u, g = sets[n]
        _, vt, bias = operands[u]
        hd = g * SWA_GROUP + 2 * pp
        sink = jnp.where(first_half, sinks_ref[hd] * LOG2E, sinks_ref[hd + 1] * LOG2E)
        s = sbuf[n % 2][:, pp * cw:(pp + 1) * cw] + bias
        m = jnp.maximum(jnp.max(s, axis=0, keepdims=True), sink)
        e = jnp.exp2(s - m)
        den = jnp.sum(e, axis=0, keepdims=True) + jnp.exp2(sink - m)
        eb = jnp.concatenate([e.astype(BF16), jnp.zeros((nkp - nk, cw), BF16)], axis=0)
        o = _dot(vt, eb)[g * hd_dim:(g + 1) * hd_dim] / den
        outs[hd, u] = o[:, :ts]
        outs[hd + 1, u] = o[:, ts:]

    for pp in range(pairs):
        produce(0, pp)
    for n in range(1, len(sets)):
        for pp in range(pairs):
            produce(n, pp)
            consume(n - 1, pp)
    for pp in range(pairs):
        consume(len(sets) - 1, pp)
    rows = [jnp.concatenate([outs[hd, u] for u in range(tq // ts)], axis=1) for hd in range(SWA_HEADS)]
    o_ref[0] = jnp.concatenate(rows, axis=0).T.astype(BF16)


def _swa_attention(q, k, v, sinks, *, seq):
    b, l, _ = q.shape
    nq = SWA_Q_COLS
    outs = None
    for front in (False, True):
        tq = BLOCK if front else SWA_TQ
        ts = min(SWA_TS, tq)
        nk = N_META if front else ts + WINDOW + N_META
        if front:
            grid = (b, 1)
            qmap = lambda bb, i: (bb, seq // BLOCK, 0)
            kvl, kvmap = BLOCK, qmap
        else:
            grid = (b, seq // tq)
            qmap = lambda bb, i: (bb, i, 0)
            kvl, kvmap = l, lambda bb, i: (bb, 0, 0)
        in_specs = [pl.BlockSpec(memory_space=pltpu.SMEM), pl.BlockSpec((1, tq, nq), qmap),
                    pl.BlockSpec((1, kvl, k.shape[2]), kvmap), pl.BlockSpec((1, kvl, v.shape[2]), kvmap)]
        args = [sinks, q, k, v]
        aliases = {}
        if front:
            in_specs.append(pl.BlockSpec(memory_space=pl.ANY))
            aliases = {len(args): 0}
            args.append(outs)
        outs = pl.pallas_call(
            functools.partial(_swa_kernel, tq=tq, seq=seq, front=front),
            grid=grid,
            in_specs=in_specs,
            out_specs=pl.BlockSpec((1, tq, nq), qmap),
            out_shape=jax.ShapeDtypeStruct((b, l, nq), BF16),
            scratch_shapes=[pltpu.VMEM((nk, SWA_GROUP * ts), F32)] * 2,
            input_output_aliases=aliases,
            compiler_params=_cparams(2),
            name="swa_attn" + ("_front" if front else ""),
        )(*args)
    return outs


def _ab_weights(w_in, w_uq, w_ukv):
    win = jnp.pad(w_in, ((0, 0), (0, AB_IN_PAD - w_in.shape[1]))).astype(BF16)
    wq = w_uq.reshape(MLA_Q_RANK, MLA_HEADS, MLA_NOPE + MLA_ROPE)
    wq = jnp.pad(wq, ((0, 0), (0, 0), (0, 2 * LANES - MLA_NOPE - MLA_ROPE))).reshape(MLA_Q_RANK, MLA_HEADS * 2 * LANES)
    wkv = w_ukv.reshape(MLA_KV_RANK, MLA_HEADS, 2, MLA_NOPE).transpose(0, 2, 1, 3).reshape(MLA_KV_RANK, -1)
    return win, wq.astype(BF16), wkv.astype(BF16)


def kernel(x, meta_tokens, ffn1_norm, ffn1_w_gate, ffn1_w_up, ffn1_w_down, mix_norm, ab_w_in, diff_lambda_q1, diff_lambda_k1, diff_lambda_q2, diff_lambda_k2, diff_subln, mla_q_norm, mla_w_uq, mla_kv_norm, mla_w_ukv, ab_w_out, swa_w_qkv, swa_b_qkv, swa_sinks, swa_w_out, swa_b_out, ffn2_norm, ffn2_w_gate, ffn2_w_up, ffn2_w_down, final_norm):
    b, seq, d = x.shape
    depth = ffn1_norm.shape[0]
    l = seq + BLOCK
    n = b * l
    front = jnp.concatenate([jnp.zeros((N_FRONT, d), x.dtype), meta_tokens.astype(x.dtype)], axis=0)

    pos = np.concatenate([np.arange(seq) + N_META, np.maximum(np.arange(BLOCK) - N_FRONT, 0)])
    tabs_p = _rope_tables(pos, PART_ROT)
    tabs_m = _rope_tables(pos, MLA_ROPE)
    row2 = lambda a: a.reshape(1, -1)

    ffn1 = (ffn1_w_gate, ffn1_w_up, ffn1_w_down.astype(BF16))
    ffn2 = (ffn2_w_gate, ffn2_w_up, ffn2_w_down.astype(BF16))
    out = None
    for ly in range(depth):
        w1 = (row2(ffn1_norm[ly]), *ffn1)
        h = _ffn_first(x, front, *w1).reshape(n, d) if ly == 0 else _ffn(h, *w1, ly)
        h3 = h.reshape(b, l, d)
        if ly % 2 == 0:
            e = ly // 2
            lambda_init = 0.8 - 0.6 * math.exp(-0.3 * ly)
            win, wuq, wukv = _ab_weights(ab_w_in[e], mla_w_uq[e], mla_w_ukv[e])
            qa, ka, va, qb, kb, vb = _ab_proj(h3, row2(mix_norm[ly]), win, row2(mla_q_norm[e]), wuq,
                                              row2(mla_kv_norm[e]), wukv, tabs_p + tabs_m)
            extra = [row2(diff_lambda_q1[e]), row2(diff_lambda_k1[e]), row2(diff_lambda_q2[e]),
                     row2(diff_lambda_k2[e]), diff_subln[e].reshape(-1, 1)]
            oa = _causal_attention(functools.partial(_diff_kernel, lambda_init=lambda_init, nh=DIFF_HEADS_PER_STEP),
                                   qa, ka, va, extra, heads=DIFF_HEADS, dk=LANES, reps=2, tq=ATT_TQ, tk=ATT_TK,
                                   seq=seq, name="diff_attn", nh=DIFF_HEADS_PER_STEP)
            ob = _causal_attention(functools.partial(_mla_kernel, nh=MLA_HEADS_PER_STEP), qb, kb, vb, [],
                                   heads=MLA_HEADS, dk=2 * LANES, reps=1, tq=ATT_TQ, tk=ATT_TK, seq=seq,
                                   name="mla_attn", nh=MLA_HEADS_PER_STEP)
            wo = ab_w_out[e].astype(BF16)
            mix, bias = [(oa, wo[:512]), (ob, wo[512:])], None
        else:
            o = ly // 2
            q, k, v = _swa_proj(h3, row2(mix_norm[ly]), swa_w_qkv[o].astype(BF16), row2(swa_b_qkv[o]), tabs_p)
            att = _swa_attention(q, k, v, swa_sinks[o], seq=seq)
            mix, bias = [(att, swa_w_out[o].astype(BF16))], row2(swa_b_out[o])
        w2 = (row2(ffn2_norm[ly]), *ffn2)
        if ly == depth - 1:
            out = _ffn_final(h3, *w2, ly, row2(final_norm), seq, mix=mix, bias=bias)
        else:
            h = _ffn(h, *w2, ly, mix=[(a.reshape(n, -1), w) for a, w in mix], bias=bias)
    return out
```

```python
import functools
import math

import numpy as np
import jax
import jax.numpy as jnp
from jax import lax
from jax.experimental import pallas as pl
from jax.experimental.pallas import tpu as pltpu

F32 = jnp.float32
BF16 = jnp.bfloat16

D_MODEL = 1024
N_META = 16
BLOCK = 128
N_FRONT = BLOCK - N_META
ROPE_THETA = 500000.0
EPS = 1e-6
NEG_INF = -1e30
D_FF = 2816

DIFF_HEADS = 4
DIFF_HEAD_DIM = 64
DIFF_V_DIM = 2 * DIFF_HEAD_DIM
DIFF_EPS = 1e-5
MLA_HEADS = 4
MLA_NOPE = 128
MLA_ROPE = 64
MLA_V = 128
MLA_Q_RANK = 256
MLA_KV_RANK = 256
SWA_HEADS = 16
SWA_KV_HEADS = 2
SWA_GROUP = SWA_HEADS // SWA_KV_HEADS
SWA_HEAD_DIM = 64
WINDOW = 128

PART_ROT = DIFF_HEAD_DIM // 4
LANES = 128
VMEM_LIMIT = 56 * 1024 * 1024

FFN_TM = 640
FFN_FC = 256
PROJ_TM = 640
FFN_FINAL_TM = 512
ATT_TK = 512
ATT_TQ = 2 * ATT_TK
ATT_CW = 256
DIFF_HEADS_PER_STEP = 2
MLA_HEADS_PER_STEP = 2
DIFF_LAG = 2
MLA_LAG = 3
ATT_GROUP = 2
VT_FILL_UNROLL = 5
LOG2E = 1.4426950408889634
SWA_TQ = 1024
SWA_TS = 128


def _cparams(n_axes):
    return pltpu.CompilerParams(dimension_semantics=("arbitrary",) * n_axes,
                                vmem_limit_bytes=VMEM_LIMIT)


def _resident(shape):
    nd = len(shape)
    return pl.BlockSpec(shape, lambda *_: (0,) * nd, pipeline_mode=pl.Buffered(1))


def _rms(x, g, eps):
    return x * lax.rsqrt(jnp.mean(x * x, axis=-1, keepdims=True) + eps) * g


def _dot(a, b):
    return jnp.dot(a, b, preferred_element_type=F32)


def _dot_nt(a, b):
    return lax.dot_general(a, b, (((1,), (1,)), ((), ())), preferred_element_type=F32)


def _ffn_kernel(x_ref, *refs, final=False, aliased=False, n_mix=0, mix_bias=False):
    mix, refs = refs[:2 * n_mix], refs[2 * n_mix:]
    if mix_bias:
        mb_ref, refs = refs[0], refs[1:]
    g_ref, wg_ref, wu_ref, wd_ref, *rest = refs
    if final:
        fg_ref, o_ref, act_ref = rest
    elif aliased:
        _, o_ref, act_ref = rest
    else:
        o_ref, act_ref = rest
    x = x_ref[...].reshape(x_ref.shape[-2:])
    for a_ref, w_ref in zip(mix[0::2], mix[1::2]):
        x = x + _dot(a_ref[...].reshape(a_ref.shape[-2:]), w_ref[...])
    if mix_bias:
        x = x + mb_ref[...]
    xn = _rms(x, g_ref[...], EPS).astype(BF16)
    for c in range(D_FF // FFN_FC):
        sl = slice(c * FFN_FC, (c + 1) * FFN_FC)
        g = _dot(xn, wg_ref[:, sl].astype(BF16))
        u = _dot(xn, wu_ref[:, sl].astype(BF16))
        act_ref[:, sl] = (g * (1.0 / (1.0 + jnp.exp(-g))) * u).astype(BF16)
    y = x + 0.5 * _dot(act_ref[...], wd_ref[...])
    if final:
        y = _rms(y, fg_ref[...], EPS)
    o_ref[...] = y.reshape(o_ref.shape)


def _mix_args(mix, bias, row):
    args, specs = [], []
    for a, w in mix:
        args += [a, w]
        specs += [row(a.shape[-1]), _resident(w.shape)]
    if bias is not None:
        args.append(bias)
        specs.append(_resident(bias.shape))
    return args, specs, dict(n_mix=len(mix), mix_bias=bias is not None)


def _ffn_wspecs(ly):
    mat = lambda r, c: pl.BlockSpec((None, r, c), lambda *_: (ly, 0, 0), pipeline_mode=pl.Buffered(1))
    return [_resident((1, D_MODEL)), mat(D_MODEL, D_FF), mat(D_MODEL, D_FF), mat(D_FF, D_MODEL)]


def _ffn(h, g, wg, wu, wd, ly, mix=(), bias=None):
    n = h.shape[0]
    tm = FFN_TM
    assert n % tm == 0
    row = lambda w: pl.BlockSpec((tm, w), lambda i: (i, 0))
    margs, mspecs, mkw = _mix_args(mix, bias, row)
    return pl.pallas_call(
        functools.partial(_ffn_kernel, **mkw),
        grid=(n // tm,),
        in_specs=[row(D_MODEL)] + mspecs + _ffn_wspecs(ly),
        out_specs=row(D_MODEL),
        out_shape=jax.ShapeDtypeStruct((n, D_MODEL), F32),
        scratch_shapes=[pltpu.VMEM((tm, D_FF), BF16)],
        compiler_params=_cparams(1),
        name="ffn",
    )(h, *margs, g, wg, wu, wd)


def _ffn_first(x, front, g, wg, wu, wd):
    b, seq, d = x.shape
    tm = FFN_FINAL_TM
    assert seq % tm == 0
    weights = _ffn_wspecs(0)
    row = pl.BlockSpec((1, tm, d), lambda bb, i: (bb, i, 0))
    h3 = pl.pallas_call(
        _ffn_kernel,
        grid=(b, seq // tm),
        in_specs=[row] + weights,
        out_specs=row,
        out_shape=jax.ShapeDtypeStruct((b, seq + BLOCK, d), F32),
        scratch_shapes=[pltpu.VMEM((tm, D_FF), BF16)],
        compiler_params=_cparams(2),
        name="ffn_first",
    )(x, g, wg, wu, wd)
    return pl.pallas_call(
        functools.partial(_ffn_kernel, aliased=True),
        grid=(b,),
        in_specs=[pl.BlockSpec((BLOCK, d), lambda bb: (0, 0))] + weights + [pl.BlockSpec(memory_space=pl.ANY)],
        out_specs=pl.BlockSpec((1, BLOCK, d), lambda bb: (bb, seq // BLOCK, 0)),
        out_shape=jax.ShapeDtypeStruct(h3.shape, F32),
        scratch_shapes=[pltpu.VMEM((BLOCK, D_FF), BF16)],
        input_output_aliases={5: 0},
        compiler_params=_cparams(1),
        name="ffn_first_front",
    )(front, g, wg, wu, wd, h3)


def _ffn_final(h3, g, wg, wu, wd, ly, fg, seq, mix=(), bias=None):
    b = h3.shape[0]
    tm = FFN_FINAL_TM
    assert seq % tm == 0
    row = lambda w: pl.BlockSpec((1, tm, w), lambda bb, i: (bb, i, 0))
    margs, mspecs, mkw = _mix_args(mix, bias, row)
    return pl.pallas_call(
        functools.partial(_ffn_kernel, final=True, **mkw),
        grid=(b, seq // tm),
        in_specs=[row(D_MODEL)] + mspecs + _ffn_wspecs(ly) + [_resident((1, D_MODEL))],
        out_specs=row(D_MODEL),
        out_shape=jax.ShapeDtypeStruct((b, seq, D_MODEL), F32),
        scratch_shapes=[pltpu.VMEM((tm, D_FF), BF16)],
        compiler_params=_cparams(2),
        name="ffn_final",
    )(h3, *margs, g, wg, wu, wd, fg)


def _rope(y, c, s, half):
    w = y.shape[1]
    reps = w // LANES
    if reps > 1:
        c = jnp.concatenate([c] * reps, axis=1)
        s = jnp.concatenate([s] * reps, axis=1)
    lane = lax.broadcasted_iota(jnp.int32, y.shape, 1)
    first = (lane & 63) < half
    partner = jnp.where(first, pltpu.roll(y, w - half, 1), pltpu.roll(y, half, 1))
    return y * c + partner * s


def _rope_tables(pos, rot_dim):
    half = rot_dim // 2
    g = np.arange(LANES) & 63
    inv = np.power(np.float32(ROPE_THETA), -(2 * (g % half)).astype(np.float32) / np.float32(rot_dim))
    ang = pos.astype(np.float32)[:, None] * inv[None, :].astype(np.float32)
    rotated = (g < rot_dim)[None, :]
    cos = np.where(rotated, np.cos(ang), np.float32(1.0)).astype(np.float32)
    sin = np.where(rotated, np.where((g < half)[None, :], -np.sin(ang), np.sin(ang)), np.float32(0.0))
    return jnp.asarray(cos), jnp.asarray(sin.astype(np.float32))


AB_IN_PAD = 3 * 512 + MLA_Q_RANK + MLA_KV_RANK + LANES


def _ab_proj_kernel(h_ref, g_ref, win_ref, qn_ref, wuq_ref, kvn_ref, wukv_ref, ca_ref, sa_ref, cm_ref, sm_ref,
                    qa_ref, ka_ref, va_ref, qb_ref, kb_ref, vb_ref):
    xn = _rms(h_ref[0], g_ref[...], EPS).astype(BF16)
    ca, sa, cm, sm = ca_ref[...], sa_ref[...], cm_ref[...], sm_ref[...]
    half_a = PART_ROT // 2
    half_m = MLA_ROPE // 2
    qa = _rope(_dot(xn, win_ref[:, 0:512]), ca, sa, half_a)
    qa_ref[0] = (qa * (DIFF_HEAD_DIM ** -0.5 * LOG2E)).astype(BF16)
    ka_ref[0] = _rope(_dot(xn, win_ref[:, 512:1024]), ca, sa, half_a).astype(BF16)
    va_ref[0] = _dot(xn, win_ref[:, 1024:1536]).astype(BF16)
    cq = _dot(xn, win_ref[:, 1536:1792])
    ckv = _dot(xn, win_ref[:, 1792:2048])
    kr = _rope(_dot(xn, win_ref[:, 2048:2176]), cm, sm, half_m).astype(BF16)
    qb = _dot(_rms(cq, qn_ref[...], EPS).astype(BF16), wuq_ref[...])
    scale = (MLA_NOPE + MLA_ROPE) ** -0.5 * LOG2E
    for hh in range(MLA_HEADS):
        o = 2 * LANES * hh
        qb_ref[0, :, o:o + LANES] = (qb[:, o:o + LANES] * scale).astype(BF16)
        qb_ref[0, :, o + LANES:o + 2 * LANES] = (_rope(qb[:, o + LANES:o + 2 * LANES], cm, sm, half_m) * scale).astype(BF16)
    kv = _dot(_rms(ckv, kvn_ref[...], EPS).astype(BF16), wukv_ref[...])
    for hh in range(MLA_HEADS):
        o = 2 * LANES * hh
        kb_ref[0, :, o:o + LANES] = kv[:, LANES * hh:LANES * (hh + 1)].astype(BF16)
        kb_ref[0, :, o + LANES:o + 2 * LANES] = kr
    vb_ref[0] = kv[:, MLA_HEADS * MLA_NOPE:].astype(BF16)


def _ab_proj(h3, g, win, qn, wuq, kvn, wukv, tabs):
    b, l, _ = h3.shape
    tm = PROJ_TM
    assert l % tm == 0
    row = lambda w: pl.BlockSpec((1, tm, w), lambda j, bb: (bb, j, 0))
    tab = pl.BlockSpec((tm, LANES), lambda j, bb: (j, 0))
    outw = (512, 512, 512, 1024, 1024, 512)
    return pl.pallas_call(
        _ab_proj_kernel,
        grid=(l // tm, b),
        in_specs=[row(D_MODEL), _resident((1, D_MODEL)), _resident((D_MODEL, AB_IN_PAD)),
                  _resident((1, MLA_Q_RANK)), _resident((MLA_Q_RANK, 1024)),
                  _resident((1, MLA_KV_RANK)), _resident((MLA_KV_RANK, 1024)), tab, tab, tab, tab],
        out_specs=[row(w) for w in outw],
        out_shape=[jax.ShapeDtypeStruct((b, l, w), BF16) for w in outw],
        compiler_params=_cparams(2),
        name="ab_proj",
    )(h3, g, win, qn, wuq, kvn, wukv, *tabs)


def _transpose(x):
    return x.T


def _head_slices(c0, heads):
    cph, dk, dv = heads
    h = c0 // cph
    return slice(h * dk, (h + 1) * dk), slice(h * dv, (h + 1) * dv)


def _flash_step(k, vt, qt, m_ref, l_ref, acc_ref, mask, heads, between=None):
    cw = min(ATT_CW, heads[0])
    probs = {}
    for c0 in range(0, qt.shape[1], cw):
        cs = slice(c0, c0 + cw)
        ks, _ = _head_slices(c0, heads)
        s = _dot(k[:, ks], qt[:, cs])
        s = jnp.where(mask(c0, s.shape), s, NEG_INF)
        m_new = jnp.max(s, axis=0, keepdims=True)
        p = jnp.exp2(s - m_new)
        l_ref[:, cs] = jnp.sum(p, axis=0, keepdims=True)
        m_ref[:, cs] = m_new
        probs[c0] = p.astype(BF16)
    if between is not None:
        between()
    for c0, p in probs.items():
        acc_ref[:, c0:c0 + cw] = _dot(vt[_head_slices(c0, heads)[1]], p)


def _produce(k, qt, s_ref, x_ref, chunks, heads):
    for c0 in chunks:
        cs = slice(c0, c0 + ATT_CW)
        s = _dot(k[:, _head_slices(c0, heads)[0]], qt[:, cs])
        s_ref[:, cs] = s
        x_ref[:, cs] = jnp.max(s, axis=0, keepdims=True)


def _consume(s_ref, x_ref, vt, m_ref, l_ref, acc_ref, chunks, heads, diag=None):
    tk = s_ref.shape[0]
    for c0 in chunks:
        cs = slice(c0, c0 + ATT_CW)
        if diag is None:
            seen = tk
        else:
            key0, tq, tri = diag
            seen = min((c0 & (tq - 1)) - key0, tk)
            if seen + ATT_CW <= 0:
                continue
        if seen == tk:
            s = s_ref[:, cs]
            smax = x_ref[:, cs]
        else:
            s = s_ref[seen:seen + ATT_CW, cs] + tri
            if seen > 0:
                s = jnp.concatenate([s_ref[0:seen, cs], s], axis=0)
            smax = jnp.max(s, axis=0, keepdims=True)
        m_prev = m_ref[:, cs]
        m_new = jnp.maximum(m_prev, smax)
        alpha = jnp.exp2(m_prev - m_new)
        p = jnp.exp2(s - m_new)
        l_ref[:, cs] = alpha * l_ref[:, cs] + jnp.sum(p, axis=0, keepdims=True)
        acc_ref[:, cs] = alpha * acc_ref[:, cs] + _dot(vt[_head_slices(c0, heads)[1], :s.shape[0]], p.astype(BF16))
        m_ref[:, cs] = m_new


def _causal_attend(make_qt, k_ref, v_ref, scr, *, tq, tk, seq, front, nh, lag):
    vt_ref, m_ref, l_ref, acc_ref, s0, s1, x0, x1 = scr
    state = (m_ref, l_ref, acc_ref)
    heads = (m_ref.shape[1] // nh, k_ref.shape[2] // nh, v_ref.shape[2] // nh)
    f0 = 0 if front else seq
    kf = k_ref[0, f0:f0 + BLOCK, :]

    def valid(c0, shape):
        return lax.broadcasted_iota(jnp.int32, shape, 0) >= N_FRONT

    def causal(key0):
        def mask(c0, shape):
            key = lax.broadcasted_iota(jnp.int32, shape, 0) + key0
            qry = (lax.broadcasted_iota(jnp.int32, shape, 1) + c0) & (tq - 1)
            return key <= qry
        return mask

    if front:
        qt = make_qt()
        both = lambda c0, shape: valid(c0, shape) & causal(0)(c0, shape)
        _flash_step(kf, _transpose(v_ref[0, f0:f0 + BLOCK, :]), qt, *state, both, heads)
        return
    assert tq == 2 * tk
    i = pl.program_id(2)
    nblk = v_ref.shape[1] // BLOCK
    sub = tk // BLOCK
    chunks = list(range(0, m_ref.shape[1], ATT_CW))
    late = [c0 for c0 in chunks if (c0 & (tq - 1)) + ATT_CW > tk]

    @pl.when(i == 0)
    def _fill():
        def fill(c, carry):
            st = pl.multiple_of(c * BLOCK, BLOCK)
            vt_ref[c] = _transpose(v_ref[0, pl.ds(st, BLOCK), :])
            return carry
        lax.fori_loop(0, nblk, fill, 0, unroll=VT_FILL_UNROLL)

    qt = make_qt()

    def k_block(j):
        return k_ref[0, pl.ds(pl.multiple_of(j * tk, tk), tk), :]

    def vt_block(j):
        return jnp.concatenate([vt_ref[j * sub + u] for u in range(sub)], axis=1)

    groups = [chunks[g0:g0 + ATT_GROUP] for g0 in range(0, len(chunks), ATT_GROUP)]
    ng = len(groups)
    lag = min(lag, ng)
    bufs = ((s0, x0), (s1, x1))

    def produce(kb, parity, g, allowed=chunks):
        _produce(kb, qt, *bufs[parity], [c0 for c0 in groups[g] if c0 in allowed], heads)

    def consume(vb, parity, g, diag=None):
        _consume(*bufs[parity], vb, *state, groups[g], heads, diag=diag)

    def run(blocks, ahead):
        nxt = [(kb, allowed) for kb, _, _, allowed in blocks] + list(ahead)
        for u in range(len(blocks) * ng):
            t = u + lag
            if t // ng < len(nxt):
                kb, allowed = nxt[t // ng]
                produce(kb, (t // ng) % 2, t % ng, allowed)
            _, vb, diag, _ = blocks[u // ng]
            consume(vb, (u // ng) % 2, u % ng, diag)

    _flash_step(kf, vt_ref[seq // BLOCK], qt, *state, valid, heads,
                between=lambda: [produce(k_block(0), 0, g) for g in range(lag)])

    def body(p, carry):
        run([(k_block(2 * p), vt_block(2 * p), None, chunks), (k_block(2 * p + 1), vt_block(2 * p + 1), None, chunks)],
            [(k_block(2 * p + 2), chunks)])
        return carry

    lax.fori_loop(0, i, body, 0)
    tri_shape = (ATT_CW, ATT_CW)
    tri = jnp.where(lax.broadcasted_iota(jnp.int32, tri_shape, 0) <= lax.broadcasted_iota(jnp.int32, tri_shape, 1),
                    0.0, NEG_INF)
    run([(k_block(2 * i), vt_block(2 * i), (0, tq, tri), chunks),
         (k_block(2 * i + 1), vt_block(2 * i + 1), (tk, tq, tri), late)], [])


def _diff_kernel(q_ref, k_ref, v_ref, lq1_ref, lk1_ref, lq2_ref, lk2_ref, sub_ref, *rest,
                 tq, tk, seq, front, lambda_init, nh):
    if front:
        _, o_ref, *scr = rest
    else:
        o_ref, *scr = rest
    m_ref, l_ref, acc_ref = scr[1:4]

    def stacked_queries():
        t = _transpose(q_ref[0])
        d = lax.broadcasted_iota(jnp.int32, (LANES, tq), 0)
        zero = jnp.zeros((LANES, tq), t.dtype)
        cols = []
        for h in range(nh):
            th = t[h * LANES:(h + 1) * LANES]
            cols += [jnp.where(d < DIFF_HEAD_DIM, th, zero), jnp.where(d >= DIFF_HEAD_DIM, th, zero)]
        return jnp.concatenate(cols, axis=1)

    _causal_attend(stacked_queries, k_ref, v_ref, scr, tq=tq, tk=tk, seq=seq, front=front, nh=nh, lag=DIFF_LAG)
    o = acc_ref[...] / l_ref[...]
    lam = (jnp.exp(jnp.sum(lq1_ref[...] * lk1_ref[...], keepdims=True))
           - jnp.exp(jnp.sum(lq2_ref[...] * lk2_ref[...], keepdims=True)) + lambda_init)
    for h in range(nh):
        w = o[:, 2 * h * tq:(2 * h + 1) * tq] - lam * o[:, (2 * h + 1) * tq:(2 * h + 2) * tq]
        w = w * lax.rsqrt(jnp.mean(w * w, axis=0, keepdims=True) + DIFF_EPS) * sub_ref[...]
        o_ref[0, :, h * LANES:(h + 1) * LANES] = (w * (1.0 - lambda_init)).T.astype(BF16)


def _mla_kernel(q_ref, k_ref, v_ref, *rest, tq, tk, seq, front, nh):
    if front:
        _, o_ref, *scr = rest
    else:
        o_ref, *scr = rest
    m_ref, l_ref, acc_ref = scr[1:4]
    dk = q_ref.shape[2] // nh

    def queries():
        t = _transpose(q_ref[0])
        return jnp.concatenate([t[h * dk:(h + 1) * dk] for h in range(nh)], axis=1)

    _causal_attend(queries, k_ref, v_ref, scr, tq=tq, tk=tk, seq=seq, front=front, nh=nh, lag=MLA_LAG)
    o = acc_ref[...] / l_ref[...]
    for h in range(nh):
        o_ref[0, :, h * LANES:(h + 1) * LANES] = o[:, h * tq:(h + 1) * tq].T.astype(BF16)


def _causal_attention(kernel, q, k, v, extra, *, heads, dk, reps, tq, tk, seq, name, nh=1):
    b, l, _ = q.shape
    dv = LANES * nh
    dk = dk * nh
    reps = reps * nh
    heads = heads // nh
    outs = None
    for front in (False, True):
        t = BLOCK if front else tq
        if front:
            grid = (b, heads, 1)
            qmap = lambda bb, hh, i: (bb, seq // BLOCK, hh)
        else:
            grid = (b, heads, seq // t)
            qmap = lambda bb, hh, i: (bb, i, hh)
        if front:
            kvl, kvmap = BLOCK, qmap
        else:
            kvl, kvmap = l, lambda bb, hh, i: (bb, 0, hh)
        in_specs = [pl.BlockSpec((1, t, dk), qmap), pl.BlockSpec((1, kvl, dk), kvmap),
                    pl.BlockSpec((1, kvl, dv), kvmap)]
        in_specs += [_resident(e.shape) for e in extra]
        args = [q, k, v, *extra]
        aliases = {}
        if front:
            in_specs.append(pl.BlockSpec(memory_space=pl.ANY))
            aliases = {len(args): 0}
            args.append(outs)
        outs = pl.pallas_call(
            functools.partial(kernel, tq=t, tk=min(t, tk), seq=seq, front=front),
            grid=grid,
            in_specs=in_specs,
            out_specs=pl.BlockSpec((1, t, dv), qmap),
            out_shape=jax.ShapeDtypeStruct((b, l, heads * dv), BF16),
            scratch_shapes=[pltpu.VMEM((1 if front else l // BLOCK, dv, BLOCK), BF16),
                            pltpu.VMEM((1, reps * t), F32), pltpu.VMEM((1, reps * t), F32),
                            pltpu.VMEM((LANES, reps * t), F32)]
                           + [pltpu.VMEM((8 if front else tk, reps * t), F32)] * 2
                           + [pltpu.VMEM((1, reps * t), F32)] * 2,
            input_output_aliases=aliases,
            compiler_params=_cparams(3),
            name=name + ("_front" if front else ""),
        )(*args)
    return outs


SWA_Q_COLS = SWA_HEADS * SWA_HEAD_DIM
SWA_KV_COLS = SWA_KV_HEADS * SWA_HEAD_DIM
SWA_IN_COLS = SWA_Q_COLS + 2 * SWA_KV_COLS


def _swa_proj_kernel(h_ref, g_ref, w_ref, b_ref, ca_ref, sa_ref, q_ref, k_ref, v_ref):
    xn = _rms(h_ref[0], g_ref[...], EPS).astype(BF16)
    ca, sa = ca_ref[...], sa_ref[...]
    half = PART_ROT // 2
    nq = SWA_Q_COLS
    nk = SWA_KV_COLS
    q = _rope(_dot(xn, w_ref[:, 0:nq]) + b_ref[:, 0:nq], ca, sa, half)
    q_ref[0] = (q * (SWA_HEAD_DIM ** -0.5 * LOG2E)).astype(BF16)
    k_ref[0] = _rope(_dot(xn, w_ref[:, nq:nq + nk]) + b_ref[:, nq:nq + nk], ca, sa, half).astype(BF16)
    v_ref[0] = (_dot(xn, w_ref[:, nq + nk:]) + b_ref[:, nq + nk:]).astype(BF16)


def _swa_proj(h3, g, w, bias, tabs):
    b, l, _ = h3.shape
    tm = PROJ_TM
    row = lambda wd: pl.BlockSpec((1, tm, wd), lambda j, bb: (bb, j, 0))
    tab = pl.BlockSpec((tm, LANES), lambda j, bb: (j, 0))
    outw = (SWA_Q_COLS, SWA_KV_COLS, SWA_KV_COLS)
    return pl.pallas_call(
        _swa_proj_kernel,
        grid=(l // tm, b),
        in_specs=[row(D_MODEL), _resident((1, D_MODEL)), _resident((D_MODEL, SWA_IN_COLS)),
                  _resident((1, SWA_IN_COLS)), tab, tab],
        out_specs=[row(w_) for w_ in outw],
        out_shape=[jax.ShapeDtypeStruct((b, l, w_), BF16) for w_ in outw],
        compiler_params=_cparams(2),
        name="swa_proj",
    )(h3, g, w, bias, *tabs)


def _swa_kernel(sinks_ref, q_ref, k_ref, v_ref, *rest, tq, seq, front):
    if front:
        _, o_ref, s_a, s_b = rest
    else:
        o_ref, s_a, s_b = rest
    hd_dim = SWA_HEAD_DIM
    ts = min(SWA_TS, tq)
    pairs = SWA_GROUP // 2
    cw = 2 * ts
    f0 = 0 if front else seq
    meta_k = k_ref[0, f0 + N_FRONT:f0 + BLOCK, :]
    meta_v = v_ref[0, f0 + N_FRONT:f0 + BLOCK, :]
    band = 0 if front else ts + WINDOW
    nk = band + N_META
    nkp = -(-nk // LANES) * LANES
    qt = _transpose(q_ref[0])
    zh = jnp.zeros((hd_dim, cw), BF16)
    key = lax.broadcasted_iota(jnp.int32, (nk, ts), 0)
    qry = lax.broadcasted_iota(jnp.int32, (nk, ts), 1)
    first_half = lax.broadcasted_iota(jnp.int32, (1, cw), 1) < ts

    def keys_of(u):
        if front:
            kc, vc = meta_k, meta_v
            allowed = key + N_FRONT <= qry
        else:
            start = pl.program_id(1) * tq + u * ts
            bs = pl.multiple_of(jnp.maximum(start - WINDOW, 0), BLOCK)
            kc = jnp.concatenate([k_ref[0, pl.ds(bs, band), :], meta_k], axis=0)
            vc = jnp.concatenate([v_ref[0, pl.ds(bs, band), :], meta_v], axis=0)
            dist = (start + qry) - (bs + key)
            allowed = (key >= band) | ((dist >= 0) & (dist < WINDOW))
        bias = jnp.where(allowed, 0.0, NEG_INF)
        vt = _transpose(jnp.concatenate([vc, jnp.zeros((nkp - nk, vc.shape[1]), vc.dtype)], axis=0))
        return kc, vt, jnp.concatenate([bias, bias], axis=1)

    sets = [(u, g) for u in range(tq // ts) for g in range(SWA_KV_HEADS)]
    operands = {}
    sbuf = (s_a, s_b)

    def produce(n, pp):
        u, g = sets[n]
        if u not in operands:
            operands[u] = keys_of(u)
        hd = g * SWA_GROUP + 2 * pp
        qh = jnp.concatenate([qt[(hd + a) * hd_dim:(hd + a + 1) * hd_dim, u * ts:(u + 1) * ts] for a in (0, 1)],
                             axis=1)
        rhs = jnp.concatenate([qh, zh] if g == 0 else [zh, qh], axis=0)
        sbuf[n % 2][:, pp * cw:(pp + 1) * cw] = _dot(operands[u][0], rhs)

    outs = {}

    def consume(n, pp):
        u, g = sets[n]
        _, vt, bias = operands[u]
        hd = g * SWA_GROUP + 2 * pp
        sink = jnp.where(first_half, sinks_ref[hd] * LOG2E, sinks_ref[hd + 1] * LOG2E)
        s = sbuf[n % 2][:, pp * cw:(pp + 1) * cw] + bias
        m = jnp.maximum(jnp.max(s, axis=0, keepdims=True), sink)
        e = jnp.exp2(s - m)
        den = jnp.sum(e, axis=0, keepdims=True) + jnp.exp2(sink - m)
        eb = jnp.concatenate([e.astype(BF16), jnp.zeros((nkp - nk, cw), BF16)], axis=0)
        o = _dot(vt, eb)[g * hd_dim:(g + 1) * hd_dim] / den
        outs[hd, u] = o[:, :ts]
        outs[hd + 1, u] = o[:, ts:]

    for pp in range(pairs):
        produce(0, pp)
    for n in range(1, len(sets)):
        for pp in range(pairs):
            produce(n, pp)
            consume(n - 1, pp)
    for pp in range(pairs):
        consume(len(sets) - 1, pp)
    rows = [jnp.concatenate([outs[hd, u] for u in range(tq // ts)], axis=1) for hd in range(SWA_HEADS)]
    o_ref[0] = jnp.concatenate(rows, axis=0).T.astype(BF16)


def _swa_attention(q, k, v, sinks, *, seq):
    b, l, _ = q.shape
    nq = SWA_Q_COLS
    outs = None
    for front in (False, True):
        tq = BLOCK if front else SWA_TQ
        ts = min(SWA_TS, tq)
        nk = N_META if front else ts + WINDOW + N_META
        if front:
            grid = (b, 1)
            qmap = lambda bb, i: (bb, seq // BLOCK, 0)
            kvl, kvmap = BLOCK, qmap
        else:
            grid = (b, seq // tq)
            qmap = lambda bb, i: (bb, i, 0)
            kvl, kvmap = l, lambda bb, i: (bb, 0, 0)
        in_specs = [pl.BlockSpec(memory_space=pltpu.SMEM), pl.BlockSpec((1, tq, nq), qmap),
                    pl.BlockSpec((1, kvl, k.shape[2]), kvmap), pl.BlockSpec((1, kvl, v.shape[2]), kvmap)]
        args = [sinks, q, k, v]
        aliases = {}
        if front:
            in_specs.append(pl.BlockSpec(memory_space=pl.ANY))
            aliases = {len(args): 0}
            args.append(outs)
        outs = pl.pallas_call(
            functools.partial(_swa_kernel, tq=tq, seq=seq, front=front),
            grid=grid,
            in_specs=in_specs,
            out_specs=pl.BlockSpec((1, tq, nq), qmap),
            out_shape=jax.ShapeDtypeStruct((b, l, nq), BF16),
            scratch_shapes=[pltpu.VMEM((nk, SWA_GROUP * ts), F32)] * 2,
            input_output_aliases=aliases,
            compiler_params=_cparams(2),
            name="swa_attn" + ("_front" if front else ""),
        )(*args)
    return outs


def _ab_weights(w_in, w_uq, w_ukv):
    win = jnp.pad(w_in, ((0, 0), (0, AB_IN_PAD - w_in.shape[1]))).astype(BF16)
    wq = w_uq.reshape(MLA_Q_RANK, MLA_HEADS, MLA_NOPE + MLA_ROPE)
    wq = jnp.pad(wq, ((0, 0), (0, 0), (0, 2 * LANES - MLA_NOPE - MLA_ROPE))).reshape(MLA_Q_RANK, MLA_HEADS * 2 * LANES)
    wkv = w_ukv.reshape(MLA_KV_RANK, MLA_HEADS, 2, MLA_NOPE).transpose(0, 2, 1, 3).reshape(MLA_KV_RANK, -1)
    return win, wq.astype(BF16), wkv.astype(BF16)


def kernel(x, meta_tokens, ffn1_norm, ffn1_w_gate, ffn1_w_up, ffn1_w_down, mix_norm, ab_w_in, diff_lambda_q1, diff_lambda_k1, diff_lambda_q2, diff_lambda_k2, diff_subln, mla_q_norm, mla_w_uq, mla_kv_norm, mla_w_ukv, ab_w_out, swa_w_qkv, swa_b_qkv, swa_sinks, swa_w_out, swa_b_out, ffn2_norm, ffn2_w_gate, ffn2_w_up, ffn2_w_down, final_norm):
    b, seq, d = x.shape
    depth = ffn1_norm.shape[0]
    l = seq + BLOCK
    n = b * l
    front = jnp.concatenate([jnp.zeros((N_FRONT, d), x.dtype), meta_tokens.astype(x.dtype)], axis=0)

    pos = np.concatenate([np.arange(seq) + N_META, np.maximum(np.arange(BLOCK) - N_FRONT, 0)])
    tabs_p = _rope_tables(pos, PART_ROT)
    tabs_m = _rope_tables(pos, MLA_ROPE)
    row2 = lambda a: a.reshape(1, -1)

    ffn1 = (ffn1_w_gate, ffn1_w_up, ffn1_w_down.astype(BF16))
    ffn2 = (ffn2_w_gate, ffn2_w_up, ffn2_w_down.astype(BF16))
    out = None
    for ly in range(depth):
        w1 = (row2(ffn1_norm[ly]), *ffn1)
        h = _ffn_first(x, front, *w1).reshape(n, d) if ly == 0 else _ffn(h, *w1, ly)
        h3 = h.reshape(b, l, d)
        if ly % 2 == 0:
            e = ly // 2
            lambda_init = 0.8 - 0.6 * math.exp(-0.3 * ly)
            win, wuq, wukv = _ab_weights(ab_w_in[e], mla_w_uq[e], mla_w_ukv[e])
            qa, ka, va, qb, kb, vb = _ab_proj(h3, row2(mix_norm[ly]), win, row2(mla_q_norm[e]), wuq,
                                              row2(mla_kv_norm[e]), wukv, tabs_p + tabs_m)
            extra = [row2(diff_lambda_q1[e]), row2(diff_lambda_k1[e]), row2(diff_lambda_q2[e]),
                     row2(diff_lambda_k2[e]), diff_subln[e].reshape(-1, 1)]
            oa = _causal_attention(functools.partial(_diff_kernel, lambda_init=lambda_init, nh=DIFF_HEADS_PER_STEP),
                                   qa, ka, va, extra, heads=DIFF_HEADS, dk=LANES, reps=2, tq=ATT_TQ, tk=ATT_TK,
                                   seq=seq, name="diff_attn", nh=DIFF_HEADS_PER_STEP)
            ob = _causal_attention(functools.partial(_mla_kernel, nh=MLA_HEADS_PER_STEP), qb, kb, vb, [],
                                   heads=MLA_HEADS, dk=2 * LANES, reps=1, tq=ATT_TQ, tk=ATT_TK, seq=seq,
                                   name="mla_attn", nh=MLA_HEADS_PER_STEP)
            wo = ab_w_out[e].astype(BF16)
            mix, bias = [(oa, wo[:512]), (ob, wo[512:])], None
        else:
            o = ly // 2
            q, k, v = _swa_proj(h3, row2(mix_norm[ly]), swa_w_qkv[o].astype(BF16), row2(swa_b_qkv[o]), tabs_p)
            att = _swa_attention(q, k, v, swa_sinks[o], seq=seq)
            mix, bias = [(att, swa_w_out[o].astype(BF16))], row2(swa_b_out[o])
        w2 = (row2(ffn2_norm[ly]), *ffn2)
        if ly == depth - 1:
            out = _ffn_final(h3, *w2, ly, row2(final_norm), seq, mix=mix, bias=bias)
        else:
            h = _ffn(h, *w2, ly, mix=[(a.reshape(n, -1), w) for a, w in mix], bias=bias)
    return out
```

```python
import functools
import math

import numpy as np
import jax
import jax.numpy as jnp
from jax import lax
from jax.experimental import pallas as pl
from jax.experimental.pallas import tpu as pltpu

F32 = jnp.float32
BF16 = jnp.bfloat16

D_MODEL = 1024
N_META = 16
BLOCK = 128
N_FRONT = BLOCK - N_META
ROPE_THETA = 500000.0
EPS = 1e-6
NEG_INF = -1e30
D_FF = 2816

DIFF_HEADS = 4
DIFF_HEAD_DIM = 64
DIFF_V_DIM = 2 * DIFF_HEAD_DIM
DIFF_EPS = 1e-5
MLA_HEADS = 4
MLA_NOPE = 128
MLA_ROPE = 64
MLA_V = 128
MLA_Q_RANK = 256
MLA_KV_RANK = 256
SWA_HEADS = 16
SWA_KV_HEADS = 2
SWA_GROUP = SWA_HEADS // SWA_KV_HEADS
SWA_HEAD_DIM = 64
WINDOW = 128

PART_ROT = DIFF_HEAD_DIM // 4
LANES = 128
VMEM_LIMIT = 56 * 1024 * 1024

FFN_TM = 640
FFN_FC = 256
PROJ_TM = 640
FFN_FINAL_TM = 512
ATT_TK = 512
ATT_TQ = 2 * ATT_TK
ATT_CW = 256
DIFF_HEADS_PER_STEP = 2
MLA_HEADS_PER_STEP = 2
DIFF_LAG = 2
MLA_LAG = 3
ATT_GROUP = 2
VT_FILL_UNROLL = 5
LOG2E = 1.4426950408889634
SWA_TQ = 1024
SWA_TS = 128


def _cparams(n_axes):
    return pltpu.CompilerParams(dimension_semantics=("arbitrary",) * n_axes,
                                vmem_limit_bytes=VMEM_LIMIT)


def _resident(shape):
    nd = len(shape)
    return pl.BlockSpec(shape, lambda *_: (0,) * nd, pipeline_mode=pl.Buffered(1))


def _rms(x, g, eps):
    return x * lax.rsqrt(jnp.mean(x * x, axis=-1, keepdims=True) + eps) * g


def _dot(a, b):
    return jnp.dot(a, b, preferred_element_type=F32)


def _dot_nt(a, b):
    return lax.dot_general(a, b, (((1,), (1,)), ((), ())), preferred_element_type=F32)


def _ffn_kernel(x_ref, *refs, final=False, aliased=False, n_mix=0, mix_bias=False):
    mix, refs = refs[:2 * n_mix], refs[2 * n_mix:]
    if mix_bias:
        mb_ref, refs = refs[0], refs[1:]
    g_ref, wg_ref, wu_ref, wd_ref, *rest = refs
    if final:
        fg_ref, o_ref, act_ref = rest
    elif aliased:
        _, o_ref, act_ref = rest
    else:
        o_ref, act_ref = rest
    x = x_ref[...].reshape(x_ref.shape[-2:])
    for a_ref, w_ref in zip(mix[0::2], mix[1::2]):
        x = x + _dot(a_ref[...].reshape(a_ref.shape[-2:]), w_ref[...])
    if mix_bias:
        x = x + mb_ref[...]
    xn = _rms(x, g_ref[...], EPS).astype(BF16)
    for c in range(D_FF // FFN_FC):
        sl = slice(c * FFN_FC, (c + 1) * FFN_FC)
        g = _dot(xn, wg_ref[:, sl].astype(BF16))
        u = _dot(xn, wu_ref[:, sl].astype(BF16))
        act_ref[:, sl] = (g * (1.0 / (1.0 + jnp.exp(-g))) * u).astype(BF16)
    y = x + 0.5 * _dot(act_ref[...], wd_ref[...])
    if final:
        y = _rms(y, fg_ref[...], EPS)
    o_ref[...] = y.reshape(o_ref.shape)


def _mix_args(mix, bias, row):
    args, specs = [], []
    for a, w in mix:
        args += [a, w]
        specs += [row(a.shape[-1]), _resident(w.shape)]
    if bias is not None:
        args.append(bias)
        specs.append(_resident(bias.shape))
    return args, specs, dict(n_mix=len(mix), mix_bias=bias is not None)


def _ffn_wspecs(ly):
    mat = lambda r, c: pl.BlockSpec((None, r, c), lambda *_: (ly, 0, 0), pipeline_mode=pl.Buffered(1))
    return [_resident((1, D_MODEL)), mat(D_MODEL, D_FF), mat(D_MODEL, D_FF), mat(D_FF, D_MODEL)]


def _ffn(h, g, wg, wu, wd, ly, mix=(), bias=None):
    n = h.shape[0]
    tm = FFN_TM
    assert n % tm == 0
    row = lambda w: pl.BlockSpec((tm, w), lambda i: (i, 0))
    margs, mspecs, mkw = _mix_args(mix, bias, row)
    return pl.pallas_call(
        functools.partial(_ffn_kernel, **mkw),
        grid=(n // tm,),
        in_specs=[row(D_MODEL)] + mspecs + _ffn_wspecs(ly),
        out_specs=row(D_MODEL),
        out_shape=jax.ShapeDtypeStruct((n, D_MODEL), F32),
        scratch_shapes=[pltpu.VMEM((tm, D_FF), BF16)],
        compiler_params=_cparams(1),
        name="ffn",
    )(h, *margs, g, wg, wu, wd)


def _ffn_first(x, front, g, wg, wu, wd):
    b, seq, d = x.shape
    tm = FFN_FINAL_TM
    assert seq % tm == 0
    weights = _ffn_wspecs(0)
    row = pl.BlockSpec((1, tm, d), lambda bb, i: (bb, i, 0))
    h3 = pl.pallas_call(
        _ffn_kernel,
        grid=(b, seq // tm),
        in_specs=[row] + weights,
        out_specs=row,
        out_shape=jax.ShapeDtypeStruct((b, seq + BLOCK, d), F32),
        scratch_shapes=[pltpu.VMEM((tm, D_FF), BF16)],
        compiler_params=_cparams(2),
        name="ffn_first",
    )(x, g, wg, wu, wd)
    return pl.pallas_call(
        functools.partial(_ffn_kernel, aliased=True),
        grid=(b,),
        in_specs=[pl.BlockSpec((BLOCK, d), lambda bb: (0, 0))] + weights + [pl.BlockSpec(memory_space=pl.ANY)],
        out_specs=pl.BlockSpec((1, BLOCK, d), lambda bb: (bb, seq // BLOCK, 0)),
        out_shape=jax.ShapeDtypeStruct(h3.shape, F32),
        scratch_shapes=[pltpu.VMEM((BLOCK, D_FF), BF16)],
        input_output_aliases={5: 0},
        compiler_params=_cparams(1),
        name="ffn_first_front",
    )(front, g, wg, wu, wd, h3)


def _ffn_final(h3, g, wg, wu, wd, ly, fg, seq, mix=(), bias=None):
    b = h3.shape[0]
    tm = FFN_FINAL_TM
    assert seq % tm == 0
    row = lambda w: pl.BlockSpec((1, tm, w), lambda bb, i: (bb, i, 0))
    margs, mspecs, mkw = _mix_args(mix, bias, row)
    return pl.pallas_call(
        functools.partial(_ffn_kernel, final=True, **mkw),
        grid=(b, seq // tm),
        in_specs=[row(D_MODEL)] + mspecs + _ffn_wspecs(ly) + [_resident((1, D_MODEL))],
        out_specs=row(D_MODEL),
        out_shape=jax.ShapeDtypeStruct((b, seq, D_MODEL), F32),
        scratch_shapes=[pltpu.VMEM((tm, D_FF), BF16)],
        compiler_params=_cparams(2),
        name="ffn_final",
    )(h3, *margs, g, wg, wu, wd, fg)


def _rope(y, c, s, half):
    w = y.shape[1]
    reps = w // LANES
    if reps > 1:
        c = jnp.concatenate([c] * reps, axis=1)
        s = jnp.concatenate([s] * reps, axis=1)
    lane = lax.broadcasted_iota(jnp.int32, y.shape, 1)
    first = (lane & 63) < half
    partner = jnp.where(first, pltpu.roll(y, w - half, 1), pltpu.roll(y, half, 1))
    return y * c + partner * s


def _rope_tables(pos, rot_dim):
    half = rot_dim // 2
    g = np.arange(LANES) & 63
    inv = np.power(np.float32(ROPE_THETA), -(2 * (g % half)).astype(np.float32) / np.float32(rot_dim))
    ang = pos.astype(np.float32)[:, None] * inv[None, :].astype(np.float32)
    rotated = (g < rot_dim)[None, :]
    cos = np.where(rotated, np.cos(ang), np.float32(1.0)).astype(np.float32)
    sin = np.where(rotated, np.where((g < half)[None, :], -np.sin(ang), np.sin(ang)), np.float32(0.0))
    return jnp.asarray(cos), jnp.asarray(sin.astype(np.float32))


AB_IN_PAD = 3 * 512 + MLA_Q_RANK + MLA_KV_RANK + LANES


def _ab_proj_kernel(h_ref, g_ref, win_ref, qn_ref, wuq_ref, kvn_ref, wukv_ref, ca_ref, sa_ref, cm_ref, sm_ref,
                    qa_ref, ka_ref, va_ref, qb_ref, kb_ref, vb_ref):
    xn = _rms(h_ref[0], g_ref[...], EPS).astype(BF16)
    ca, sa, cm, sm = ca_ref[...], sa_ref[...], cm_ref[...], sm_ref[...]
    half_a = PART_ROT // 2
    half_m = MLA_ROPE // 2
    qa = _rope(_dot(xn, win_ref[:, 0:512]), ca, sa, half_a)
    qa_ref[0] = (qa * (DIFF_HEAD_DIM ** -0.5 * LOG2E)).astype(BF16)
    ka_ref[0] = _rope(_dot(xn, win_ref[:, 512:1024]), ca, sa, half_a).astype(BF16)
    va_ref[0] = _dot(xn, win_ref[:, 1024:1536]).astype(BF16)
    cq = _dot(xn, win_ref[:, 1536:1792])
    ckv = _dot(xn, win_ref[:, 1792:2048])
    kr = _rope(_dot(xn, win_ref[:, 2048:2176]), cm, sm, half_m).astype(BF16)
    qb = _dot(_rms(cq, qn_ref[...], EPS).astype(BF16), wuq_ref[...])
    scale = (MLA_NOPE + MLA_ROPE) ** -0.5 * LOG2E
    for hh in range(MLA_HEADS):
        o = 2 * LANES * hh
        qb_ref[0, :, o:o + LANES] = (qb[:, o:o + LANES] * scale).astype(BF16)
        qb_ref[0, :, o + LANES:o + 2 * LANES] = (_rope(qb[:, o + LANES:o + 2 * LANES], cm, sm, half_m) * scale).astype(BF16)
    kv = _dot(_rms(ckv, kvn_ref[...], EPS).astype(BF16), wukv_ref[...])
    for hh in range(MLA_HEADS):
        o = 2 * LANES * hh
        kb_ref[0, :, o:o + LANES] = kv[:, LANES * hh:LANES * (hh + 1)].astype(BF16)
        kb_ref[0, :, o + LANES:o + 2 * LANES] = kr
    vb_ref[0] = kv[:, MLA_HEADS * MLA_NOPE:].astype(BF16)


def _ab_proj(h3, g, win, qn, wuq, kvn, wukv, tabs):
    b, l, _ = h3.shape
    tm = PROJ_TM
    assert l % tm == 0
    row = lambda w: pl.BlockSpec((1, tm, w), lambda j, bb: (bb, j, 0))
    tab = pl.BlockSpec((tm, LANES), lambda j, bb: (j, 0))
    outw = (512, 512, 512, 1024, 1024, 512)
    return pl.pallas_call(
        _ab_proj_kernel,
        grid=(l // tm, b),
        in_specs=[row(D_MODEL), _resident((1, D_MODEL)), _resident((D_MODEL, AB_IN_PAD)),
                  _resident((1, MLA_Q_RANK)), _resident((MLA_Q_RANK, 1024)),
                  _resident((1, MLA_KV_RANK)), _resident((MLA_KV_RANK, 1024)), tab, tab, tab, tab],
        out_specs=[row(w) for w in outw],
        out_shape=[jax.ShapeDtypeStruct((b, l, w), BF16) for w in outw],
        compiler_params=_cparams(2),
        name="ab_proj",
    )(h3, g, win, qn, wuq, kvn, wukv, *tabs)


def _transpose(x):
    return x.T


def _head_slices(c0, heads):
    cph, dk, dv = heads
    h = c0 // cph
    return slice(h * dk, (h + 1) * dk), slice(h * dv, (h + 1) * dv)


def _flash_step(k, vt, qt, m_ref, l_ref, acc_ref, mask, heads, between=None):
    cw = min(ATT_CW, heads[0])
    probs = {}
    for c0 in range(0, qt.shape[1], cw):
        cs = slice(c0, c0 + cw)
        ks, _ = _head_slices(c0, heads)
        s = _dot(k[:, ks], qt[:, cs])
        if mask is not None:
            s = jnp.where(mask(c0, s.shape), s, NEG_INF)
        m_new = jnp.max(s, axis=0, keepdims=True)
        p = jnp.exp2(s - m_new)
        l_ref[:, cs] = jnp.sum(p, axis=0, keepdims=True)
        m_ref[:, cs] = m_new
        probs[c0] = p.astype(BF16)
    if between is not None:
        between()
    for c0, p in probs.items():
        acc_ref[:, c0:c0 + cw] = _dot(vt[_head_slices(c0, heads)[1]], p)


def _produce(k, qt, s_ref, x_ref, chunks, heads):
    for c0, rows in chunks.items():
        cs = slice(c0, c0 + ATT_CW)
        s = _dot(k[:rows, _head_slices(c0, heads)[0]], qt[:, cs])
        s_ref[:rows, cs] = s
        if rows == k.shape[0]:
            x_ref[:, cs] = jnp.max(s, axis=0, keepdims=True)


def _consume(s_ref, x_ref, vt, m_ref, l_ref, acc_ref, chunks, heads, diag=None):
    tk = s_ref.shape[0]
    for c0 in chunks:
        cs = slice(c0, c0 + ATT_CW)
        if diag is None:
            seen = tk
        else:
            key0, tq, tri = diag
            seen = min((c0 & (tq - 1)) - key0, tk)
            if seen + ATT_CW <= 0:
                continue
        if seen == tk:
            s = s_ref[:, cs]
            smax = x_ref[:, cs]
        else:
            s = s_ref[seen:seen + ATT_CW, cs] + tri
            if seen > 0:
                s = jnp.concatenate([s_ref[0:seen, cs], s], axis=0)
            smax = jnp.max(s, axis=0, keepdims=True)
        m_prev = m_ref[:, cs]
        m_new = jnp.maximum(m_prev, smax)
        alpha = jnp.exp2(m_prev - m_new)
        p = jnp.exp2(s - m_new)
        l_ref[:, cs] = alpha * l_ref[:, cs] + jnp.sum(p, axis=0, keepdims=True)
        acc_ref[:, cs] = alpha * acc_ref[:, cs] + _dot(vt[_head_slices(c0, heads)[1], :s.shape[0]], p.astype(BF16))
        m_ref[:, cs] = m_new


def _causal_attend(make_qt, k_ref, v_ref, scr, *, tq, tk, seq, front, nh, lag):
    vt_ref, m_ref, l_ref, acc_ref, s0, s1, x0, x1 = scr
    state = (m_ref, l_ref, acc_ref)
    heads = (m_ref.shape[1] // nh, k_ref.shape[2] // nh, v_ref.shape[2] // nh)
    f0 = 0 if front else seq
    kf = k_ref[0, f0:f0 + BLOCK, :]

    def valid(c0, shape):
        return lax.broadcasted_iota(jnp.int32, shape, 0) >= N_FRONT

    def causal(key0):
        def mask(c0, shape):
            key = lax.broadcasted_iota(jnp.int32, shape, 0) + key0
            qry = (lax.broadcasted_iota(jnp.int32, shape, 1) + c0) & (tq - 1)
            return key <= qry
        return mask

    if front:
        qt = make_qt()
        both = lambda c0, shape: valid(c0, shape) & causal(0)(c0, shape)
        _flash_step(kf, _transpose(v_ref[0, f0:f0 + BLOCK, :]), qt, *state, both, heads)
        return
    assert tq == 2 * tk
    i = pl.program_id(2)
    nblk = v_ref.shape[1] // BLOCK
    sub = tk // BLOCK
    chunks = list(range(0, m_ref.shape[1], ATT_CW))

    @pl.when(i == 0)
    def _fill():
        def fill(c, carry):
            st = pl.multiple_of(c * BLOCK, BLOCK)
            vt_ref[c] = _transpose(v_ref[0, pl.ds(st, BLOCK), :])
            return carry
        lax.fori_loop(0, nblk, fill, 0, unroll=VT_FILL_UNROLL)

    qt = make_qt()

    def k_block(j):
        return k_ref[0, pl.ds(pl.multiple_of(j * tk, tk), tk), :]

    def vt_block(j):
        return jnp.concatenate([vt_ref[j * sub + u] for u in range(sub)], axis=1)

    groups = [chunks[g0:g0 + ATT_GROUP] for g0 in range(0, len(chunks), ATT_GROUP)]
    ng = len(groups)
    lag = min(lag, ng)
    bufs = ((s0, x0), (s1, x1))

    every = {c0: tk for c0 in chunks}

    def produce(kb, parity, g, allowed=every):
        _produce(kb, qt, *bufs[parity], {c0: allowed[c0] for c0 in groups[g] if c0 in allowed}, heads)

    def consume(vb, parity, g, diag=None):
        _consume(*bufs[parity], vb, *state, groups[g], heads, diag=diag)

    def run(blocks, ahead):
        nxt = [(kb, allowed) for kb, _, _, allowed in blocks] + list(ahead)
        for u in range(len(blocks) * ng):
            t = u + lag
            if t // ng < len(nxt):
                kb, allowed = nxt[t // ng]
                produce(kb, (t // ng) % 2, t % ng, allowed)
            _, vb, diag, _ = blocks[u // ng]
            consume(vb, (u // ng) % 2, u % ng, diag)

    meta = slice(seq + N_FRONT, seq + BLOCK)
    _flash_step(k_ref[0, meta, :], _transpose(v_ref[0, meta, :]), qt, *state, None, heads,
                between=lambda: [produce(k_block(0), 0, g) for g in range(lag)])

    def body(p, carry):
        run([(k_block(2 * p), vt_block(2 * p), None, every), (k_block(2 * p + 1), vt_block(2 * p + 1), None, every)],
            [(k_block(2 * p + 2), every)])
        return carry

    lax.fori_loop(0, i, body, 0)
    tri_shape = (ATT_CW, ATT_CW)
    tri = jnp.where(lax.broadcasted_iota(jnp.int32, tri_shape, 0) <= lax.broadcasted_iota(jnp.int32, tri_shape, 1),
                    0.0, NEG_INF)
    seen = lambda key0: {c0: min((c0 & (tq - 1)) + ATT_CW - key0, tk) for c0 in chunks
                         if (c0 & (tq - 1)) + ATT_CW > key0}
    run([(k_block(2 * i), vt_block(2 * i), (0, tq, tri), seen(0)),
         (k_block(2 * i + 1), vt_block(2 * i + 1), (tk, tq, tri), seen(tk))], [])


def _diff_kernel(q_ref, k_ref, v_ref, lq1_ref, lk1_ref, lq2_ref, lk2_ref, sub_ref, *rest,
                 tq, tk, seq, front, lambda_init, nh):
    if front:
        _, o_ref, *scr = rest
    else:
        o_ref, *scr = rest
    m_ref, l_ref, acc_ref = scr[1:4]

    def stacked_queries():
        t = _transpose(q_ref[0])
        d = lax.broadcasted_iota(jnp.int32, (LANES, tq), 0)
        zero = jnp.zeros((LANES, tq), t.dtype)
        cols = []
        for h in range(nh):
            th = t[h * LANES:(h + 1) * LANES]
            cols += [jnp.where(d < DIFF_HEAD_DIM, th, zero), jnp.where(d >= DIFF_HEAD_DIM, th, zero)]
        return jnp.concatenate(cols, axis=1)

    _causal_attend(stacked_queries, k_ref, v_ref, scr, tq=tq, tk=tk, seq=seq, front=front, nh=nh, lag=DIFF_LAG)
    o = acc_ref[...] / l_ref[...]
    lam = (jnp.exp(jnp.sum(lq1_ref[...] * lk1_ref[...], keepdims=True))
           - jnp.exp(jnp.sum(lq2_ref[...] * lk2_ref[...], keepdims=True)) + lambda_init)
    for h in range(nh):
        w = o[:, 2 * h * tq:(2 * h + 1) * tq] - lam * o[:, (2 * h + 1) * tq:(2 * h + 2) * tq]
        w = w * lax.rsqrt(jnp.mean(w * w, axis=0, keepdims=True) + DIFF_EPS) * sub_ref[...]
        o_ref[0, :, h * LANES:(h + 1) * LANES] = (w * (1.0 - lambda_init)).T.astype(BF16)


def _mla_kernel(q_ref, k_ref, v_ref, *rest, tq, tk, seq, front, nh):
    if front:
        _, o_ref, *scr = rest
    else:
        o_ref, *scr = rest
    m_ref, l_ref, acc_ref = scr[1:4]
    dk = q_ref.shape[2] // nh

    def queries():
        t = _transpose(q_ref[0])
        return jnp.concatenate([t[h * dk:(h + 1) * dk] for h in range(nh)], axis=1)

    _causal_attend(queries, k_ref, v_ref, scr, tq=tq, tk=tk, seq=seq, front=front, nh=nh, lag=MLA_LAG)
    o = acc_ref[...] / l_ref[...]
    for h in range(nh):
        o_ref[0, :, h * LANES:(h + 1) * LANES] = o[:, h * tq:(h + 1) * tq].T.astype(BF16)


def _causal_attention(kernel, q, k, v, extra, *, heads, dk, reps, tq, tk, seq, name, nh=1):
    b, l, _ = q.shape
    dv = LANES * nh
    dk = dk * nh
    reps = reps * nh
    heads = heads // nh
    outs = None
    for front in (False, True):
        t = BLOCK if front else tq
        if front:
            grid = (b, heads, 1)
            qmap = lambda bb, hh, i: (bb, seq // BLOCK, hh)
        else:
            grid = (b, heads, seq // t)
            qmap = lambda bb, hh, i: (bb, i, hh)
        if front:
            kvl, kvmap = BLOCK, qmap
        else:
            kvl, kvmap = l, lambda bb, hh, i: (bb, 0, hh)
        in_specs = [pl.BlockSpec((1, t, dk), qmap), pl.BlockSpec((1, kvl, dk), kvmap),
                    pl.BlockSpec((1, kvl, dv), kvmap)]
        in_specs += [_resident(e.shape) for e in extra]
        args = [q, k, v, *extra]
        aliases = {}
        if front:
            in_specs.append(pl.BlockSpec(memory_space=pl.ANY))
            aliases = {len(args): 0}
            args.append(outs)
        outs = pl.pallas_call(
            functools.partial(kernel, tq=t, tk=min(t, tk), seq=seq, front=front),
            grid=grid,
            in_specs=in_specs,
            out_specs=pl.BlockSpec((1, t, dv), qmap),
            out_shape=jax.ShapeDtypeStruct((b, l, heads * dv), BF16),
            scratch_shapes=[pltpu.VMEM((1 if front else l // BLOCK, dv, BLOCK), BF16),
                            pltpu.VMEM((1, reps * t), F32), pltpu.VMEM((1, reps * t), F32),
                            pltpu.VMEM((LANES, reps * t), F32)]
                           + [pltpu.VMEM((8 if front else tk, reps * t), F32)] * 2
                           + [pltpu.VMEM((1, reps * t), F32)] * 2,
            input_output_aliases=aliases,
            compiler_params=_cparams(3),
            name=name + ("_front" if front else ""),
        )(*args)
    return outs


SWA_Q_COLS = SWA_HEADS * SWA_HEAD_DIM
SWA_KV_COLS = SWA_KV_HEADS * SWA_HEAD_DIM
SWA_IN_COLS = SWA_Q_COLS + 2 * SWA_KV_COLS


def _swa_proj_kernel(h_ref, g_ref, w_ref, b_ref, ca_ref, sa_ref, q_ref, k_ref, v_ref):
    xn = _rms(h_ref[0], g_ref[...], EPS).astype(BF16)
    ca, sa = ca_ref[...], sa_ref[...]
    half = PART_ROT // 2
    nq = SWA_Q_COLS
    nk = SWA_KV_COLS
    q = _rope(_dot(xn, w_ref[:, 0:nq]) + b_ref[:, 0:nq], ca, sa, half)
    q_ref[0] = (q * (SWA_HEAD_DIM ** -0.5 * LOG2E)).astype(BF16)
    k_ref[0] = _rope(_dot(xn, w_ref[:, nq:nq + nk]) + b_ref[:, nq:nq + nk], ca, sa, half).astype(BF16)
    v_ref[0] = (_dot(xn, w_ref[:, nq + nk:]) + b_ref[:, nq + nk:]).astype(BF16)


def _swa_proj(h3, g, w, bias, tabs):
    b, l, _ = h3.shape
    tm = PROJ_TM
    row = lambda wd: pl.BlockSpec((1, tm, wd), lambda j, bb: (bb, j, 0))
    tab = pl.BlockSpec((tm, LANES), lambda j, bb: (j, 0))
    outw = (SWA_Q_COLS, SWA_KV_COLS, SWA_KV_COLS)
    return pl.pallas_call(
        _swa_proj_kernel,
        grid=(l // tm, b),
        in_specs=[row(D_MODEL), _resident((1, D_MODEL)), _resident((D_MODEL, SWA_IN_COLS)),
                  _resident((1, SWA_IN_COLS)), tab, tab],
        out_specs=[row(w_) for w_ in outw],
        out_shape=[jax.ShapeDtypeStruct((b, l, w_), BF16) for w_ in outw],
        compiler_params=_cparams(2),
        name="swa_proj",
    )(h3, g, w, bias, *tabs)


def _swa_kernel(sinks_ref, q_ref, k_ref, v_ref, *rest, tq, seq, front):
    if front:
        _, o_ref, s_a, s_b = rest
    else:
        o_ref, s_a, s_b = rest
    hd_dim = SWA_HEAD_DIM
    ts = min(SWA_TS, tq)
    pairs = SWA_GROUP // 2
    cw = 2 * ts
    f0 = 0 if front else seq
    meta_k = k_ref[0, f0 + N_FRONT:f0 + BLOCK, :]
    meta_v = v_ref[0, f0 + N_FRONT:f0 + BLOCK, :]
    band = 0 if front else ts + WINDOW
    nk = band + N_META
    nkp = -(-nk // LANES) * LANES
    qt = _transpose(q_ref[0])
    zh = jnp.zeros((hd_dim, cw), BF16)
    key = lax.broadcasted_iota(jnp.int32, (nk, ts), 0)
    qry = lax.broadcasted_iota(jnp.int32, (nk, ts), 1)
    first_half = lax.broadcasted_iota(jnp.int32, (1, cw), 1) < ts

    def keys_of(u):
        if front:
            kc, vc = meta_k, meta_v
            allowed = key + N_FRONT <= qry
        else:
            start = pl.program_id(1) * tq + u * ts
            bs = pl.multiple_of(jnp.maximum(start - WINDOW, 0), BLOCK)
            kc = jnp.concatenate([k_ref[0, pl.ds(bs, band), :], meta_k], axis=0)
            vc = jnp.concatenate([v_ref[0, pl.ds(bs, band), :], meta_v], axis=0)
            dist = (start + qry) - (bs + key)
            allowed = (key >= band) | ((dist >= 0) & (dist < WINDOW))
        bias = jnp.where(allowed, 0.0, NEG_INF)
        vt = _transpose(jnp.concatenate([vc, jnp.zeros((nkp - nk, vc.shape[1]), vc.dtype)], axis=0))
        return kc, vt, jnp.concatenate([bias, bias], axis=1)

    sets = [(u, g) for u in range(tq // ts) for g in range(SWA_KV_HEADS)]
    operands = {}
    sbuf = (s_a, s_b)

    def produce(n, pp):
        u, g = sets[n]
        if u not in operands:
            operands[u] = keys_of(u)
        hd = g * SWA_GROUP + 2 * pp
        qh = jnp.concatenate([qt[(hd + a) * hd_dim:(hd + a + 1) * hd_dim, u * ts:(u + 1) * ts] for a in (0, 1)],
                             axis=1)
        rhs = jnp.concatenate([qh, zh] if g == 0 else [zh, qh], axis=0)
        sbuf[n % 2][:, pp * cw:(pp + 1) * cw] = _dot(operands[u][0], rhs)

    outs = {}

    def consume(n, pp):
        u, g = sets[n]
        _, vt, bias = operands[u]
        hd = g * SWA_GROUP + 2 * pp
        sink = jnp.where(first_half, sinks_ref[hd] * LOG2E, sinks_ref[hd + 1] * LOG2E)
        s = sbuf[n % 2][:, pp * cw:(pp + 1) * cw] + bias
        m = jnp.maximum(jnp.max(s, axis=0, keepdims=True), sink)
        e = jnp.exp2(s - m)
        den = jnp.sum(e, axis=0, keepdims=True) + jnp.exp2(sink - m)
        eb = jnp.concatenate([e.astype(BF16), jnp.zeros((nkp - nk, cw), BF16)], axis=0)
        o = _dot(vt, eb)[g * hd_dim:(g + 1) * hd_dim] / den
        outs[hd, u] = o[:, :ts]
        outs[hd + 1, u] = o[:, ts:]

    for pp in range(pairs):
        produce(0, pp)
    for n in range(1, len(sets)):
        for pp in range(pairs):
            produce(n, pp)
            consume(n - 1, pp)
    for pp in range(pairs):
        consume(len(sets) - 1, pp)
    rows = [jnp.concatenate([outs[hd, u] for u in range(tq // ts)], axis=1) for hd in range(SWA_HEADS)]
    o_ref[0] = jnp.concatenate(rows, axis=0).T.astype(BF16)


def _swa_attention(q, k, v, sinks, *, seq):
    b, l, _ = q.shape
    nq = SWA_Q_COLS
    outs = None
    for front in (False, True):
        tq = BLOCK if front else SWA_TQ
        ts = min(SWA_TS, tq)
        nk = N_META if front else ts + WINDOW + N_META
        if front:
            grid = (b, 1)
            qmap = lambda bb, i: (bb, seq // BLOCK, 0)
            kvl, kvmap = BLOCK, qmap
        else:
            grid = (b, seq // tq)
            qmap = lambda bb, i: (bb, i, 0)
            kvl, kvmap = l, lambda bb, i: (bb, 0, 0)
        in_specs = [pl.BlockSpec(memory_space=pltpu.SMEM), pl.BlockSpec((1, tq, nq), qmap),
                    pl.BlockSpec((1, kvl, k.shape[2]), kvmap), pl.BlockSpec((1, kvl, v.shape[2]), kvmap)]
        args = [sinks, q, k, v]
        aliases = {}
        if front:
            in_specs.append(pl.BlockSpec(memory_space=pl.ANY))
            aliases = {len(args): 0}
            args.append(outs)
        outs = pl.pallas_call(
            functools.partial(_swa_kernel, tq=tq, seq=seq, front=front),
            grid=grid,
            in_specs=in_specs,
            out_specs=pl.BlockSpec((1, tq, nq), qmap),
            out_shape=jax.ShapeDtypeStruct((b, l, nq), BF16),
            scratch_shapes=[pltpu.VMEM((nk, SWA_GROUP * ts), F32)] * 2,
            input_output_aliases=aliases,
            compiler_params=_cparams(2),
            name="swa_attn" + ("_front" if front else ""),
        )(*args)
    return outs


def _ab_weights(w_in, w_uq, w_ukv):
    win = jnp.pad(w_in, ((0, 0), (0, AB_IN_PAD - w_in.shape[1]))).astype(BF16)
    wq = w_uq.reshape(MLA_Q_RANK, MLA_HEADS, MLA_NOPE + MLA_ROPE)
    wq = jnp.pad(wq, ((0, 0), (0, 0), (0, 2 * LANES - MLA_NOPE - MLA_ROPE))).reshape(MLA_Q_RANK, MLA_HEADS * 2 * LANES)
    wkv = w_ukv.reshape(MLA_KV_RANK, MLA_HEADS, 2, MLA_NOPE).transpose(0, 2, 1, 3).reshape(MLA_KV_RANK, -1)
    return win, wq.astype(BF16), wkv.astype(BF16)


def kernel(x, meta_tokens, ffn1_norm, ffn1_w_gate, ffn1_w_up, ffn1_w_down, mix_norm, ab_w_in, diff_lambda_q1, diff_lambda_k1, diff_lambda_q2, diff_lambda_k2, diff_subln, mla_q_norm, mla_w_uq, mla_kv_norm, mla_w_ukv, ab_w_out, swa_w_qkv, swa_b_qkv, swa_sinks, swa_w_out, swa_b_out, ffn2_norm, ffn2_w_gate, ffn2_w_up, ffn2_w_down, final_norm):
    b, seq, d = x.shape
    depth = ffn1_norm.shape[0]
    l = seq + BLOCK
    n = b * l
    front = jnp.concatenate([jnp.zeros((N_FRONT, d), x.dtype), meta_tokens.astype(x.dtype)], axis=0)

    pos = np.concatenate([np.arange(seq) + N_META, np.maximum(np.arange(BLOCK) - N_FRONT, 0)])
    tabs_p = _rope_tables(pos, PART_ROT)
    tabs_m = _rope_tables(pos, MLA_ROPE)
    row2 = lambda a: a.reshape(1, -1)

    ffn1 = (ffn1_w_gate, ffn1_w_up, ffn1_w_down.astype(BF16))
    ffn2 = (ffn2_w_gate, ffn2_w_up, ffn2_w_down.astype(BF16))
    out = None
    for ly in range(depth):
        w1 = (row2(ffn1_norm[ly]), *ffn1)
        h = _ffn_first(x, front, *w1).reshape(n, d) if ly == 0 else _ffn(h, *w1, ly)
        h3 = h.reshape(b, l, d)
        if ly % 2 == 0:
            e = ly // 2
            lambda_init = 0.8 - 0.6 * math.exp(-0.3 * ly)
            win, wuq, wukv = _ab_weights(ab_w_in[e], mla_w_uq[e], mla_w_ukv[e])
            qa, ka, va, qb, kb, vb = _ab_proj(h3, row2(mix_norm[ly]), win, row2(mla_q_norm[e]), wuq,
                                              row2(mla_kv_norm[e]), wukv, tabs_p + tabs_m)
            extra = [row2(diff_lambda_q1[e]), row2(diff_lambda_k1[e]), row2(diff_lambda_q2[e]),
                     row2(diff_lambda_k2[e]), diff_subln[e].reshape(-1, 1)]
            oa = _causal_attention(functools.partial(_diff_kernel, lambda_init=lambda_init, nh=DIFF_HEADS_PER_STEP),
                                   qa, ka, va, extra, heads=DIFF_HEADS, dk=LANES, reps=2, tq=ATT_TQ, tk=ATT_TK,
                                   seq=seq, name="diff_attn", nh=DIFF_HEADS_PER_STEP)
            ob = _causal_attention(functools.partial(_mla_kernel, nh=MLA_HEADS_PER_STEP), qb, kb, vb, [],
                                   heads=MLA_HEADS, dk=2 * LANES, reps=1, tq=ATT_TQ, tk=ATT_TK, seq=seq,
                                   name="mla_attn", nh=MLA_HEADS_PER_STEP)
            wo = ab_w_out[e].astype(BF16)
            mix, bias = [(oa, wo[:512]), (ob, wo[512:])], None
        else:
            o = ly // 2
            q, k, v = _swa_proj(h3, row2(mix_norm[ly]), swa_w_qkv[o].astype(BF16), row2(swa_b_qkv[o]), tabs_p)
            att = _swa_attention(q, k, v, swa_sinks[o], seq=seq)
            mix, bias = [(att, swa_w_out[o].astype(BF16))], row2(swa_b_out[o])
        w2 = (row2(ffn2_norm[ly]), *ffn2)
        if ly == depth - 1:
            out = _ffn_final(h3, *w2, ly, row2(final_norm), seq, mix=mix, bias=bias)
        else:
            h = _ffn(h, *w2, ly, mix=[(a.reshape(n, -1), w) for a, w in mix], bias=bias)
    return out
```

```python
import functools
import math

import numpy as np
import jax
import jax.numpy as jnp
from jax import lax
from jax.experimental import pallas as pl
from jax.experimental.pallas import tpu as pltpu

F32 = jnp.float32
BF16 = jnp.bfloat16

D_MODEL = 1024
N_META = 16
BLOCK = 128
N_FRONT = BLOCK - N_META
ROPE_THETA = 500000.0
EPS = 1e-6
NEG_INF = -1e30
D_FF = 2816

DIFF_HEADS = 4
DIFF_HEAD_DIM = 64
DIFF_EPS = 1e-5
MLA_HEADS = 4
MLA_NOPE = 128
MLA_ROPE = 64
MLA_Q_RANK = 256
MLA_KV_RANK = 256
SWA_HEADS = 16
SWA_KV_HEADS = 2
SWA_GROUP = SWA_HEADS // SWA_KV_HEADS
SWA_HEAD_DIM = 64
WINDOW = 128

PART_ROT = DIFF_HEAD_DIM // 4
LANES = 128
VMEM_LIMIT = 56 * 1024 * 1024

FFN_TM = 640
FFN_FC = 256
PROJ_TM = 640
FFN_FINAL_TM = 512
ATT_TK = 512
ATT_TQ = 2 * ATT_TK
ATT_CW = 256
DIFF_HEADS_PER_STEP = 2
MLA_HEADS_PER_STEP = 2
DIFF_LAG = 2
MLA_LAG = 3
ATT_GROUP = 2
VT_FILL_UNROLL = 5
LOG2E = 1.4426950408889634
SWA_TQ = 1024
SWA_TS = 128


def _cparams(n_axes):
    return pltpu.CompilerParams(dimension_semantics=("arbitrary",) * n_axes,
                                vmem_limit_bytes=VMEM_LIMIT)


def _resident(shape):
    nd = len(shape)
    return pl.BlockSpec(shape, lambda *_: (0,) * nd, pipeline_mode=pl.Buffered(1))


def _rms(x, g, eps):
    return x * lax.rsqrt(jnp.mean(x * x, axis=-1, keepdims=True) + eps) * g


def _dot(a, b):
    return jnp.dot(a, b, preferred_element_type=F32)


def _ffn_kernel(x_ref, *refs, final=False, aliased=False, n_mix=0, mix_bias=False):
    mix, refs = refs[:2 * n_mix], refs[2 * n_mix:]
    if mix_bias:
        mb_ref, refs = refs[0], refs[1:]
    g_ref, wg_ref, wu_ref, wd_ref, *rest = refs
    if final:
        fg_ref, o_ref, act_ref = rest
    elif aliased:
        _, o_ref, act_ref = rest
    else:
        o_ref, act_ref = rest
    x = x_ref[...].reshape(x_ref.shape[-2:])
    for a_ref, w_ref in zip(mix[0::2], mix[1::2]):
        x = x + _dot(a_ref[...].reshape(a_ref.shape[-2:]), w_ref[...])
    if mix_bias:
        x = x + mb_ref[...]
    xn = _rms(x, g_ref[...], EPS).astype(BF16)
    for c in range(D_FF // FFN_FC):
        sl = slice(c * FFN_FC, (c + 1) * FFN_FC)
        g = _dot(xn, wg_ref[:, sl].astype(BF16))
        u = _dot(xn, wu_ref[:, sl].astype(BF16))
        act_ref[:, sl] = (g * (1.0 / (1.0 + jnp.exp(-g))) * u).astype(BF16)
    y = x + 0.5 * _dot(act_ref[...], wd_ref[...])
    if final:
        y = _rms(y, fg_ref[...], EPS)
    o_ref[...] = jnp.broadcast_to(y, o_ref.shape)


def _mix_args(mix, bias, row):
    args, specs = [], []
    for a, w in mix:
        args += [a, w]
        specs += [row(a.shape[-1]), _resident(w.shape)]
    if bias is not None:
        args.append(bias)
        specs.append(_resident(bias.shape))
    return args, specs, dict(n_mix=len(mix), mix_bias=bias is not None)


def _ffn_wspecs(ly):
    mat = lambda r, c: pl.BlockSpec((None, r, c), lambda *_: (ly, 0, 0), pipeline_mode=pl.Buffered(1))
    return [_resident((1, D_MODEL)), mat(D_MODEL, D_FF), mat(D_MODEL, D_FF), mat(D_FF, D_MODEL)]


def _ffn(h, g, wg, wu, wd, ly, mix=(), bias=None):
    n = h.shape[0]
    tm = FFN_TM
    assert n % tm == 0
    row = lambda w: pl.BlockSpec((tm, w), lambda i: (i, 0))
    margs, mspecs, mkw = _mix_args(mix, bias, row)
    return pl.pallas_call(
        functools.partial(_ffn_kernel, **mkw),
        grid=(n // tm,),
        in_specs=[row(D_MODEL)] + mspecs + _ffn_wspecs(ly),
        out_specs=row(D_MODEL),
        out_shape=jax.ShapeDtypeStruct((n, D_MODEL), F32),
        scratch_shapes=[pltpu.VMEM((tm, D_FF), BF16)],
        compiler_params=_cparams(1),
        name="ffn",
    )(h, *margs, g, wg, wu, wd)


def _ffn_first(x, front, g, wg, wu, wd):
    b, seq, d = x.shape
    tm = FFN_FINAL_TM
    assert seq % tm == 0
    weights = _ffn_wspecs(0)
    row = pl.BlockSpec((1, tm, d), lambda bb, i: (bb, i, 0))
    h3 = pl.pallas_call(
        _ffn_kernel,
        grid=(b, seq // tm),
        in_specs=[row] + weights,
        out_specs=row,
        out_shape=jax.ShapeDtypeStruct((b, seq + BLOCK, d), F32),
        scratch_shapes=[pltpu.VMEM((tm, D_FF), BF16)],
        compiler_params=_cparams(2),
        name="ffn_first",
    )(x, g, wg, wu, wd)
    return pl.pallas_call(
        functools.partial(_ffn_kernel, aliased=True),
        grid=(1,),
        in_specs=[pl.BlockSpec((BLOCK, d), lambda _: (0, 0))] + weights + [pl.BlockSpec(memory_space=pl.ANY)],
        out_specs=pl.BlockSpec((b, BLOCK, d), lambda _: (0, seq // BLOCK, 0)),
        out_shape=jax.ShapeDtypeStruct(h3.shape, F32),
        scratch_shapes=[pltpu.VMEM((BLOCK, D_FF), BF16)],
        input_output_aliases={5: 0},
        compiler_params=_cparams(1),
        name="ffn_first_front",
    )(front, g, wg, wu, wd, h3)


def _ffn_final(h3, g, wg, wu, wd, ly, fg, seq, mix=(), bias=None):
    b = h3.shape[0]
    tm = FFN_FINAL_TM
    assert seq % tm == 0
    row = lambda w: pl.BlockSpec((1, tm, w), lambda bb, i: (bb, i, 0))
    margs, mspecs, mkw = _mix_args(mix, bias, row)
    return pl.pallas_call(
        functools.partial(_ffn_kernel, final=True, **mkw),
        grid=(b, seq // tm),
        in_specs=[row(D_MODEL)] + mspecs + _ffn_wspecs(ly) + [_resident((1, D_MODEL))],
        out_specs=row(D_MODEL),
        out_shape=jax.ShapeDtypeStruct((b, seq, D_MODEL), F32),
        scratch_shapes=[pltpu.VMEM((tm, D_FF), BF16)],
        compiler_params=_cparams(2),
        name="ffn_final",
    )(h3, *margs, g, wg, wu, wd, fg)


def _rope(y, c, s, half):
    w = y.shape[1]
    reps = w // LANES
    if reps > 1:
        c = jnp.concatenate([c] * reps, axis=1)
        s = jnp.concatenate([s] * reps, axis=1)
    lane = lax.broadcasted_iota(jnp.int32, y.shape, 1)
    first = (lane & 63) < half
    partner = jnp.where(first, pltpu.roll(y, w - half, 1), pltpu.roll(y, half, 1))
    return y * c + partner * s


def _rope_tables(pos, rot_dim):
    half = rot_dim // 2
    g = np.arange(LANES) & 63
    inv = np.power(np.float32(ROPE_THETA), -(2 * (g % half)).astype(np.float32) / np.float32(rot_dim))
    ang = pos.astype(np.float32)[:, None] * inv[None, :].astype(np.float32)
    rotated = (g < rot_dim)[None, :]
    cos = np.where(rotated, np.cos(ang), np.float32(1.0)).astype(np.float32)
    sin = np.where(rotated, np.where((g < half)[None, :], -np.sin(ang), np.sin(ang)), np.float32(0.0))
    return jnp.asarray(cos), jnp.asarray(sin.astype(np.float32))


AB_IN_PAD = 3 * 512 + MLA_Q_RANK + MLA_KV_RANK + LANES


def _ab_proj_kernel(h_ref, g_ref, win_ref, qn_ref, wuq_ref, kvn_ref, wukv_ref, ca_ref, sa_ref, cm_ref, sm_ref,
                    qa_ref, ka_ref, va_ref, qb_ref, kb_ref, vb_ref):
    xn = _rms(h_ref[0], g_ref[...], EPS).astype(BF16)
    ca, sa, cm, sm = ca_ref[...], sa_ref[...], cm_ref[...], sm_ref[...]
    half_a = PART_ROT // 2
    half_m = MLA_ROPE // 2
    qa = _rope(_dot(xn, win_ref[:, 0:512]), ca, sa, half_a)
    qa_ref[0] = (qa * (DIFF_HEAD_DIM ** -0.5 * LOG2E)).astype(BF16)
    ka_ref[0] = _rope(_dot(xn, win_ref[:, 512:1024]), ca, sa, half_a).astype(BF16)
    va_ref[0] = _dot(xn, win_ref[:, 1024:1536]).astype(BF16)
    cq = _dot(xn, win_ref[:, 1536:1792])
    ckv = _dot(xn, win_ref[:, 1792:2048])
    kr = _rope(_dot(xn, win_ref[:, 2048:2176]), cm, sm, half_m).astype(BF16)
    qb = _dot(_rms(cq, qn_ref[...], EPS).astype(BF16), wuq_ref[...])
    scale = (MLA_NOPE + MLA_ROPE) ** -0.5 * LOG2E
    for hh in range(MLA_HEADS):
        o = 2 * LANES * hh
        qb_ref[0, :, o:o + LANES] = (qb[:, o:o + LANES] * scale).astype(BF16)
        qb_ref[0, :, o + LANES:o + 2 * LANES] = (_rope(qb[:, o + LANES:o + 2 * LANES], cm, sm, half_m) * scale).astype(BF16)
    kv = _dot(_rms(ckv, kvn_ref[...], EPS).astype(BF16), wukv_ref[...])
    for hh in range(MLA_HEADS):
        o = 2 * LANES * hh
        kb_ref[0, :, o:o + LANES] = kv[:, LANES * hh:LANES * (hh + 1)].astype(BF16)
        kb_ref[0, :, o + LANES:o + 2 * LANES] = kr
    vb_ref[0] = kv[:, MLA_HEADS * MLA_NOPE:].astype(BF16)


def _ab_proj(h3, g, win, qn, wuq, kvn, wukv, tabs):
    b, l, _ = h3.shape
    tm = PROJ_TM
    assert l % tm == 0
    row = lambda w: pl.BlockSpec((1, tm, w), lambda j, bb: (bb, j, 0))
    tab = pl.BlockSpec((tm, LANES), lambda j, bb: (j, 0))
    outw = (512, 512, 512, 1024, 1024, 512)
    return pl.pallas_call(
        _ab_proj_kernel,
        grid=(l // tm, b),
        in_specs=[row(D_MODEL), _resident((1, D_MODEL)), _resident((D_MODEL, AB_IN_PAD)),
                  _resident((1, MLA_Q_RANK)), _resident((MLA_Q_RANK, 1024)),
                  _resident((1, MLA_KV_RANK)), _resident((MLA_KV_RANK, 1024)), tab, tab, tab, tab],
        out_specs=[row(w) for w in outw],
        out_shape=[jax.ShapeDtypeStruct((b, l, w), BF16) for w in outw],
        compiler_params=_cparams(2),
        name="ab_proj",
    )(h3, g, win, qn, wuq, kvn, wukv, *tabs)


def _transpose(x):
    return x.T


def _store(o_ref, cols, val):
    for bb in range(o_ref.shape[0]):
        o_ref[bb, :, cols] = val


def _head_slices(c0, heads):
    cph, dk, dv = heads
    h = c0 // cph
    return slice(h * dk, (h + 1) * dk), slice(h * dv, (h + 1) * dv)


def _flash_step(k, vt, qt, m_ref, l_ref, acc_ref, mask, heads, between=None):
    cw = min(ATT_CW, heads[0])
    probs = {}
    for c0 in range(0, qt.shape[1], cw):
        cs = slice(c0, c0 + cw)
        ks, _ = _head_slices(c0, heads)
        s = _dot(k[:, ks], qt[:, cs])
        if mask is not None:
            s = jnp.where(mask(c0, s.shape), s, NEG_INF)
        m_new = jnp.max(s, axis=0, keepdims=True)
        p = jnp.exp2(s - m_new)
        l_ref[:, cs] = jnp.sum(p, axis=0, keepdims=True)
        m_ref[:, cs] = m_new
        probs[c0] = p.astype(BF16)
    if between is not None:
        between()
    for c0, p in probs.items():
        acc_ref[:, c0:c0 + cw] = _dot(vt[_head_slices(c0, heads)[1]], p)


def _produce(k, qt, s_ref, x_ref, chunks, heads):
    for c0, rows in chunks.items():
        cs = slice(c0, c0 + ATT_CW)
        s = _dot(k[:rows, _head_slices(c0, heads)[0]], qt[:, cs])
        s_ref[:rows, cs] = s
        if rows == k.shape[0]:
            x_ref[:, cs] = jnp.max(s, axis=0, keepdims=True)


def _consume(s_ref, x_ref, vt, m_ref, l_ref, acc_ref, chunks, heads, diag=None):
    tk = s_ref.shape[0]
    for c0 in chunks:
        cs = slice(c0, c0 + ATT_CW)
        if diag is None:
            seen = tk
        else:
            key0, tq, tri = diag
            seen = min((c0 & (tq - 1)) - key0, tk)
            if seen + ATT_CW <= 0:
                continue
        if seen == tk:
            s = s_ref[:, cs]
            smax = x_ref[:, cs]
        else:
            s = s_ref[seen:seen + ATT_CW, cs] + tri
            if seen > 0:
                s = jnp.concatenate([s_ref[0:seen, cs], s], axis=0)
            smax = jnp.max(s, axis=0, keepdims=True)
        m_prev = m_ref[:, cs]
        m_new = jnp.maximum(m_prev, smax)
        alpha = jnp.exp2(m_prev - m_new)
        p = jnp.exp2(s - m_new)
        l_ref[:, cs] = alpha * l_ref[:, cs] + jnp.sum(p, axis=0, keepdims=True)
        acc_ref[:, cs] = alpha * acc_ref[:, cs] + _dot(vt[_head_slices(c0, heads)[1], :s.shape[0]], p.astype(BF16))
        m_ref[:, cs] = m_new


def _causal_attend(make_qt, k_ref, v_ref, scr, *, tq, tk, seq, front, nh, lag):
    vt_ref, m_ref, l_ref, acc_ref, s0, s1, x0, x1 = scr
    state = (m_ref, l_ref, acc_ref)
    heads = (m_ref.shape[1] // nh, k_ref.shape[2] // nh, v_ref.shape[2] // nh)
    f0 = 0 if front else seq
    kf = k_ref[0, f0:f0 + BLOCK, :]

    def valid(c0, shape):
        return lax.broadcasted_iota(jnp.int32, shape, 0) >= N_FRONT

    def causal(key0):
        def mask(c0, shape):
            key = lax.broadcasted_iota(jnp.int32, shape, 0) + key0
            qry = (lax.broadcasted_iota(jnp.int32, shape, 1) + c0) & (tq - 1)
            return key <= qry
        return mask

    if front:
        qt = make_qt()
        both = lambda c0, shape: valid(c0, shape) & causal(0)(c0, shape)
        _flash_step(kf, _transpose(v_ref[0, f0:f0 + BLOCK, :]), qt, *state, both, heads)
        return
    assert tq == 2 * tk
    i = pl.program_id(2)
    nblk = v_ref.shape[1] // BLOCK
    sub = tk // BLOCK
    chunks = list(range(0, m_ref.shape[1], ATT_CW))

    @pl.when(i == 0)
    def _fill():
        def fill(c, carry):
            st = pl.multiple_of(c * BLOCK, BLOCK)
            vt_ref[c] = _transpose(v_ref[0, pl.ds(st, BLOCK), :])
            return carry
        lax.fori_loop(0, nblk, fill, 0, unroll=VT_FILL_UNROLL)

    qt = make_qt()

    def k_block(j):
        return k_ref[0, pl.ds(pl.multiple_of(j * tk, tk), tk), :]

    def vt_block(j):
        return jnp.concatenate([vt_ref[j * sub + u] for u in range(sub)], axis=1)

    groups = [chunks[g0:g0 + ATT_GROUP] for g0 in range(0, len(chunks), ATT_GROUP)]
    ng = len(groups)
    lag = min(lag, ng)
    bufs = ((s0, x0), (s1, x1))

    every = {c0: tk for c0 in chunks}

    def produce(kb, parity, g, allowed=every):
        _produce(kb, qt, *bufs[parity], {c0: allowed[c0] for c0 in groups[g] if c0 in allowed}, heads)

    def consume(vb, parity, g, diag=None):
        _consume(*bufs[parity], vb, *state, groups[g], heads, diag=diag)

    def run(blocks, ahead):
        nxt = [(kb, allowed) for kb, _, _, allowed in blocks] + list(ahead)
        for u in range(len(blocks) * ng):
            t = u + lag
            if t // ng < len(nxt):
                kb, allowed = nxt[t // ng]
                produce(kb, (t // ng) % 2, t % ng, allowed)
            _, vb, diag, _ = blocks[u // ng]
            consume(vb, (u // ng) % 2, u % ng, diag)

    meta = slice(seq + N_FRONT, seq + BLOCK)
    _flash_step(k_ref[0, meta, :], _transpose(v_ref[0, meta, :]), qt, *state, None, heads,
                between=lambda: [produce(k_block(0), 0, g) for g in range(lag)])

    def body(p, carry):
        run([(k_block(2 * p), vt_block(2 * p), None, every), (k_block(2 * p + 1), vt_block(2 * p + 1), None, every)],
            [(k_block(2 * p + 2), every)])
        return carry

    lax.fori_loop(0, i, body, 0)
    tri_shape = (ATT_CW, ATT_CW)
    tri = jnp.where(lax.broadcasted_iota(jnp.int32, tri_shape, 0) <= lax.broadcasted_iota(jnp.int32, tri_shape, 1),
                    0.0, NEG_INF)
    seen = lambda key0: {c0: min((c0 & (tq - 1)) + ATT_CW - key0, tk) for c0 in chunks
                         if (c0 & (tq - 1)) + ATT_CW > key0}
    run([(k_block(2 * i), vt_block(2 * i), (0, tq, tri), seen(0)),
         (k_block(2 * i + 1), vt_block(2 * i + 1), (tk, tq, tri), seen(tk))], [])


def _diff_kernel(q_ref, k_ref, v_ref, lq1_ref, lk1_ref, lq2_ref, lk2_ref, sub_ref, *rest,
                 tq, tk, seq, front, lambda_init, nh):
    if front:
        _, o_ref, *scr = rest
    else:
        o_ref, *scr = rest
    m_ref, l_ref, acc_ref = scr[1:4]

    def stacked_queries():
        t = _transpose(q_ref[0])
        d = lax.broadcasted_iota(jnp.int32, (LANES, tq), 0)
        zero = jnp.zeros((LANES, tq), t.dtype)
        cols = []
        for h in range(nh):
            th = t[h * LANES:(h + 1) * LANES]
            cols += [jnp.where(d < DIFF_HEAD_DIM, th, zero), jnp.where(d >= DIFF_HEAD_DIM, th, zero)]
        return jnp.concatenate(cols, axis=1)

    _causal_attend(stacked_queries, k_ref, v_ref, scr, tq=tq, tk=tk, seq=seq, front=front, nh=nh, lag=DIFF_LAG)
    o = acc_ref[...] / l_ref[...]
    lam = (jnp.exp(jnp.sum(lq1_ref[...] * lk1_ref[...], keepdims=True))
           - jnp.exp(jnp.sum(lq2_ref[...] * lk2_ref[...], keepdims=True)) + lambda_init)
    for h in range(nh):
        w = o[:, 2 * h * tq:(2 * h + 1) * tq] - lam * o[:, (2 * h + 1) * tq:(2 * h + 2) * tq]
        w = w * lax.rsqrt(jnp.mean(w * w, axis=0, keepdims=True) + DIFF_EPS) * sub_ref[...]
        _store(o_ref, slice(h * LANES, (h + 1) * LANES), (w * (1.0 - lambda_init)).T.astype(BF16))


def _mla_kernel(q_ref, k_ref, v_ref, *rest, tq, tk, seq, front, nh):
    if front:
        _, o_ref, *scr = rest
    else:
        o_ref, *scr = rest
    m_ref, l_ref, acc_ref = scr[1:4]
    dk = q_ref.shape[2] // nh

    def queries():
        t = _transpose(q_ref[0])
        return jnp.concatenate([t[h * dk:(h + 1) * dk] for h in range(nh)], axis=1)

    _causal_attend(queries, k_ref, v_ref, scr, tq=tq, tk=tk, seq=seq, front=front, nh=nh, lag=MLA_LAG)
    o = acc_ref[...] / l_ref[...]
    for h in range(nh):
        _store(o_ref, slice(h * LANES, (h + 1) * LANES), o[:, h * tq:(h + 1) * tq].T.astype(BF16))


def _causal_attention(kernel, q, k, v, extra, *, heads, dk, reps, tq, tk, seq, name, nh=1):
    b, l, _ = q.shape
    dv = LANES * nh
    dk = dk * nh
    reps = reps * nh
    heads = heads // nh
    outs = None
    for front in (False, True):
        t = BLOCK if front else tq
        if front:
            grid = (1, heads, 1)
            qmap = lambda bb, hh, i: (0, seq // BLOCK, hh)
        else:
            grid = (b, heads, seq // t)
            qmap = lambda bb, hh, i: (bb, i, hh)
        if front:
            kvl, kvmap = BLOCK, qmap
        else:
            kvl, kvmap = l, lambda bb, hh, i: (bb, 0, hh)
        in_specs = [pl.BlockSpec((1, t, dk), qmap), pl.BlockSpec((1, kvl, dk), kvmap),
                    pl.BlockSpec((1, kvl, dv), kvmap)]
        in_specs += [_resident(e.shape) for e in extra]
        args = [q, k, v, *extra]
        aliases = {}
        if front:
            in_specs.append(pl.BlockSpec(memory_space=pl.ANY))
            aliases = {len(args): 0}
            args.append(outs)
        outs = pl.pallas_call(
            functools.partial(kernel, tq=t, tk=min(t, tk), seq=seq, front=front),
            grid=grid,
            in_specs=in_specs,
            out_specs=pl.BlockSpec((b if front else 1, t, dv), qmap),
            out_shape=jax.ShapeDtypeStruct((b, l, heads * dv), BF16),
            scratch_shapes=[pltpu.VMEM((1 if front else l // BLOCK, dv, BLOCK), BF16),
                            pltpu.VMEM((1, reps * t), F32), pltpu.VMEM((1, reps * t), F32),
                            pltpu.VMEM((LANES, reps * t), F32)]
                           + [pltpu.VMEM((8 if front else tk, reps * t), F32)] * 2
                           + [pltpu.VMEM((1, reps * t), F32)] * 2,
            input_output_aliases=aliases,
            compiler_params=_cparams(3),
            name=name + ("_front" if front else ""),
        )(*args)
    return outs


SWA_Q_COLS = SWA_HEADS * SWA_HEAD_DIM
SWA_KV_COLS = SWA_KV_HEADS * SWA_HEAD_DIM
SWA_IN_COLS = SWA_Q_COLS + 2 * SWA_KV_COLS


def _swa_proj_kernel(h_ref, g_ref, w_ref, b_ref, ca_ref, sa_ref, q_ref, k_ref, v_ref):
    xn = _rms(h_ref[0], g_ref[...], EPS).astype(BF16)
    ca, sa = ca_ref[...], sa_ref[...]
    half = PART_ROT // 2
    nq = SWA_Q_COLS
    nk = SWA_KV_COLS
    q = _rope(_dot(xn, w_ref[:, 0:nq]) + b_ref[:, 0:nq], ca, sa, half)
    q_ref[0] = (q * (SWA_HEAD_DIM ** -0.5 * LOG2E)).astype(BF16)
    k_ref[0] = _rope(_dot(xn, w_ref[:, nq:nq + nk]) + b_ref[:, nq:nq + nk], ca, sa, half).astype(BF16)
    v_ref[0] = (_dot(xn, w_ref[:, nq + nk:]) + b_ref[:, nq + nk:]).astype(BF16)


def _swa_proj(h3, g, w, bias, tabs):
    b, l, _ = h3.shape
    tm = PROJ_TM
    row = lambda wd: pl.BlockSpec((1, tm, wd), lambda j, bb: (bb, j, 0))
    tab = pl.BlockSpec((tm, LANES), lambda j, bb: (j, 0))
    outw = (SWA_Q_COLS, SWA_KV_COLS, SWA_KV_COLS)
    return pl.pallas_call(
        _swa_proj_kernel,
        grid=(l // tm, b),
        in_specs=[row(D_MODEL), _resident((1, D_MODEL)), _resident((D_MODEL, SWA_IN_COLS)),
                  _resident((1, SWA_IN_COLS)), tab, tab],
        out_specs=[row(w_) for w_ in outw],
        out_shape=[jax.ShapeDtypeStruct((b, l, w_), BF16) for w_ in outw],
        compiler_params=_cparams(2),
        name="swa_proj",
    )(h3, g, w, bias, *tabs)


def _swa_kernel(sinks_ref, q_ref, k_ref, v_ref, *rest, tq, seq, front):
    if front:
        _, o_ref, s_a, s_b = rest
    else:
        o_ref, s_a, s_b = rest
    hd_dim = SWA_HEAD_DIM
    ts = min(SWA_TS, tq)
    pairs = SWA_GROUP // 2
    cw = 2 * ts
    f0 = 0 if front else seq
    meta_k = k_ref[0, f0 + N_FRONT:f0 + BLOCK, :]
    meta_v = v_ref[0, f0 + N_FRONT:f0 + BLOCK, :]
    band = 0 if front else ts + WINDOW
    nk = band + N_META
    nkp = -(-nk // LANES) * LANES
    qt = _transpose(q_ref[0])
    zh = jnp.zeros((hd_dim, cw), BF16)
    key = lax.broadcasted_iota(jnp.int32, (nk, ts), 0)
    qry = lax.broadcasted_iota(jnp.int32, (nk, ts), 1)
    first_half = lax.broadcasted_iota(jnp.int32, (1, cw), 1) < ts

    def keys_of(u):
        if front:
            kc, vc = meta_k, meta_v
            allowed = key + N_FRONT <= qry
        else:
            start = pl.program_id(1) * tq + u * ts
            bs = pl.multiple_of(jnp.maximum(start - WINDOW, 0), BLOCK)
            kc = jnp.concatenate([k_ref[0, pl.ds(bs, band), :], meta_k], axis=0)
            vc = jnp.concatenate([v_ref[0, pl.ds(bs, band), :], meta_v], axis=0)
            dist = (start + qry) - (bs + key)
            allowed = (key >= band) | ((dist >= 0) & (dist < WINDOW))
        bias = jnp.where(allowed, 0.0, NEG_INF)
        vt = _transpose(jnp.concatenate([vc, jnp.zeros((nkp - nk, vc.shape[1]), vc.dtype)], axis=0))
        return kc, vt, jnp.concatenate([bias, bias], axis=1)

    sets = [(u, g) for u in range(tq // ts) for g in range(SWA_KV_HEADS)]
    operands = {}
    sbuf = (s_a, s_b)

    def produce(n, pp):
        u, g = sets[n]
        if u not in operands:
            operands[u] = keys_of(u)
        hd = g * SWA_GROUP + 2 * pp
        qh = jnp.concatenate([qt[(hd + a) * hd_dim:(hd + a + 1) * hd_dim, u * ts:(u + 1) * ts] for a in (0, 1)],
                             axis=1)
        rhs = jnp.concatenate([qh, zh] if g == 0 else [zh, qh], axis=0)
        sbuf[n % 2][:, pp * cw:(pp + 1) * cw] = _dot(operands[u][0], rhs)

    outs = {}

    def consume(n, pp):
        u, g = sets[n]
        _, vt, bias = operands[u]
        hd = g * SWA_GROUP + 2 * pp
        sink = jnp.where(first_half, sinks_ref[hd] * LOG2E, sinks_ref[hd + 1] * LOG2E)
        s = sbuf[n % 2][:, pp * cw:(pp + 1) * cw] + bias
        m = jnp.maximum(jnp.max(s, axis=0, keepdims=True), sink)
        e = jnp.exp2(s - m)
        den = jnp.sum(e, axis=0, keepdims=True) + jnp.exp2(sink - m)
        eb = jnp.concatenate([e.astype(BF16), jnp.zeros((nkp - nk, cw), BF16)], axis=0)
        o = _dot(vt[g * hd_dim:(g + 1) * hd_dim], eb) / den
        outs[hd, u] = o[:, :ts]
        outs[hd + 1, u] = o[:, ts:]

    for pp in range(pairs):
        produce(0, pp)
    for n in range(1, len(sets)):
        for pp in range(pairs):
            produce(n, pp)
            consume(n - 1, pp)
    for pp in range(pairs):
        consume(len(sets) - 1, pp)
    rows = [jnp.concatenate([outs[hd, u] for u in range(tq // ts)], axis=1) for hd in range(SWA_HEADS)]
    _store(o_ref, slice(None), jnp.concatenate(rows, axis=0).T.astype(BF16))


def _swa_attention(q, k, v, sinks, *, seq):
    b, l, _ = q.shape
    nq = SWA_Q_COLS
    outs = None
    for front in (False, True):
        tq = BLOCK if front else SWA_TQ
        ts = min(SWA_TS, tq)
        nk = N_META if front else ts + WINDOW + N_META
        if front:
            grid = (1, 1)
            qmap = lambda bb, i: (0, seq // BLOCK, 0)
            kvl, kvmap = BLOCK, qmap
        else:
            grid = (b, seq // tq)
            qmap = lambda bb, i: (bb, i, 0)
            kvl, kvmap = l, lambda bb, i: (bb, 0, 0)
        in_specs = [pl.BlockSpec(memory_space=pltpu.SMEM), pl.BlockSpec((1, tq, nq), qmap),
                    pl.BlockSpec((1, kvl, k.shape[2]), kvmap), pl.BlockSpec((1, kvl, v.shape[2]), kvmap)]
        args = [sinks, q, k, v]
        aliases = {}
        if front:
            in_specs.append(pl.BlockSpec(memory_space=pl.ANY))
            aliases = {len(args): 0}
            args.append(outs)
        outs = pl.pallas_call(
            functools.partial(_swa_kernel, tq=tq, seq=seq, front=front),
            grid=grid,
            in_specs=in_specs,
            out_specs=pl.BlockSpec((b if front else 1, tq, nq), qmap),
            out_shape=jax.ShapeDtypeStruct((b, l, nq), BF16),
            scratch_shapes=[pltpu.VMEM((nk, SWA_GROUP * ts), F32)] * 2,
            input_output_aliases=aliases,
            compiler_params=_cparams(2),
            name="swa_attn" + ("_front" if front else ""),
        )(*args)
    return outs


def _ab_weights(w_in, w_uq, w_ukv):
    win = jnp.pad(w_in, ((0, 0), (0, AB_IN_PAD - w_in.shape[1]))).astype(BF16)
    wq = w_uq.reshape(MLA_Q_RANK, MLA_HEADS, MLA_NOPE + MLA_ROPE)
    wq = jnp.pad(wq, ((0, 0), (0, 0), (0, 2 * LANES - MLA_NOPE - MLA_ROPE))).reshape(MLA_Q_RANK, MLA_HEADS * 2 * LANES)
    wkv = w_ukv.reshape(MLA_KV_RANK, MLA_HEADS, 2, MLA_NOPE).transpose(0, 2, 1, 3).reshape(MLA_KV_RANK, -1)
    return win, wq.astype(BF16), wkv.astype(BF16)


def kernel(x, meta_tokens, ffn1_norm, ffn1_w_gate, ffn1_w_up, ffn1_w_down, mix_norm, ab_w_in, diff_lambda_q1, diff_lambda_k1, diff_lambda_q2, diff_lambda_k2, diff_subln, mla_q_norm, mla_w_uq, mla_kv_norm, mla_w_ukv, ab_w_out, swa_w_qkv, swa_b_qkv, swa_sinks, swa_w_out, swa_b_out, ffn2_norm, ffn2_w_gate, ffn2_w_up, ffn2_w_down, final_norm):
    b, seq, d = x.shape
    depth = ffn1_norm.shape[0]
    l = seq + BLOCK
    n = b * l
    front = jnp.concatenate([jnp.zeros((N_FRONT, d), x.dtype), meta_tokens.astype(x.dtype)], axis=0)

    pos = np.concatenate([np.arange(seq) + N_META, np.maximum(np.arange(BLOCK) - N_FRONT, 0)])
    tabs_p = _rope_tables(pos, PART_ROT)
    tabs_m = _rope_tables(pos, MLA_ROPE)
    row2 = lambda a: a.reshape(1, -1)

    ffn1 = (ffn1_w_gate, ffn1_w_up, ffn1_w_down.astype(BF16))
    ffn2 = (ffn2_w_gate, ffn2_w_up, ffn2_w_down.astype(BF16))
    out = None
    for ly in range(depth):
        w1 = (row2(ffn1_norm[ly]), *ffn1)
        h = _ffn_first(x, front, *w1).reshape(n, d) if ly == 0 else _ffn(h, *w1, ly)
        h3 = h.reshape(b, l, d)
        if ly % 2 == 0:
            e = ly // 2
            lambda_init = 0.8 - 0.6 * math.exp(-0.3 * ly)
            win, wuq, wukv = _ab_weights(ab_w_in[e], mla_w_uq[e], mla_w_ukv[e])
            qa, ka, va, qb, kb, vb = _ab_proj(h3, row2(mix_norm[ly]), win, row2(mla_q_norm[e]), wuq,
                                              row2(mla_kv_norm[e]), wukv, tabs_p + tabs_m)
            extra = [row2(diff_lambda_q1[e]), row2(diff_lambda_k1[e]), row2(diff_lambda_q2[e]),
                     row2(diff_lambda_k2[e]), diff_subln[e].reshape(-1, 1)]
            oa = _causal_attention(functools.partial(_diff_kernel, lambda_init=lambda_init, nh=DIFF_HEADS_PER_STEP),
                                   qa, ka, va, extra, heads=DIFF_HEADS, dk=LANES, reps=2, tq=ATT_TQ, tk=ATT_TK,
                                   seq=seq, name="diff_attn", nh=DIFF_HEADS_PER_STEP)
            ob = _causal_attention(functools.partial(_mla_kernel, nh=MLA_HEADS_PER_STEP), qb, kb, vb, [],
                                   heads=MLA_HEADS, dk=2 * LANES, reps=1, tq=ATT_TQ, tk=ATT_TK, seq=seq,
                                   name="mla_attn", nh=MLA_HEADS_PER_STEP)
            wo = ab_w_out[e].astype(BF16)
            mix, bias = [(oa, wo[:512]), (ob, wo[512:])], None
        else:
            o = ly // 2
            q, k, v = _swa_proj(h3, row2(mix_norm[ly]), swa_w_qkv[o].astype(BF16), row2(swa_b_qkv[o]), tabs_p)
            att = _swa_attention(q, k, v, swa_sinks[o], seq=seq)
            mix, bias = [(att, swa_w_out[o].astype(BF16))], row2(swa_b_out[o])
        w2 = (row2(ffn2_norm[ly]), *ffn2)
        if ly == depth - 1:
            out = _ffn_final(h3, *w2, ly, row2(final_norm), seq, mix=mix, bias=bias)
        else:
            h = _ffn(h, *w2, ly, mix=[(a.reshape(n, -1), w) for a, w in mix], bias=bias)
    return out
```

```python
import functools
import math

import numpy as np
import jax
import jax.numpy as jnp
from jax import lax
from jax.experimental import pallas as pl
from jax.experimental.pallas import tpu as pltpu

F32 = jnp.float32
BF16 = jnp.bfloat16

D_MODEL = 1024
N_META = 16
BLOCK = 128
N_FRONT = BLOCK - N_META
ROPE_THETA = 500000.0
EPS = 1e-6
NEG_INF = -1e30
D_FF = 2816

DIFF_HEADS = 4
DIFF_HEAD_DIM = 64
DIFF_EPS = 1e-5
MLA_HEADS = 4
MLA_NOPE = 128
MLA_ROPE = 64
MLA_Q_RANK = 256
MLA_KV_RANK = 256
SWA_HEADS = 16
SWA_KV_HEADS = 2
SWA_GROUP = SWA_HEADS // SWA_KV_HEADS
SWA_HEAD_DIM = 64
WINDOW = 128

PART_ROT = DIFF_HEAD_DIM // 4
LANES = 128
VMEM_LIMIT = 56 * 1024 * 1024

FFN_TM = 640
FFN_FC = 256
PROJ_TM = 640
FFN_FINAL_TM = 512
ATT_TK = 256
ATT_TQ = 4 * ATT_TK
ATT_CW = 256
DIFF_HEADS_PER_STEP = 2
MLA_HEADS_PER_STEP = 2
DIFF_LAG = 2
MLA_LAG = 3
ATT_GROUP = 2
VT_FILL_UNROLL = 5
LOG2E = 1.4426950408889634
SWA_TQ = 1024
SWA_TS = 128


def _cparams(n_axes):
    return pltpu.CompilerParams(dimension_semantics=("arbitrary",) * n_axes,
                                vmem_limit_bytes=VMEM_LIMIT)


def _resident(shape):
    nd = len(shape)
    return pl.BlockSpec(shape, lambda *_: (0,) * nd, pipeline_mode=pl.Buffered(1))


def _rms(x, g, eps):
    return x * lax.rsqrt(jnp.mean(x * x, axis=-1, keepdims=True) + eps) * g


def _dot(a, b):
    return jnp.dot(a, b, preferred_element_type=F32)


def _ffn_kernel(x_ref, *refs, final=False, aliased=False, n_mix=0, mix_bias=False):
    mix, refs = refs[:2 * n_mix], refs[2 * n_mix:]
    if mix_bias:
        mb_ref, refs = refs[0], refs[1:]
    g_ref, wg_ref, wu_ref, wd_ref, *rest = refs
    if final:
        fg_ref, o_ref, act_ref = rest
    elif aliased:
        _, o_ref, act_ref = rest
    else:
        o_ref, act_ref = rest
    x = x_ref[...].reshape(x_ref.shape[-2:])
    for a_ref, w_ref in zip(mix[0::2], mix[1::2]):
        x = x + _dot(a_ref[...].reshape(a_ref.shape[-2:]), w_ref[...])
    if mix_bias:
        x = x + mb_ref[...]
    xn = _rms(x, g_ref[...], EPS).astype(BF16)
    for c in range(D_FF // FFN_FC):
        sl = slice(c * FFN_FC, (c + 1) * FFN_FC)
        g = _dot(xn, wg_ref[:, sl].astype(BF16))
        u = _dot(xn, wu_ref[:, sl].astype(BF16))
        act_ref[:, sl] = (g * (1.0 / (1.0 + jnp.exp(-g))) * u).astype(BF16)
    y = x + 0.5 * _dot(act_ref[...], wd_ref[...])
    if final:
        y = _rms(y, fg_ref[...], EPS)
    o_ref[...] = jnp.broadcast_to(y, o_ref.shape)


def _mix_args(mix, bias, row):
    args, specs = [], []
    for a, w in mix:
        args += [a, w]
        specs += [row(a.shape[-1]), _resident(w.shape)]
    if bias is not None:
        args.append(bias)
        specs.append(_resident(bias.shape))
    return args, specs, dict(n_mix=len(mix), mix_bias=bias is not None)


def _ffn_wspecs(ly):
    mat = lambda r, c: pl.BlockSpec((None, r, c), lambda *_: (ly, 0, 0), pipeline_mode=pl.Buffered(1))
    return [_resident((1, D_MODEL)), mat(D_MODEL, D_FF), mat(D_MODEL, D_FF), mat(D_FF, D_MODEL)]


def _ffn(h, g, wg, wu, wd, ly, mix=(), bias=None):
    n = h.shape[0]
    tm = FFN_TM
    assert n % tm == 0
    row = lambda w: pl.BlockSpec((tm, w), lambda i: (i, 0))
    margs, mspecs, mkw = _mix_args(mix, bias, row)
    return pl.pallas_call(
        functools.partial(_ffn_kernel, **mkw),
        grid=(n // tm,),
        in_specs=[row(D_MODEL)] + mspecs + _ffn_wspecs(ly),
        out_specs=row(D_MODEL),
        out_shape=jax.ShapeDtypeStruct((n, D_MODEL), F32),
        scratch_shapes=[pltpu.VMEM((tm, D_FF), BF16)],
        compiler_params=_cparams(1),
        name="ffn",
    )(h, *margs, g, wg, wu, wd)


def _ffn_first(x, front, g, wg, wu, wd):
    b, seq, d = x.shape
    tm = FFN_FINAL_TM
    assert seq % tm == 0
    weights = _ffn_wspecs(0)
    row = pl.BlockSpec((1, tm, d), lambda bb, i: (bb, i, 0))
    h3 = pl.pallas_call(
        _ffn_kernel,
        grid=(b, seq // tm),
        in_specs=[row] + weights,
        out_specs=row,
        out_shape=jax.ShapeDtypeStruct((b, seq + BLOCK, d), F32),
        scratch_shapes=[pltpu.VMEM((tm, D_FF), BF16)],
        compiler_params=_cparams(2),
        name="ffn_first",
    )(x, g, wg, wu, wd)
    return pl.pallas_call(
        functools.partial(_ffn_kernel, aliased=True),
        grid=(1,),
        in_specs=[pl.BlockSpec((BLOCK, d), lambda _: (0, 0))] + weights + [pl.BlockSpec(memory_space=pl.ANY)],
        out_specs=pl.BlockSpec((b, BLOCK, d), lambda _: (0, seq // BLOCK, 0)),
        out_shape=jax.ShapeDtypeStruct(h3.shape, F32),
        scratch_shapes=[pltpu.VMEM((BLOCK, D_FF), BF16)],
        input_output_aliases={5: 0},
        compiler_params=_cparams(1),
        name="ffn_first_front",
    )(front, g, wg, wu, wd, h3)


def _ffn_final(h3, g, wg, wu, wd, ly, fg, seq, mix=(), bias=None):
    b = h3.shape[0]
    tm = FFN_FINAL_TM
    assert seq % tm == 0
    row = lambda w: pl.BlockSpec((1, tm, w), lambda bb, i: (bb, i, 0))
    margs, mspecs, mkw = _mix_args(mix, bias, row)
    return pl.pallas_call(
        functools.partial(_ffn_kernel, final=True, **mkw),
        grid=(b, seq // tm),
        in_specs=[row(D_MODEL)] + mspecs + _ffn_wspecs(ly) + [_resident((1, D_MODEL))],
        out_specs=row(D_MODEL),
        out_shape=jax.ShapeDtypeStruct((b, seq, D_MODEL), F32),
        scratch_shapes=[pltpu.VMEM((tm, D_FF), BF16)],
        compiler_params=_cparams(2),
        name="ffn_final",
    )(h3, *margs, g, wg, wu, wd, fg)


def _rope(y, c, s, half):
    w = y.shape[1]
    reps = w // LANES
    if reps > 1:
        c = jnp.concatenate([c] * reps, axis=1)
        s = jnp.concatenate([s] * reps, axis=1)
    lane = lax.broadcasted_iota(jnp.int32, y.shape, 1)
    first = (lane & 63) < half
    partner = jnp.where(first, pltpu.roll(y, w - half, 1), pltpu.roll(y, half, 1))
    return y * c + partner * s


def _rope_tables(pos, rot_dim):
    half = rot_dim // 2
    g = np.arange(LANES) & 63
    inv = np.power(np.float32(ROPE_THETA), -(2 * (g % half)).astype(np.float32) / np.float32(rot_dim))
    ang = pos.astype(np.float32)[:, None] * inv[None, :].astype(np.float32)
    rotated = (g < rot_dim)[None, :]
    cos = np.where(rotated, np.cos(ang), np.float32(1.0)).astype(np.float32)
    sin = np.where(rotated, np.where((g < half)[None, :], -np.sin(ang), np.sin(ang)), np.float32(0.0))
    return jnp.asarray(cos), jnp.asarray(sin.astype(np.float32))


DIFF_COLS = DIFF_HEADS * 2 * DIFF_HEAD_DIM
MLA_QK_COLS = MLA_HEADS * 2 * LANES
MLA_V_COLS = MLA_HEADS * LANES
AB_SPLITS = [int(c) for c in np.cumsum([0, DIFF_COLS, DIFF_COLS, DIFF_COLS, MLA_Q_RANK, MLA_KV_RANK, LANES])]
AB_IN_PAD = AB_SPLITS[-1]


def _ab_proj_kernel(h_ref, g_ref, win_ref, qn_ref, wuq_ref, kvn_ref, wukv_ref, ca_ref, sa_ref, cm_ref, sm_ref,
                    qa_ref, ka_ref, va_ref, qb_ref, kb_ref, vb_ref):
    xn = _rms(h_ref[0], g_ref[...], EPS).astype(BF16)
    ca, sa, cm, sm = ca_ref[...], sa_ref[...], cm_ref[...], sm_ref[...]
    half_a = PART_ROT // 2
    half_m = MLA_ROPE // 2
    cols = [slice(a, b) for a, b in zip(AB_SPLITS[:-1], AB_SPLITS[1:])]
    qa = _rope(_dot(xn, win_ref[:, cols[0]]), ca, sa, half_a)
    qa_ref[0] = (qa * (DIFF_HEAD_DIM ** -0.5 * LOG2E)).astype(BF16)
    ka_ref[0] = _rope(_dot(xn, win_ref[:, cols[1]]), ca, sa, half_a).astype(BF16)
    va_ref[0] = _dot(xn, win_ref[:, cols[2]]).astype(BF16)
    cq = _dot(xn, win_ref[:, cols[3]])
    ckv = _dot(xn, win_ref[:, cols[4]])
    kr = _rope(_dot(xn, win_ref[:, cols[5]]), cm, sm, half_m).astype(BF16)
    qb = _dot(_rms(cq, qn_ref[...], EPS).astype(BF16), wuq_ref[...])
    scale = (MLA_NOPE + MLA_ROPE) ** -0.5 * LOG2E
    for hh in range(MLA_HEADS):
        o = 2 * LANES * hh
        qb_ref[0, :, o:o + LANES] = (qb[:, o:o + LANES] * scale).astype(BF16)
        qb_ref[0, :, o + LANES:o + 2 * LANES] = (_rope(qb[:, o + LANES:o + 2 * LANES], cm, sm, half_m) * scale).astype(BF16)
    kv = _dot(_rms(ckv, kvn_ref[...], EPS).astype(BF16), wukv_ref[...])
    for hh in range(MLA_HEADS):
        o = 2 * LANES * hh
        kb_ref[0, :, o:o + LANES] = kv[:, LANES * hh:LANES * (hh + 1)].astype(BF16)
        kb_ref[0, :, o + LANES:o + 2 * LANES] = kr
    vb_ref[0] = kv[:, MLA_HEADS * MLA_NOPE:].astype(BF16)


def _ab_proj(h3, g, win, qn, wuq, kvn, wukv, tabs):
    b, l, _ = h3.shape
    tm = PROJ_TM
    assert l % tm == 0
    row = lambda w: pl.BlockSpec((1, tm, w), lambda j, bb: (bb, j, 0))
    tab = pl.BlockSpec((tm, LANES), lambda j, bb: (j, 0))
    outw = (DIFF_COLS, DIFF_COLS, DIFF_COLS, MLA_QK_COLS, MLA_QK_COLS, MLA_V_COLS)
    return pl.pallas_call(
        _ab_proj_kernel,
        grid=(l // tm, b),
        in_specs=[row(D_MODEL), _resident((1, D_MODEL)), _resident((D_MODEL, AB_IN_PAD)),
                  _resident((1, MLA_Q_RANK)), _resident((MLA_Q_RANK, MLA_QK_COLS)),
                  _resident((1, MLA_KV_RANK)), _resident((MLA_KV_RANK, MLA_HEADS * MLA_NOPE + MLA_V_COLS)),
                  tab, tab, tab, tab],
        out_specs=[row(w) for w in outw],
        out_shape=[jax.ShapeDtypeStruct((b, l, w), BF16) for w in outw],
        compiler_params=_cparams(2),
        name="ab_proj",
    )(h3, g, win, qn, wuq, kvn, wukv, *tabs)


def _transpose(x):
    return x.T


def _store(o_ref, cols, val):
    for bb in range(o_ref.shape[0]):
        o_ref[bb, :, cols] = val


def _head_slices(c0, heads):
    cph, dk, dv = heads
    h = c0 // cph
    return slice(h * dk, (h + 1) * dk), slice(h * dv, (h + 1) * dv)


def _flash_step(k, vt, qt, m_ref, l_ref, acc_ref, mask, heads, between=None):
    cw = min(ATT_CW, heads[0])
    probs = {}
    for c0 in range(0, qt.shape[1], cw):
        cs = slice(c0, c0 + cw)
        ks, _ = _head_slices(c0, heads)
        s = _dot(k[:, ks], qt[:, cs])
        if mask is not None:
            s = jnp.where(mask(c0, s.shape), s, NEG_INF)
        m_new = jnp.max(s, axis=0, keepdims=True)
        p = jnp.exp2(s - m_new)
        l_ref[:, cs] = jnp.sum(p, axis=0, keepdims=True)
        m_ref[:, cs] = m_new
        probs[c0] = p.astype(BF16)
    if between is not None:
        between()
    for c0, p in probs.items():
        acc_ref[:, c0:c0 + cw] = _dot(vt[_head_slices(c0, heads)[1]], p)


def _produce(k, qt, s_ref, x_ref, chunks, heads):
    for c0, rows in chunks.items():
        cs = slice(c0, c0 + ATT_CW)
        s = _dot(k[:rows, _head_slices(c0, heads)[0]], qt[:, cs])
        s_ref[:rows, cs] = s
        if rows == k.shape[0]:
            x_ref[:, cs] = jnp.max(s, axis=0, keepdims=True)


def _consume(s_ref, x_ref, vt, m_ref, l_ref, acc_ref, chunks, heads, diag=None):
    tk = s_ref.shape[0]
    for c0 in chunks:
        cs = slice(c0, c0 + ATT_CW)
        if diag is None:
            seen = tk
        else:
            key0, tq, tri = diag
            seen = min((c0 & (tq - 1)) - key0, tk)
            if seen + ATT_CW <= 0:
                continue
        if seen == tk:
            s = s_ref[:, cs]
            smax = x_ref[:, cs]
        else:
            s = s_ref[seen:seen + ATT_CW, cs] + tri
            if seen > 0:
                s = jnp.concatenate([s_ref[0:seen, cs], s], axis=0)
            smax = jnp.max(s, axis=0, keepdims=True)
        m_prev = m_ref[:, cs]
        m_new = jnp.maximum(m_prev, smax)
        alpha = jnp.exp2(m_prev - m_new)
        p = jnp.exp2(s - m_new)
        l_ref[:, cs] = alpha * l_ref[:, cs] + jnp.sum(p, axis=0, keepdims=True)
        acc_ref[:, cs] = alpha * acc_ref[:, cs] + _dot(vt[_head_slices(c0, heads)[1], :s.shape[0]], p.astype(BF16))
        m_ref[:, cs] = m_new


def _causal_attend(make_qt, k_ref, v_ref, scr, *, tq, tk, seq, front, nh, lag):
    vt_ref, m_ref, l_ref, acc_ref, s0, s1, x0, x1 = scr
    state = (m_ref, l_ref, acc_ref)
    heads = (m_ref.shape[1] // nh, k_ref.shape[2] // nh, v_ref.shape[2] // nh)
    f0 = 0 if front else seq
    kf = k_ref[0, f0:f0 + BLOCK, :]

    def valid(c0, shape):
        return lax.broadcasted_iota(jnp.int32, shape, 0) >= N_FRONT

    def causal(key0):
        def mask(c0, shape):
            key = lax.broadcasted_iota(jnp.int32, shape, 0) + key0
            qry = (lax.broadcasted_iota(jnp.int32, shape, 1) + c0) & (tq - 1)
            return key <= qry
        return mask

    if front:
        qt = make_qt()
        both = lambda c0, shape: valid(c0, shape) & causal(0)(c0, shape)
        _flash_step(kf, _transpose(v_ref[0, f0:f0 + BLOCK, :]), qt, *state, both, heads)
        return
    per = tq // tk
    assert tq == per * tk and per % 2 == 0
    i = pl.program_id(2)
    nblk = v_ref.shape[1] // BLOCK
    sub = tk // BLOCK
    chunks = list(range(0, m_ref.shape[1], ATT_CW))

    @pl.when(i == 0)
    def _fill():
        def fill(c, carry):
            st = pl.multiple_of(c * BLOCK, BLOCK)
            vt_ref[c] = _transpose(v_ref[0, pl.ds(st, BLOCK), :])
            return carry
        lax.fori_loop(0, nblk, fill, 0, unroll=VT_FILL_UNROLL)

    qt = make_qt()

    def k_block(j):
        return k_ref[0, pl.ds(pl.multiple_of(j * tk, tk), tk), :]

    def vt_block(j):
        return jnp.concatenate([vt_ref[j * sub + u] for u in range(sub)], axis=1)

    groups = [chunks[g0:g0 + ATT_GROUP] for g0 in range(0, len(chunks), ATT_GROUP)]
    ng = len(groups)
    lag = min(lag, ng)
    bufs = ((s0, x0), (s1, x1))

    every = {c0: tk for c0 in chunks}

    def produce(kb, parity, g, allowed=every):
        _produce(kb, qt, *bufs[parity], {c0: allowed[c0] for c0 in groups[g] if c0 in allowed}, heads)

    def consume(vb, parity, g, diag=None):
        _consume(*bufs[parity], vb, *state, groups[g], heads, diag=diag)

    def run(blocks, ahead):
        nxt = [(kb, allowed) for kb, _, _, allowed in blocks] + list(ahead)
        for u in range(len(blocks) * ng):
            t = u + lag
            if t // ng < len(nxt):
                kb, allowed = nxt[t // ng]
                produce(kb, (t // ng) % 2, t % ng, allowed)
            _, vb, diag, _ = blocks[u // ng]
            consume(vb, (u // ng) % 2, u % ng, diag)

    meta = slice(seq + N_FRONT, seq + BLOCK)
    _flash_step(k_ref[0, meta, :], _transpose(v_ref[0, meta, :]), qt, *state, None, heads,
                between=lambda: [produce(k_block(0), 0, g) for g in range(lag)])

    def body(p, carry):
        run([(k_block(2 * p), vt_block(2 * p), None, every), (k_block(2 * p + 1), vt_block(2 * p + 1), None, every)],
            [(k_block(2 * p + 2), every)])
        return carry

    lax.fori_loop(0, (per // 2) * i, body, 0)
    tri_shape = (ATT_CW, ATT_CW)
    tri = jnp.where(lax.broadcasted_iota(jnp.int32, tri_shape, 0) <= lax.broadcasted_iota(jnp.int32, tri_shape, 1),
                    0.0, NEG_INF)
    seen = lambda key0: {c0: min((c0 & (tq - 1)) + ATT_CW - key0, tk) for c0 in chunks
                         if (c0 & (tq - 1)) + ATT_CW > key0}
    run([(k_block(per * i + u), vt_block(per * i + u), (u * tk, tq, tri), seen(u * tk)) for u in range(per)], [])


def _diff_kernel(q_ref, k_ref, v_ref, lq1_ref, lk1_ref, lq2_ref, lk2_ref, sub_ref, *rest,
                 tq, tk, seq, front, lambda_init, nh):
    if front:
        _, o_ref, *scr = rest
    else:
        o_ref, *scr = rest
    m_ref, l_ref, acc_ref = scr[1:4]

    def stacked_queries():
        t = _transpose(q_ref[0])
        d = lax.broadcasted_iota(jnp.int32, (LANES, tq), 0)
        zero = jnp.zeros((LANES, tq), t.dtype)
        cols = []
        for h in range(nh):
            th = t[h * LANES:(h + 1) * LANES]
            cols += [jnp.where(d < DIFF_HEAD_DIM, th, zero), jnp.where(d >= DIFF_HEAD_DIM, th, zero)]
        return jnp.concatenate(cols, axis=1)

    _causal_attend(stacked_queries, k_ref, v_ref, scr, tq=tq, tk=tk, seq=seq, front=front, nh=nh, lag=DIFF_LAG)
    o = acc_ref[...] / l_ref[...]
    lam = (jnp.exp(jnp.sum(lq1_ref[...] * lk1_ref[...], keepdims=True))
           - jnp.exp(jnp.sum(lq2_ref[...] * lk2_ref[...], keepdims=True)) + lambda_init)
    for h in range(nh):
        w = o[:, 2 * h * tq:(2 * h + 1) * tq] - lam * o[:, (2 * h + 1) * tq:(2 * h + 2) * tq]
        w = w * lax.rsqrt(jnp.mean(w * w, axis=0, keepdims=True) + DIFF_EPS) * sub_ref[...]
        _store(o_ref, slice(h * LANES, (h + 1) * LANES), (w * (1.0 - lambda_init)).T.astype(BF16))


def _mla_kernel(q_ref, k_ref, v_ref, *rest, tq, tk, seq, front, nh):
    if front:
        _, o_ref, *scr = rest
    else:
        o_ref, *scr = rest
    m_ref, l_ref, acc_ref = scr[1:4]
    dk = q_ref.shape[2] // nh

    def queries():
        t = _transpose(q_ref[0])
        return jnp.concatenate([t[h * dk:(h + 1) * dk] for h in range(nh)], axis=1)

    _causal_attend(queries, k_ref, v_ref, scr, tq=tq, tk=tk, seq=seq, front=front, nh=nh, lag=MLA_LAG)
    o = acc_ref[...] / l_ref[...]
    for h in range(nh):
        _store(o_ref, slice(h * LANES, (h + 1) * LANES), o[:, h * tq:(h + 1) * tq].T.astype(BF16))


def _causal_attention(kernel, q, k, v, extra, *, heads, dk, reps, tq, tk, seq, name, nh=1):
    b, l, _ = q.shape
    dv = LANES * nh
    dk = dk * nh
    reps = reps * nh
    heads = heads // nh
    outs = None
    for front in (False, True):
        t = BLOCK if front else tq
        if front:
            grid = (1, heads, 1)
            qmap = lambda bb, hh, i: (0, seq // BLOCK, hh)
        else:
            grid = (b, heads, seq // t)
            qmap = lambda bb, hh, i: (bb, i, hh)
        if front:
            kvl, kvmap = BLOCK, qmap
        else:
            kvl, kvmap = l, lambda bb, hh, i: (bb, 0, hh)
        in_specs = [pl.BlockSpec((1, t, dk), qmap), pl.BlockSpec((1, kvl, dk), kvmap),
                    pl.BlockSpec((1, kvl, dv), kvmap)]
        in_specs += [_resident(e.shape) for e in extra]
        args = [q, k, v, *extra]
        aliases = {}
        if front:
            in_specs.append(pl.BlockSpec(memory_space=pl.ANY))
            aliases = {len(args): 0}
            args.append(outs)
        outs = pl.pallas_call(
            functools.partial(kernel, tq=t, tk=min(t, tk), seq=seq, front=front),
            grid=grid,
            in_specs=in_specs,
            out_specs=pl.BlockSpec((b if front else 1, t, dv), qmap),
            out_shape=jax.ShapeDtypeStruct((b, l, heads * dv), BF16),
            scratch_shapes=[pltpu.VMEM((1 if front else l // BLOCK, dv, BLOCK), BF16),
                            pltpu.VMEM((1, reps * t), F32), pltpu.VMEM((1, reps * t), F32),
                            pltpu.VMEM((LANES, reps * t), F32)]
                           + [pltpu.VMEM((8 if front else tk, reps * t), F32)] * 2
                           + [pltpu.VMEM((1, reps * t), F32)] * 2,
            input_output_aliases=aliases,
            compiler_params=_cparams(3),
            name=name + ("_front" if front else ""),
        )(*args)
    return outs


SWA_Q_COLS = SWA_HEADS * SWA_HEAD_DIM
SWA_KV_COLS = SWA_KV_HEADS * SWA_HEAD_DIM
SWA_IN_COLS = SWA_Q_COLS + 2 * SWA_KV_COLS


def _swa_proj_kernel(h_ref, g_ref, w_ref, b_ref, ca_ref, sa_ref, q_ref, k_ref, v_ref):
    xn = _rms(h_ref[0], g_ref[...], EPS).astype(BF16)
    ca, sa = ca_ref[...], sa_ref[...]
    half = PART_ROT // 2
    nq = SWA_Q_COLS
    nk = SWA_KV_COLS
    q = _rope(_dot(xn, w_ref[:, 0:nq]) + b_ref[:, 0:nq], ca, sa, half)
    q_ref[0] = (q * (SWA_HEAD_DIM ** -0.5 * LOG2E)).astype(BF16)
    k_ref[0] = _rope(_dot(xn, w_ref[:, nq:nq + nk]) + b_ref[:, nq:nq + nk], ca, sa, half).astype(BF16)
    v_ref[0] = (_dot(xn, w_ref[:, nq + nk:]) + b_ref[:, nq + nk:]).astype(BF16)


def _swa_proj(h3, g, w, bias, tabs):
    b, l, _ = h3.shape
    tm = PROJ_TM
    row = lambda wd: pl.BlockSpec((1, tm, wd), lambda j, bb: (bb, j, 0))
    tab = pl.BlockSpec((tm, LANES), lambda j, bb: (j, 0))
    outw = (SWA_Q_COLS, SWA_KV_COLS, SWA_KV_COLS)
    return pl.pallas_call(
        _swa_proj_kernel,
        grid=(l // tm, b),
        in_specs=[row(D_MODEL), _resident((1, D_MODEL)), _resident((D_MODEL, SWA_IN_COLS)),
                  _resident((1, SWA_IN_COLS)), tab, tab],
        out_specs=[row(w_) for w_ in outw],
        out_shape=[jax.ShapeDtypeStruct((b, l, w_), BF16) for w_ in outw],
        compiler_params=_cparams(2),
        name="swa_proj",
    )(h3, g, w, bias, *tabs)


def _swa_kernel(sinks_ref, q_ref, k_ref, v_ref, *rest, tq, seq, front):
    if front:
        _, o_ref, s_a, s_b = rest
    else:
        o_ref, s_a, s_b = rest
    hd_dim = SWA_HEAD_DIM
    ts = min(SWA_TS, tq)
    pairs = SWA_GROUP // 2
    cw = 2 * ts
    f0 = 0 if front else seq
    meta_k = k_ref[0, f0 + N_FRONT:f0 + BLOCK, :]
    meta_v = v_ref[0, f0 + N_FRONT:f0 + BLOCK, :]
    band = 0 if front else ts + WINDOW
    nk = band + N_META
    nkp = -(-nk // LANES) * LANES
    qt = _transpose(q_ref[0])
    zh = jnp.zeros((hd_dim, cw), BF16)
    key = lax.broadcasted_iota(jnp.int32, (nk, ts), 0)
    qry = lax.broadcasted_iota(jnp.int32, (nk, ts), 1)
    first_half = lax.broadcasted_iota(jnp.int32, (1, cw), 1) < ts

    def keys_of(u):
        if front:
            kc, vc = meta_k, meta_v
            allowed = key + N_FRONT <= qry
        else:
            start = pl.program_id(1) * tq + u * ts
            bs = pl.multiple_of(jnp.maximum(start - WINDOW, 0), BLOCK)
            kc = jnp.concatenate([k_ref[0, pl.ds(bs, band), :], meta_k], axis=0)
            vc = jnp.concatenate([v_ref[0, pl.ds(bs, band), :], meta_v], axis=0)
            dist = (start + qry) - (bs + key)
            allowed = (key >= band) | ((dist >= 0) & (dist < WINDOW))
        bias = jnp.where(allowed, 0.0, NEG_INF)
        vt = _transpose(jnp.concatenate([vc, jnp.zeros((nkp - nk, vc.shape[1]), vc.dtype)], axis=0))
        return kc, vt, jnp.concatenate([bias, bias], axis=1)

    sets = [(u, g) for u in range(tq // ts) for g in range(SWA_KV_HEADS)]
    operands = {}
    sbuf = (s_a, s_b)

    def produce(n, pp):
        u, g = sets[n]
        if u not in operands:
            operands[u] = keys_of(u)
        hd = g * SWA_GROUP + 2 * pp
        qh = jnp.concatenate([qt[(hd + a) * hd_dim:(hd + a + 1) * hd_dim, u * ts:(u + 1) * ts] for a in (0, 1)],
                             axis=1)
        rhs = jnp.concatenate([qh, zh] if g == 0 else [zh, qh], axis=0)
        sbuf[n % 2][:, pp * cw:(pp + 1) * cw] = _dot(operands[u][0], rhs)

    outs = {}

    def consume(n, pp):
        u, g = sets[n]
        _, vt, bias = operands[u]
        hd = g * SWA_GROUP + 2 * pp
        sink = jnp.where(first_half, sinks_ref[hd] * LOG2E, sinks_ref[hd + 1] * LOG2E)
        s = sbuf[n % 2][:, pp * cw:(pp + 1) * cw] + bias
        m = jnp.maximum(jnp.max(s, axis=0, keepdims=True), sink)
        e = jnp.exp2(s - m)
        den = jnp.sum(e, axis=0, keepdims=True) + jnp.exp2(sink - m)
        eb = jnp.concatenate([e.astype(BF16), jnp.zeros((nkp - nk, cw), BF16)], axis=0)
        o = _dot(vt[g * hd_dim:(g + 1) * hd_dim], eb) / den
        outs[hd, u] = o[:, :ts]
        outs[hd + 1, u] = o[:, ts:]

    for pp in range(pairs):
        produce(0, pp)
    for n in range(1, len(sets)):
        for pp in range(pairs):
            produce(n, pp)
            consume(n - 1, pp)
    for pp in range(pairs):
        consume(len(sets) - 1, pp)
    rows = [jnp.concatenate([outs[hd, u] for u in range(tq // ts)], axis=1) for hd in range(SWA_HEADS)]
    _store(o_ref, slice(None), jnp.concatenate(rows, axis=0).T.astype(BF16))


def _swa_attention(q, k, v, sinks, *, seq):
    b, l, _ = q.shape
    nq = SWA_Q_COLS
    outs = None
    for front in (False, True):
        tq = BLOCK if front else SWA_TQ
        ts = min(SWA_TS, tq)
        nk = N_META if front else ts + WINDOW + N_META
        if front:
            grid = (1, 1)
            qmap = lambda bb, i: (0, seq // BLOCK, 0)
            kvl, kvmap = BLOCK, qmap
        else:
            grid = (b, seq // tq)
            qmap = lambda bb, i: (bb, i, 0)
            kvl, kvmap = l, lambda bb, i: (bb, 0, 0)
        in_specs = [pl.BlockSpec(memory_space=pltpu.SMEM), pl.BlockSpec((1, tq, nq), qmap),
                    pl.BlockSpec((1, kvl, k.shape[2]), kvmap), pl.BlockSpec((1, kvl, v.shape[2]), kvmap)]
        args = [sinks, q, k, v]
        aliases = {}
        if front:
            in_specs.append(pl.BlockSpec(memory_space=pl.ANY))
            aliases = {len(args): 0}
            args.append(outs)
        outs = pl.pallas_call(
            functools.partial(_swa_kernel, tq=tq, seq=seq, front=front),
            grid=grid,
            in_specs=in_specs,
            out_specs=pl.BlockSpec((b if front else 1, tq, nq), qmap),
            out_shape=jax.ShapeDtypeStruct((b, l, nq), BF16),
            scratch_shapes=[pltpu.VMEM((nk, SWA_GROUP * ts), F32)] * 2,
            input_output_aliases=aliases,
            compiler_params=_cparams(2),
            name="swa_attn" + ("_front" if front else ""),
        )(*args)
    return outs


def _ab_weights(w_in, w_uq, w_ukv):
    win = jnp.pad(w_in, ((0, 0), (0, AB_IN_PAD - w_in.shape[1]))).astype(BF16)
    wq = w_uq.reshape(MLA_Q_RANK, MLA_HEADS, MLA_NOPE + MLA_ROPE)
    wq = jnp.pad(wq, ((0, 0), (0, 0), (0, 2 * LANES - MLA_NOPE - MLA_ROPE))).reshape(MLA_Q_RANK, MLA_HEADS * 2 * LANES)
    wkv = w_ukv.reshape(MLA_KV_RANK, MLA_HEADS, 2, MLA_NOPE).transpose(0, 2, 1, 3).reshape(MLA_KV_RANK, -1)
    return win, wq.astype(BF16), wkv.astype(BF16)


def kernel(x, meta_tokens, ffn1_norm, ffn1_w_gate, ffn1_w_up, ffn1_w_down, mix_norm, ab_w_in, diff_lambda_q1, diff_lambda_k1, diff_lambda_q2, diff_lambda_k2, diff_subln, mla_q_norm, mla_w_uq, mla_kv_norm, mla_w_ukv, ab_w_out, swa_w_qkv, swa_b_qkv, swa_sinks, swa_w_out, swa_b_out, ffn2_norm, ffn2_w_gate, ffn2_w_up, ffn2_w_down, final_norm):
    b, seq, d = x.shape
    depth = ffn1_norm.shape[0]
    l = seq + BLOCK
    n = b * l
    front = jnp.concatenate([jnp.zeros((N_FRONT, d), x.dtype), meta_tokens.astype(x.dtype)], axis=0)

    pos = np.concatenate([np.arange(seq) + N_META, np.maximum(np.arange(BLOCK) - N_FRONT, 0)])
    tabs_p = _rope_tables(pos, PART_ROT)
    tabs_m = _rope_tables(pos, MLA_ROPE)
    row2 = lambda a: a.reshape(1, -1)

    ffn1 = (ffn1_w_gate, ffn1_w_up, ffn1_w_down.astype(BF16))
    ffn2 = (ffn2_w_gate, ffn2_w_up, ffn2_w_down.astype(BF16))
    out = None
    for ly in range(depth):
        w1 = (row2(ffn1_norm[ly]), *ffn1)
        h = _ffn_first(x, front, *w1).reshape(n, d) if ly == 0 else _ffn(h, *w1, ly)
        h3 = h.reshape(b, l, d)
        if ly % 2 == 0:
            e = ly // 2
            lambda_init = 0.8 - 0.6 * math.exp(-0.3 * ly)
            win, wuq, wukv = _ab_weights(ab_w_in[e], mla_w_uq[e], mla_w_ukv[e])
            qa, ka, va, qb, kb, vb = _ab_proj(h3, row2(mix_norm[ly]), win, row2(mla_q_norm[e]), wuq,
                                              row2(mla_kv_norm[e]), wukv, tabs_p + tabs_m)
            extra = [row2(diff_lambda_q1[e]), row2(diff_lambda_k1[e]), row2(diff_lambda_q2[e]),
                     row2(diff_lambda_k2[e]), diff_subln[e].reshape(-1, 1)]
            oa = _causal_attention(functools.partial(_diff_kernel, lambda_init=lambda_init, nh=DIFF_HEADS_PER_STEP),
                                   qa, ka, va, extra, heads=DIFF_HEADS, dk=LANES, reps=2, tq=ATT_TQ, tk=ATT_TK,
                                   seq=seq, name="diff_attn", nh=DIFF_HEADS_PER_STEP)
            ob = _causal_attention(functools.partial(_mla_kernel, nh=MLA_HEADS_PER_STEP), qb, kb, vb, [],
                                   heads=MLA_HEADS, dk=2 * LANES, reps=1, tq=ATT_TQ, tk=ATT_TK, seq=seq,
                                   name="mla_attn", nh=MLA_HEADS_PER_STEP)
            wo = ab_w_out[e].astype(BF16)
            mix, bias = [(oa, wo[:DIFF_COLS]), (ob, wo[DIFF_COLS:])], None
        else:
            o = ly // 2
            q, k, v = _swa_proj(h3, row2(mix_norm[ly]), swa_w_qkv[o].astype(BF16), row2(swa_b_qkv[o]), tabs_p)
            att = _swa_attention(q, k, v, swa_sinks[o], seq=seq)
            mix, bias = [(att, swa_w_out[o].astype(BF16))], row2(swa_b_out[o])
        w2 = (row2(ffn2_norm[ly]), *ffn2)
        if ly == depth - 1:
            out = _ffn_final(h3, *w2, ly, row2(final_norm), seq, mix=mix, bias=bias)
        else:
            h = _ffn(h, *w2, ly, mix=[(a.reshape(n, -1), w) for a, w in mix], bias=bias)
    return out
```

```python
import functools
import math

import numpy as np
import jax
import jax.numpy as jnp
from jax import lax
from jax.experimental import pallas as pl
from jax.experimental.pallas import tpu as pltpu

F32 = jnp.float32
BF16 = jnp.bfloat16

D_MODEL = 1024
N_META = 16
BLOCK = 128
N_FRONT = BLOCK - N_META
ROPE_THETA = 500000.0
EPS = 1e-6
NEG_INF = -1e30
D_FF = 2816

DIFF_HEADS = 4
DIFF_HEAD_DIM = 64
DIFF_EPS = 1e-5
MLA_HEADS = 4
MLA_NOPE = 128
MLA_ROPE = 64
MLA_Q_RANK = 256
MLA_KV_RANK = 256
SWA_HEADS = 16
SWA_KV_HEADS = 2
SWA_GROUP = SWA_HEADS // SWA_KV_HEADS
SWA_HEAD_DIM = 64
WINDOW = 128

PART_ROT = DIFF_HEAD_DIM // 4
LANES = 128
VMEM_LIMIT = 56 * 1024 * 1024

FFN_TM = 640
FFN_FC = 256
PROJ_TM = 640
FFN_FINAL_TM = 512
ATT_TK = 512
ATT_TQ = 2 * ATT_TK
ATT_CW = 256
DIFF_HEADS_PER_STEP = 2
MLA_HEADS_PER_STEP = 2
DIFF_LAG = 2
MLA_LAG = 3
ATT_GROUP = 2
VT_FILL_UNROLL = 5
LOG2E = 1.4426950408889634
SWA_TQ = 1024
SWA_TS = 128


def _cparams(n_axes):
    return pltpu.CompilerParams(dimension_semantics=("arbitrary",) * n_axes,
                                vmem_limit_bytes=VMEM_LIMIT)


def _resident(shape):
    nd = len(shape)
    return pl.BlockSpec(shape, lambda *_: (0,) * nd, pipeline_mode=pl.Buffered(1))


def _rms(x, g, eps):
    return x * lax.rsqrt(jnp.mean(x * x, axis=-1, keepdims=True) + eps) * g


def _dot(a, b):
    return jnp.dot(a, b, preferred_element_type=F32)


def _ffn_kernel(x_ref, *refs, final=False, aliased=False, n_mix=0, mix_bias=False):
    mix, refs = refs[:2 * n_mix], refs[2 * n_mix:]
    if mix_bias:
        mb_ref, refs = refs[0], refs[1:]
    g_ref, wg_ref, wu_ref, wd_ref, *rest = refs
    if final:
        fg_ref, o_ref, act_ref = rest
    elif aliased:
        _, o_ref, act_ref = rest
    else:
        o_ref, act_ref = rest
    x = x_ref[...].reshape(x_ref.shape[-2:])
    for a_ref, w_ref in zip(mix[0::2], mix[1::2]):
        x = x + _dot(a_ref[...].reshape(a_ref.shape[-2:]), w_ref[...])
    if mix_bias:
        x = x + mb_ref[...]
    xn = _rms(x, g_ref[...], EPS).astype(BF16)
    for c in range(D_FF // FFN_FC):
        sl = slice(c * FFN_FC, (c + 1) * FFN_FC)
        g = _dot(xn, wg_ref[:, sl].astype(BF16))
        u = _dot(xn, wu_ref[:, sl].astype(BF16))
        act_ref[:, sl] = (g * (1.0 / (1.0 + jnp.exp(-g))) * u).astype(BF16)
    y = x + 0.5 * _dot(act_ref[...], wd_ref[...])
    if final:
        y = _rms(y, fg_ref[...], EPS)
    o_ref[...] = jnp.broadcast_to(y, o_ref.shape)


def _mix_args(mix, bias, row):
    args, specs = [], []
    for a, w in mix:
        args += [a, w]
        specs += [row(a.shape[-1]), _resident(w.shape)]
    if bias is not None:
        args.append(bias)
        specs.append(_resident(bias.shape))
    return args, specs, dict(n_mix=len(mix), mix_bias=bias is not None)


def _ffn_wspecs(ly):
    mat = lambda r, c: pl.BlockSpec((None, r, c), lambda *_: (ly, 0, 0), pipeline_mode=pl.Buffered(1))
    return [_resident((1, D_MODEL)), mat(D_MODEL, D_FF), mat(D_MODEL, D_FF), mat(D_FF, D_MODEL)]


def _ffn(h, g, wg, wu, wd, ly, mix=(), bias=None):
    n = h.shape[0]
    tm = FFN_TM
    assert n % tm == 0
    row = lambda w: pl.BlockSpec((tm, w), lambda i: (i, 0))
    margs, mspecs, mkw = _mix_args(mix, bias, row)
    return pl.pallas_call(
        functools.partial(_ffn_kernel, **mkw),
        grid=(n // tm,),
        in_specs=[row(D_MODEL)] + mspecs + _ffn_wspecs(ly),
        out_specs=row(D_MODEL),
        out_shape=jax.ShapeDtypeStruct((n, D_MODEL), F32),
        scratch_shapes=[pltpu.VMEM((tm, D_FF), BF16)],
        compiler_params=_cparams(1),
        name="ffn",
    )(h, *margs, g, wg, wu, wd)


def _ffn_first(x, front, g, wg, wu, wd):
    b, seq, d = x.shape
    tm = FFN_FINAL_TM
    assert seq % tm == 0
    weights = _ffn_wspecs(0)
    row = pl.BlockSpec((1, tm, d), lambda bb, i: (bb, i, 0))
    h3 = pl.pallas_call(
        _ffn_kernel,
        grid=(b, seq // tm),
        in_specs=[row] + weights,
        out_specs=row,
        out_shape=jax.ShapeDtypeStruct((b, seq + BLOCK, d), F32),
        scratch_shapes=[pltpu.VMEM((tm, D_FF), BF16)],
        compiler_params=_cparams(2),
        name="ffn_first",
    )(x, g, wg, wu, wd)
    return pl.pallas_call(
        functools.partial(_ffn_kernel, aliased=True),
        grid=(1,),
        in_specs=[pl.BlockSpec((BLOCK, d), lambda _: (0, 0))] + weights + [pl.BlockSpec(memory_space=pl.ANY)],
        out_specs=pl.BlockSpec((b, BLOCK, d), lambda _: (0, seq // BLOCK, 0)),
        out_shape=jax.ShapeDtypeStruct(h3.shape, F32),
        scratch_shapes=[pltpu.VMEM((BLOCK, D_FF), BF16)],
        input_output_aliases={5: 0},
        compiler_params=_cparams(1),
        name="ffn_first_front",
    )(front, g, wg, wu, wd, h3)


def _ffn_final(h3, g, wg, wu, wd, ly, fg, seq, mix=(), bias=None):
    b = h3.shape[0]
    tm = FFN_FINAL_TM
    assert seq % tm == 0
    row = lambda w: pl.BlockSpec((1, tm, w), lambda bb, i: (bb, i, 0))
    margs, mspecs, mkw = _mix_args(mix, bias, row)
    return pl.pallas_call(
        functools.partial(_ffn_kernel, final=True, **mkw),
        grid=(b, seq // tm),
        in_specs=[row(D_MODEL)] + mspecs + _ffn_wspecs(ly) + [_resident((1, D_MODEL))],
        out_specs=row(D_MODEL),
        out_shape=jax.ShapeDtypeStruct((b, seq, D_MODEL), F32),
        scratch_shapes=[pltpu.VMEM((tm, D_FF), BF16)],
        compiler_params=_cparams(2),
        name="ffn_final",
    )(h3, *margs, g, wg, wu, wd, fg)


def _rope(y, c, s, half):
    w = y.shape[1]
    reps = w // LANES
    if reps > 1:
        c = jnp.concatenate([c] * reps, axis=1)
        s = jnp.concatenate([s] * reps, axis=1)
    lane = lax.broadcasted_iota(jnp.int32, y.shape, 1)
    first = (lane & 63) < half
    partner = jnp.where(first, pltpu.roll(y, w - half, 1), pltpu.roll(y, half, 1))
    return y * c + partner * s


def _rope_tables(pos, rot_dim):
    half = rot_dim // 2
    g = np.arange(LANES) & 63
    inv = np.power(np.float32(ROPE_THETA), -(2 * (g % half)).astype(np.float32) / np.float32(rot_dim))
    ang = pos.astype(np.float32)[:, None] * inv[None, :].astype(np.float32)
    rotated = (g < rot_dim)[None, :]
    cos = np.where(rotated, np.cos(ang), np.float32(1.0)).astype(np.float32)
    sin = np.where(rotated, np.where((g < half)[None, :], -np.sin(ang), np.sin(ang)), np.float32(0.0))
    return jnp.asarray(cos), jnp.asarray(sin.astype(np.float32))


DIFF_COLS = DIFF_HEADS * 2 * DIFF_HEAD_DIM
MLA_QK_COLS = MLA_HEADS * 2 * LANES
MLA_V_COLS = MLA_HEADS * LANES
AB_SPLITS = [int(c) for c in np.cumsum([0, DIFF_COLS, DIFF_COLS, DIFF_COLS, MLA_Q_RANK, MLA_KV_RANK, LANES])]
AB_IN_PAD = AB_SPLITS[-1]


def _ab_proj_kernel(h_ref, g_ref, win_ref, qn_ref, wuq_ref, kvn_ref, wukv_ref, ca_ref, sa_ref, cm_ref, sm_ref,
                    qa_ref, ka_ref, va_ref, qb_ref, kb_ref, vb_ref):
    xn = _rms(h_ref[0], g_ref[...], EPS).astype(BF16)
    ca, sa, cm, sm = ca_ref[...], sa_ref[...], cm_ref[...], sm_ref[...]
    half_a = PART_ROT // 2
    half_m = MLA_ROPE // 2
    cols = [slice(a, b) for a, b in zip(AB_SPLITS[:-1], AB_SPLITS[1:])]
    qa = _rope(_dot(xn, win_ref[:, cols[0]]), ca, sa, half_a)
    qa_ref[0] = (qa * (DIFF_HEAD_DIM ** -0.5 * LOG2E)).astype(BF16)
    ka_ref[0] = _rope(_dot(xn, win_ref[:, cols[1]]), ca, sa, half_a).astype(BF16)
    va_ref[0] = _dot(xn, win_ref[:, cols[2]]).astype(BF16)
    cq = _dot(xn, win_ref[:, cols[3]])
    ckv = _dot(xn, win_ref[:, cols[4]])
    kr = _rope(_dot(xn, win_ref[:, cols[5]]), cm, sm, half_m).astype(BF16)
    qb = _dot(_rms(cq, qn_ref[...], EPS).astype(BF16), wuq_ref[...])
    scale = (MLA_NOPE + MLA_ROPE) ** -0.5 * LOG2E
    for hh in range(MLA_HEADS):
        o = 2 * LANES * hh
        qb_ref[0, :, o:o + LANES] = (qb[:, o:o + LANES] * scale).astype(BF16)
        qb_ref[0, :, o + LANES:o + 2 * LANES] = (_rope(qb[:, o + LANES:o + 2 * LANES], cm, sm, half_m) * scale).astype(BF16)
    kv = _dot(_rms(ckv, kvn_ref[...], EPS).astype(BF16), wukv_ref[...])
    for hh in range(MLA_HEADS):
        o = 2 * LANES * hh
        kb_ref[0, :, o:o + LANES] = kv[:, LANES * hh:LANES * (hh + 1)].astype(BF16)
        kb_ref[0, :, o + LANES:o + 2 * LANES] = kr
    vb_ref[0] = kv[:, MLA_HEADS * MLA_NOPE:].astype(BF16)


def _ab_proj(h3, g, win, qn, wuq, kvn, wukv, tabs):
    b, l, _ = h3.shape
    tm = PROJ_TM
    assert l % tm == 0
    row = lambda w: pl.BlockSpec((1, tm, w), lambda j, bb: (bb, j, 0))
    tab = pl.BlockSpec((tm, LANES), lambda j, bb: (j, 0))
    outw = (DIFF_COLS, DIFF_COLS, DIFF_COLS, MLA_QK_COLS, MLA_QK_COLS, MLA_V_COLS)
    return pl.pallas_call(
        _ab_proj_kernel,
        grid=(l // tm, b),
        in_specs=[row(D_MODEL), _resident((1, D_MODEL)), _resident((D_MODEL, AB_IN_PAD)),
                  _resident((1, MLA_Q_RANK)), _resident((MLA_Q_RANK, MLA_QK_COLS)),
                  _resident((1, MLA_KV_RANK)), _resident((MLA_KV_RANK, MLA_HEADS * MLA_NOPE + MLA_V_COLS)),
                  tab, tab, tab, tab],
        out_specs=[row(w) for w in outw],
        out_shape=[jax.ShapeDtypeStruct((b, l, w), BF16) for w in outw],
        compiler_params=_cparams(2),
        name="ab_proj",
    )(h3, g, win, qn, wuq, kvn, wukv, *tabs)


def _transpose(x):
    return x.T


def _store(o_ref, cols, val):
    for bb in range(o_ref.shape[0]):
        o_ref[bb, :, cols] = val


def _head_slices(c0, heads):
    cph, dk, dv = heads
    h = c0 // cph
    return slice(h * dk, (h + 1) * dk), slice(h * dv, (h + 1) * dv)


def _flash_step(k, vt, qt, m_ref, l_ref, acc_ref, mask, heads, between=None):
    cw = min(ATT_CW, heads[0])
    probs = {}
    for c0 in range(0, qt.shape[1], cw):
        cs = slice(c0, c0 + cw)
        ks, _ = _head_slices(c0, heads)
        s = _dot(k[:, ks], qt[:, cs])
        if mask is not None:
            s = jnp.where(mask(c0, s.shape), s, NEG_INF)
        m_new = jnp.max(s, axis=0, keepdims=True)
        p = jnp.exp2(s - m_new)
        l_ref[:, cs] = jnp.sum(p, axis=0, keepdims=True)
        m_ref[:, cs] = m_new
        probs[c0] = p.astype(BF16)
    if between is not None:
        between()
    for c0, p in probs.items():
        acc_ref[:, c0:c0 + cw] = _dot(vt[_head_slices(c0, heads)[1]], p)


def _produce(k, qt, s_ref, x_ref, chunks, heads):
    for c0, rows in chunks.items():
        cs = slice(c0, c0 + ATT_CW)
        s = _dot(k[:rows, _head_slices(c0, heads)[0]], qt[:, cs])
        s_ref[:rows, cs] = s
        if rows == k.shape[0]:
            x_ref[:, cs] = jnp.max(s, axis=0, keepdims=True)


def _consume(s_ref, x_ref, vt, m_ref, l_ref, acc_ref, chunks, heads, diag=None):
    tk = s_ref.shape[0]
    for c0 in chunks:
        cs = slice(c0, c0 + ATT_CW)
        if diag is None:
            seen = tk
        else:
            key0, tq, tri = diag
            seen = min((c0 & (tq - 1)) - key0, tk)
            if seen + ATT_CW <= 0:
                continue
        if seen == tk:
            s = s_ref[:, cs]
            smax = x_ref[:, cs]
        else:
            s = s_ref[seen:seen + ATT_CW, cs] + tri
            if seen > 0:
                s = jnp.concatenate([s_ref[0:seen, cs], s], axis=0)
            smax = jnp.max(s, axis=0, keepdims=True)
        m_prev = m_ref[:, cs]
        m_new = jnp.maximum(m_prev, smax)
        alpha = jnp.exp2(m_prev - m_new)
        p = jnp.exp2((s - m_new).astype(BF16))
        rows = s.shape[0]
        vt_h = vt[_head_slices(c0, heads)[1], :rows]
        pv = _dot(jnp.concatenate([vt_h, jnp.ones((16, rows), BF16)], axis=0), p)
        l_ref[:, cs] = alpha * l_ref[:, cs] + pv[LANES:LANES + 1]
        acc_ref[:, cs] = alpha * acc_ref[:, cs] + pv[:LANES]
        m_ref[:, cs] = m_new


def _causal_attend(make_qt, k_ref, v_ref, scr, *, tq, tk, seq, front, nh, lag):
    vt_ref, m_ref, l_ref, acc_ref, s0, s1, x0, x1 = scr
    state = (m_ref, l_ref, acc_ref)
    heads = (m_ref.shape[1] // nh, k_ref.shape[2] // nh, v_ref.shape[2] // nh)
    f0 = 0 if front else seq
    kf = k_ref[0, f0:f0 + BLOCK, :]

    def valid(c0, shape):
        return lax.broadcasted_iota(jnp.int32, shape, 0) >= N_FRONT

    def causal(key0):
        def mask(c0, shape):
            key = lax.broadcasted_iota(jnp.int32, shape, 0) + key0
            qry = (lax.broadcasted_iota(jnp.int32, shape, 1) + c0) & (tq - 1)
            return key <= qry
        return mask

    if front:
        qt = make_qt()
        both = lambda c0, shape: valid(c0, shape) & causal(0)(c0, shape)
        _flash_step(kf, _transpose(v_ref[0, f0:f0 + BLOCK, :]), qt, *state, both, heads)
        return
    assert tq == 2 * tk
    i = pl.program_id(2)
    nblk = v_ref.shape[1] // BLOCK
    sub = tk // BLOCK
    chunks = list(range(0, m_ref.shape[1], ATT_CW))

    @pl.when(i == 0)
    def _fill():
        def fill(c, carry):
            st = pl.multiple_of(c * BLOCK, BLOCK)
            vt_ref[c] = _transpose(v_ref[0, pl.ds(st, BLOCK), :])
            return carry
        lax.fori_loop(0, nblk, fill, 0, unroll=VT_FILL_UNROLL)

    qt = make_qt()

    def k_block(j):
        return k_ref[0, pl.ds(pl.multiple_of(j * tk, tk), tk), :]

    def vt_block(j):
        return jnp.concatenate([vt_ref[j * sub + u] for u in range(sub)], axis=1)

    groups = [chunks[g0:g0 + ATT_GROUP] for g0 in range(0, len(chunks), ATT_GROUP)]
    ng = len(groups)
    lag = min(lag, ng)
    bufs = ((s0, x0), (s1, x1))

    every = {c0: tk for c0 in chunks}

    def produce(kb, parity, g, allowed=every):
        _produce(kb, qt, *bufs[parity], {c0: allowed[c0] for c0 in groups[g] if c0 in allowed}, heads)

    def consume(vb, parity, g, diag=None):
        _consume(*bufs[parity], vb, *state, groups[g], heads, diag=diag)

    def run(blocks, ahead):
        nxt = [(kb, allowed) for kb, _, _, allowed in blocks] + list(ahead)
        for u in range(len(blocks) * ng):
            t = u + lag
            if t // ng < len(nxt):
                kb, allowed = nxt[t // ng]
                produce(kb, (t // ng) % 2, t % ng, allowed)
            _, vb, diag, _ = blocks[u // ng]
            consume(vb, (u // ng) % 2, u % ng, diag)

    meta = slice(seq + N_FRONT, seq + BLOCK)
    _flash_step(k_ref[0, meta, :], _transpose(v_ref[0, meta, :]), qt, *state, None, heads,
                between=lambda: [produce(k_block(0), 0, g) for g in range(lag)])

    def body(p, carry):
        run([(k_block(2 * p), vt_block(2 * p), None, every), (k_block(2 * p + 1), vt_block(2 * p + 1), None, every)],
            [(k_block(2 * p + 2), every)])
        return carry

    lax.fori_loop(0, i, body, 0)
    tri_shape = (ATT_CW, ATT_CW)
    tri = jnp.where(lax.broadcasted_iota(jnp.int32, tri_shape, 0) <= lax.broadcasted_iota(jnp.int32, tri_shape, 1),
                    0.0, NEG_INF)
    seen = lambda key0: {c0: min((c0 & (tq - 1)) + ATT_CW - key0, tk) for c0 in chunks
                         if (c0 & (tq - 1)) + ATT_CW > key0}
    run([(k_block(2 * i), vt_block(2 * i), (0, tq, tri), seen(0)),
         (k_block(2 * i + 1), vt_block(2 * i + 1), (tk, tq, tri), seen(tk))], [])


def _diff_kernel(q_ref, k_ref, v_ref, lq1_ref, lk1_ref, lq2_ref, lk2_ref, sub_ref, *rest,
                 tq, tk, seq, front, lambda_init, nh):
    if front:
        _, o_ref, *scr = rest
    else:
        o_ref, *scr = rest
    m_ref, l_ref, acc_ref = scr[1:4]

    def stacked_queries():
        t = _transpose(q_ref[0])
        d = lax.broadcasted_iota(jnp.int32, (LANES, tq), 0)
        zero = jnp.zeros((LANES, tq), t.dtype)
        cols = []
        for h in range(nh):
            th = t[h * LANES:(h + 1) * LANES]
            cols += [jnp.where(d < DIFF_HEAD_DIM, th, zero), jnp.where(d >= DIFF_HEAD_DIM, th, zero)]
        return jnp.concatenate(cols, axis=1)

    _causal_attend(stacked_queries, k_ref, v_ref, scr, tq=tq, tk=tk, seq=seq, front=front, nh=nh, lag=DIFF_LAG)
    o = acc_ref[...] / l_ref[...]
    lam = (jnp.exp(jnp.sum(lq1_ref[...] * lk1_ref[...], keepdims=True))
           - jnp.exp(jnp.sum(lq2_ref[...] * lk2_ref[...], keepdims=True)) + lambda_init)
    for h in range(nh):
        w = o[:, 2 * h * tq:(2 * h + 1) * tq] - lam * o[:, (2 * h + 1) * tq:(2 * h + 2) * tq]
        w = w * lax.rsqrt(jnp.mean(w * w, axis=0, keepdims=True) + DIFF_EPS) * sub_ref[...]
        _store(o_ref, slice(h * LANES, (h + 1) * LANES), (w * (1.0 - lambda_init)).T.astype(BF16))


def _mla_kernel(q_ref, k_ref, v_ref, *rest, tq, tk, seq, front, nh):
    if front:
        _, o_ref, *scr = rest
    else:
        o_ref, *scr = rest
    m_ref, l_ref, acc_ref = scr[1:4]
    dk = q_ref.shape[2] // nh

    def queries():
        t = _transpose(q_ref[0])
        return jnp.concatenate([t[h * dk:(h + 1) * dk] for h in range(nh)], axis=1)

    _causal_attend(queries, k_ref, v_ref, scr, tq=tq, tk=tk, seq=seq, front=front, nh=nh, lag=MLA_LAG)
    o = acc_ref[...] / l_ref[...]
    for h in range(nh):
        _store(o_ref, slice(h * LANES, (h + 1) * LANES), o[:, h * tq:(h + 1) * tq].T.astype(BF16))


def _causal_attention(kernel, q, k, v, extra, *, heads, dk, reps, tq, tk, seq, name, nh=1):
    b, l, _ = q.shape
    dv = LANES * nh
    dk = dk * nh
    reps = reps * nh
    heads = heads // nh
    outs = None
    for front in (False, True):
        t = BLOCK if front else tq
        if front:
            grid = (1, heads, 1)
            qmap = lambda bb, hh, i: (0, seq // BLOCK, hh)
        else:
            grid = (b, heads, seq // t)
            qmap = lambda bb, hh, i: (bb, i, hh)
        if front:
            kvl, kvmap = BLOCK, qmap
        else:
            kvl, kvmap = l, lambda bb, hh, i: (bb, 0, hh)
        in_specs = [pl.BlockSpec((1, t, dk), qmap), pl.BlockSpec((1, kvl, dk), kvmap),
                    pl.BlockSpec((1, kvl, dv), kvmap)]
        in_specs += [_resident(e.shape) for e in extra]
        args = [q, k, v, *extra]
        aliases = {}
        if front:
            in_specs.append(pl.BlockSpec(memory_space=pl.ANY))
            aliases = {len(args): 0}
            args.append(outs)
        outs = pl.pallas_call(
            functools.partial(kernel, tq=t, tk=min(t, tk), seq=seq, front=front),
            grid=grid,
            in_specs=in_specs,
            out_specs=pl.BlockSpec((b if front else 1, t, dv), qmap),
            out_shape=jax.ShapeDtypeStruct((b, l, heads * dv), BF16),
            scratch_shapes=[pltpu.VMEM((1 if front else l // BLOCK, dv, BLOCK), BF16),
                            pltpu.VMEM((1, reps * t), F32), pltpu.VMEM((1, reps * t), F32),
                            pltpu.VMEM((LANES, reps * t), F32)]
                           + [pltpu.VMEM((8 if front else tk, reps * t), F32)] * 2
                           + [pltpu.VMEM((1, reps * t), F32)] * 2,
            input_output_aliases=aliases,
            compiler_params=_cparams(3),
            name=name + ("_front" if front else ""),
        )(*args)
    return outs


SWA_Q_COLS = SWA_HEADS * SWA_HEAD_DIM
SWA_KV_COLS = SWA_KV_HEADS * SWA_HEAD_DIM
SWA_IN_COLS = SWA_Q_COLS + 2 * SWA_KV_COLS


def _swa_proj_kernel(h_ref, g_ref, w_ref, b_ref, ca_ref, sa_ref, q_ref, k_ref, v_ref):
    xn = _rms(h_ref[0], g_ref[...], EPS).astype(BF16)
    ca, sa = ca_ref[...], sa_ref[...]
    half = PART_ROT // 2
    nq = SWA_Q_COLS
    nk = SWA_KV_COLS
    q = _rope(_dot(xn, w_ref[:, 0:nq]) + b_ref[:, 0:nq], ca, sa, half)
    q_ref[0] = (q * (SWA_HEAD_DIM ** -0.5 * LOG2E)).astype(BF16)
    k_ref[0] = _rope(_dot(xn, w_ref[:, nq:nq + nk]) + b_ref[:, nq:nq + nk], ca, sa, half).astype(BF16)
    v_ref[0] = (_dot(xn, w_ref[:, nq + nk:]) + b_ref[:, nq + nk:]).astype(BF16)


def _swa_proj(h3, g, w, bias, tabs):
    b, l, _ = h3.shape
    tm = PROJ_TM
    row = lambda wd: pl.BlockSpec((1, tm, wd), lambda j, bb: (bb, j, 0))
    tab = pl.BlockSpec((tm, LANES), lambda j, bb: (j, 0))
    outw = (SWA_Q_COLS, SWA_KV_COLS, SWA_KV_COLS)
    return pl.pallas_call(
        _swa_proj_kernel,
        grid=(l // tm, b),
        in_specs=[row(D_MODEL), _resident((1, D_MODEL)), _resident((D_MODEL, SWA_IN_COLS)),
                  _resident((1, SWA_IN_COLS)), tab, tab],
        out_specs=[row(w_) for w_ in outw],
        out_shape=[jax.ShapeDtypeStruct((b, l, w_), BF16) for w_ in outw],
        compiler_params=_cparams(2),
        name="swa_proj",
    )(h3, g, w, bias, *tabs)


def _swa_kernel(sinks_ref, q_ref, k_ref, v_ref, *rest, tq, seq, front):
    if front:
        _, o_ref, s_a, s_b = rest
    else:
        o_ref, s_a, s_b = rest
    hd_dim = SWA_HEAD_DIM
    ts = min(SWA_TS, tq)
    pairs = SWA_GROUP // 2
    cw = 2 * ts
    f0 = 0 if front else seq
    meta_k = k_ref[0, f0 + N_FRONT:f0 + BLOCK, :]
    meta_v = v_ref[0, f0 + N_FRONT:f0 + BLOCK, :]
    band = 0 if front else ts + WINDOW
    nk = band + N_META
    nkp = -(-nk // LANES) * LANES
    qt = _transpose(q_ref[0])
    zh = jnp.zeros((hd_dim, cw), BF16)
    key = lax.broadcasted_iota(jnp.int32, (nk, ts), 0)
    qry = lax.broadcasted_iota(jnp.int32, (nk, ts), 1)
    first_half = lax.broadcasted_iota(jnp.int32, (1, cw), 1) < ts

    def keys_of(u):
        if front:
            kc, vc = meta_k, meta_v
            allowed = key + N_FRONT <= qry
        else:
            start = pl.program_id(1) * tq + u * ts
            bs = pl.multiple_of(jnp.maximum(start - WINDOW, 0), BLOCK)
            kc = jnp.concatenate([k_ref[0, pl.ds(bs, band), :], meta_k], axis=0)
            vc = jnp.concatenate([v_ref[0, pl.ds(bs, band), :], meta_v], axis=0)
            dist = (start + qry) - (bs + key)
            allowed = (key >= band) | ((dist >= 0) & (dist < WINDOW))
        bias = jnp.where(allowed, 0.0, NEG_INF)
        vt = _transpose(jnp.concatenate([vc, jnp.zeros((nkp - nk, vc.shape[1]), vc.dtype)], axis=0))
        return kc, vt, jnp.concatenate([bias, bias], axis=1)

    sets = [(u, g) for u in range(tq // ts) for g in range(SWA_KV_HEADS)]
    operands = {}
    sbuf = (s_a, s_b)

    def produce(n, pp):
        u, g = sets[n]
        if u not in operands:
            operands[u] = keys_of(u)
        hd = g * SWA_GROUP + 2 * pp
        qh = jnp.concatenate([qt[(hd + a) * hd_dim:(hd + a + 1) * hd_dim, u * ts:(u + 1) * ts] for a in (0, 1)],
                             axis=1)
        rhs = jnp.concatenate([qh, zh] if g == 0 else [zh, qh], axis=0)
        sbuf[n % 2][:, pp * cw:(pp + 1) * cw] = _dot(operands[u][0], rhs)

    outs = {}

    def consume(n, pp):
        u, g = sets[n]
        _, vt, bias = operands[u]
        hd = g * SWA_GROUP + 2 * pp
        sink = jnp.where(first_half, sinks_ref[hd] * LOG2E, sinks_ref[hd + 1] * LOG2E)
        s = sbuf[n % 2][:, pp * cw:(pp + 1) * cw] + bias
        m = jnp.maximum(jnp.max(s, axis=0, keepdims=True), sink)
        e = jnp.exp2(s - m)
        den = jnp.sum(e, axis=0, keepdims=True) + jnp.exp2(sink - m)
        eb = jnp.concatenate([e.astype(BF16), jnp.zeros((nkp - nk, cw), BF16)], axis=0)
        o = _dot(vt[g * hd_dim:(g + 1) * hd_dim], eb) / den
        outs[hd, u] = o[:, :ts]
        outs[hd + 1, u] = o[:, ts:]

    for pp in range(pairs):
        produce(0, pp)
    for n in range(1, len(sets)):
        for pp in range(pairs):
            produce(n, pp)
            consume(n - 1, pp)
    for pp in range(pairs):
        consume(len(sets) - 1, pp)
    rows = [jnp.concatenate([outs[hd, u] for u in range(tq // ts)], axis=1) for hd in range(SWA_HEADS)]
    _store(o_ref, slice(None), jnp.concatenate(rows, axis=0).T.astype(BF16))


def _swa_attention(q, k, v, sinks, *, seq):
    b, l, _ = q.shape
    nq = SWA_Q_COLS
    outs = None
    for front in (False, True):
        tq = BLOCK if front else SWA_TQ
        ts = min(SWA_TS, tq)
        nk = N_META if front else ts + WINDOW + N_META
        if front:
            grid = (1, 1)
            qmap = lambda bb, i: (0, seq // BLOCK, 0)
            kvl, kvmap = BLOCK, qmap
        else:
            grid = (b, seq // tq)
            qmap = lambda bb, i: (bb, i, 0)
            kvl, kvmap = l, lambda bb, i: (bb, 0, 0)
        in_specs = [pl.BlockSpec(memory_space=pltpu.SMEM), pl.BlockSpec((1, tq, nq), qmap),
                    pl.BlockSpec((1, kvl, k.shape[2]), kvmap), pl.BlockSpec((1, kvl, v.shape[2]), kvmap)]
        args = [sinks, q, k, v]
        aliases = {}
        if front:
            in_specs.append(pl.BlockSpec(memory_space=pl.ANY))
            aliases = {len(args): 0}
            args.append(outs)
        outs = pl.pallas_call(
            functools.partial(_swa_kernel, tq=tq, seq=seq, front=front),
            grid=grid,
            in_specs=in_specs,
            out_specs=pl.BlockSpec((b if front else 1, tq, nq), qmap),
            out_shape=jax.ShapeDtypeStruct((b, l, nq), BF16),
            scratch_shapes=[pltpu.VMEM((nk, SWA_GROUP * ts), F32)] * 2,
            input_output_aliases=aliases,
            compiler_params=_cparams(2),
            name="swa_attn" + ("_front" if front else ""),
        )(*args)
    return outs


def _ab_weights(w_in, w_uq, w_ukv):
    win = jnp.pad(w_in, ((0, 0), (0, AB_IN_PAD - w_in.shape[1]))).astype(BF16)
    wq = w_uq.reshape(MLA_Q_RANK, MLA_HEADS, MLA_NOPE + MLA_ROPE)
    wq = jnp.pad(wq, ((0, 0), (0, 0), (0, 2 * LANES - MLA_NOPE - MLA_ROPE))).reshape(MLA_Q_RANK, MLA_HEADS * 2 * LANES)
    wkv = w_ukv.reshape(MLA_KV_RANK, MLA_HEADS, 2, MLA_NOPE).transpose(0, 2, 1, 3).reshape(MLA_KV_RANK, -1)
    return win, wq.astype(BF16), wkv.astype(BF16)


def kernel(x, meta_tokens, ffn1_norm, ffn1_w_gate, ffn1_w_up, ffn1_w_down, mix_norm, ab_w_in, diff_lambda_q1, diff_lambda_k1, diff_lambda_q2, diff_lambda_k2, diff_subln, mla_q_norm, mla_w_uq, mla_kv_norm, mla_w_ukv, ab_w_out, swa_w_qkv, swa_b_qkv, swa_sinks, swa_w_out, swa_b_out, ffn2_norm, ffn2_w_gate, ffn2_w_up, ffn2_w_down, final_norm):
    b, seq, d = x.shape
    depth = ffn1_norm.shape[0]
    l = seq + BLOCK
    n = b * l
    front = jnp.concatenate([jnp.zeros((N_FRONT, d), x.dtype), meta_tokens.astype(x.dtype)], axis=0)

    pos = np.concatenate([np.arange(seq) + N_META, np.maximum(np.arange(BLOCK) - N_FRONT, 0)])
    tabs_p = _rope_tables(pos, PART_ROT)
    tabs_m = _rope_tables(pos, MLA_ROPE)
    row2 = lambda a: a.reshape(1, -1)

    ffn1 = (ffn1_w_gate, ffn1_w_up, ffn1_w_down.astype(BF16))
    ffn2 = (ffn2_w_gate, ffn2_w_up, ffn2_w_down.astype(BF16))
    out = None
    for ly in range(depth):
        w1 = (row2(ffn1_norm[ly]), *ffn1)
        h = _ffn_first(x, front, *w1).reshape(n, d) if ly == 0 else _ffn(h, *w1, ly)
        h3 = h.reshape(b, l, d)
        if ly % 2 == 0:
            e = ly // 2
            lambda_init = 0.8 - 0.6 * math.exp(-0.3 * ly)
            win, wuq, wukv = _ab_weights(ab_w_in[e], mla_w_uq[e], mla_w_ukv[e])
            qa, ka, va, qb, kb, vb = _ab_proj(h3, row2(mix_norm[ly]), win, row2(mla_q_norm[e]), wuq,
                                              row2(mla_kv_norm[e]), wukv, tabs_p + tabs_m)
            extra = [row2(diff_lambda_q1[e]), row2(diff_lambda_k1[e]), row2(diff_lambda_q2[e]),
                     row2(diff_lambda_k2[e]), diff_subln[e].reshape(-1, 1)]
            oa = _causal_attention(functools.partial(_diff_kernel, lambda_init=lambda_init, nh=DIFF_HEADS_PER_STEP),
                                   qa, ka, va, extra, heads=DIFF_HEADS, dk=LANES, reps=2, tq=ATT_TQ, tk=ATT_TK,
                                   seq=seq, name="diff_attn", nh=DIFF_HEADS_PER_STEP)
            ob = _causal_attention(functools.partial(_mla_kernel, nh=MLA_HEADS_PER_STEP), qb, kb, vb, [],
                                   heads=MLA_HEADS, dk=2 * LANES, reps=1, tq=ATT_TQ, tk=ATT_TK, seq=seq,
                                   name="mla_attn", nh=MLA_HEADS_PER_STEP)
            wo = ab_w_out[e].astype(BF16)
            mix, bias = [(oa, wo[:DIFF_COLS]), (ob, wo[DIFF_COLS:])], None
        else:
            o = ly // 2
            q, k, v = _swa_proj(h3, row2(mix_norm[ly]), swa_w_qkv[o].astype(BF16), row2(swa_b_qkv[o]), tabs_p)
            att = _swa_attention(q, k, v, swa_sinks[o], seq=seq)
            mix, bias = [(att, swa_w_out[o].astype(BF16))], row2(swa_b_out[o])
        w2 = (row2(ffn2_norm[ly]), *ffn2)
        if ly == depth - 1:
            out = _ffn_final(h3, *w2, ly, row2(final_norm), seq, mix=mix, bias=bias)
        else:
            h = _ffn(h, *w2, ly, mix=[(a.reshape(n, -1), w) for a, w in mix], bias=bias)
    return out
```
